```python
import math
import jax, jax.numpy as jnp
from jax import lax
import numpy as np


D_MODEL = 1024
BATCH = 16
SEQ = 2048
DEPTH = 1

D_MIX = D_MODEL
HEAD_DIM = 64
N_ATTN_HEADS = 8
N_KV_GROUPS = 2
HEADS_PER_GROUP = N_ATTN_HEADS // N_KV_GROUPS
D_ATTN = N_ATTN_HEADS * HEAD_DIM
D_KV = N_KV_GROUPS * HEAD_DIM
N_BRANCH = 3
D_CONV = D_MIX - D_ATTN
N_CONV_GROUPS = 8
CONV_WIDTH = 3
CMP_BLOCK = 32
CMP_STRIDE = 16
CMP_HIDDEN = 256
SEL_BLOCK = 64
SEL_TOPK = 16
N_LOCAL_FORCED = 2
WINDOW = 512
Q_CHUNK = 32
N_BUCKETS = 32
MAX_DISTANCE = 128
D_FF = -(-(8 * D_MODEL // 3) // 256) * 256
D_IN_PROJ = D_ATTN + 6 * D_KV + N_BRANCH * N_ATTN_HEADS + 3 * D_CONV
EPS = 1e-6
NEG_INF = -1e30
FORCED_SCORE = 1e6

kernel_name = "hybrid_nsa_shortconv_adaln_layer"


def rms_norm(x, gain):
    x32 = x.astype(jnp.float32)
    y = x32 * lax.rsqrt(jnp.mean(x32 * x32, axis=-1, keepdims=True) + EPS)
    return y.astype(x.dtype) * gain


def t5_bucket(rel):
    n = jnp.maximum(rel, 0)
    max_exact = N_BUCKETS // 2
    nf = jnp.maximum(n, 1).astype(jnp.float32)
    large = max_exact + (jnp.log(nf / max_exact) / math.log(MAX_DISTANCE / max_exact)
                         * (N_BUCKETS - max_exact)).astype(jnp.int32)
    return jnp.where(n < max_exact, n, jnp.minimum(large, N_BUCKETS - 1))


def head_bias(table, rel):
    b = table[t5_bucket(rel)]
    return jnp.moveaxis(b, -1, 0).reshape((N_KV_GROUPS, HEADS_PER_GROUP) + tuple(rel.shape))


def masked_softmax(logits, mask):
    logits = jnp.where(mask, logits.astype(jnp.float32), NEG_INF)
    p = jax.nn.softmax(logits, axis=-1)
    return p * jnp.any(mask, axis=-1, keepdims=True)


def nsa_mixer(q, k_c, v_c, k_s, v_s, k_w, v_w, gate_logits, q_gain, k_cmp_gain, k_sel_gain,
              k_win_gain, cmp_pos_k, cmp_pos_v, w_ck1, w_ck2, w_cv1, w_cv2, rel_bias_table):
    bsz, seq = q.shape[0], q.shape[1]
    G, HPG, DH = N_KV_GROUPS, HEADS_PER_GROUP, HEAD_DIM
    scale = DH ** -0.5
    t_pos = np.arange(seq)

    q = rms_norm(q.reshape(bsz, seq, G, HPG, DH), q_gain).transpose(0, 2, 3, 1, 4)

    def kv_heads(a):
        return a.reshape(bsz, seq, G, DH).transpose(0, 2, 1, 3)

    n_cmp = (seq - CMP_BLOCK) // CMP_STRIDE + 1
    cmp_start = np.arange(n_cmp) * CMP_STRIDE
    cmp_end = cmp_start + CMP_BLOCK - 1
    cmp_idx = cmp_start[:, None] + np.arange(CMP_BLOCK)[None, :]

    def compress(a, pos_emb, w1, w2):
        blocks = kv_heads(a)[:, :, cmp_idx] + pos_emb
        flat = blocks.reshape(bsz, G, n_cmp, CMP_BLOCK * DH)
        return jnp.dot(jax.nn.silu(jnp.dot(flat, w1)), w2)

    kc = rms_norm(compress(k_c, cmp_pos_k, w_ck1, w_ck2), k_cmp_gain)
    vc = compress(v_c, cmp_pos_v, w_cv1, w_cv2)
    rel_c = t_pos[:, None] - cmp_end[None, :]
    logits_c = jnp.einsum('bghsd,bgnd->bghsn', q, kc) * scale + head_bias(rel_bias_table, rel_c)
    p_c = masked_softmax(logits_c, jnp.asarray(rel_c >= 0))
    o_c = jnp.einsum('bghsn,bgnd->bghsd', p_c.astype(vc.dtype), vc)

    n_sel = seq // SEL_BLOCK
    top_k = min(SEL_TOPK, n_sel)
    sel_start = np.arange(n_sel) * SEL_BLOCK
    overlap = np.clip(np.minimum(cmp_end[:, None], sel_start[None, :] + SEL_BLOCK - 1)
                      - np.maximum(cmp_start[:, None], sel_start[None, :]) + 1, 0, None) / CMP_STRIDE
    p_slc = jnp.einsum('bghsn,nj->bgsj', p_c, jnp.asarray(overlap, jnp.float32))
    dist_blk = (t_pos // SEL_BLOCK)[:, None] - np.arange(n_sel)[None, :]
    valid = dist_blk >= 0
    forced = (np.arange(n_sel)[None, :] == 0) | (valid & (dist_blk < N_LOCAL_FORCED))
    score = jnp.where(valid, jnp.where(forced, FORCED_SCORE, p_slc), NEG_INF)
    sel_idx = lax.top_k(score, top_k)[1]

    k_blocks = rms_norm(kv_heads(k_s), k_sel_gain).reshape(bsz, G, n_sel, SEL_BLOCK, DH)
    v_blocks = kv_heads(v_s).reshape(bsz, G, n_sel, SEL_BLOCK, DH)
    tab_g = rel_bias_table.reshape(N_BUCKETS, G, HPG)
    b_ar = jnp.arange(bsz)[:, None, None, None]
    g_ar = jnp.arange(G)[None, :, None, None]
    n_tok = top_k * SEL_BLOCK

    pad = ((0, 0), (0, 0), (WINDOW, 0), (0, 0))
    kwin = jnp.pad(rms_norm(kv_heads(k_w), k_win_gain), pad)
    vwin = jnp.pad(kv_heads(v_w), pad)
    span = WINDOW + Q_CHUNK
    rel_w = WINDOW + np.arange(Q_CHUNK)[:, None] - np.arange(span)[None, :]
    bias_w = head_bias(rel_bias_table, rel_w)

    def chunk(ci):
        s0 = ci * Q_CHUNK
        t_q = s0 + jnp.arange(Q_CHUNK)
        qc = lax.dynamic_slice_in_dim(q, s0, Q_CHUNK, axis=3)
        idx = lax.dynamic_slice_in_dim(sel_idx, s0, Q_CHUNK, axis=2)
        kg = k_blocks[b_ar, g_ar, idx].reshape(bsz, G, Q_CHUNK, n_tok, DH)
        vg = v_blocks[b_ar, g_ar, idx].reshape(bsz, G, Q_CHUNK, n_tok, DH)
        key_pos = (idx[..., None] * SEL_BLOCK + jnp.arange(SEL_BLOCK)).reshape(bsz, G, Q_CHUNK, n_tok)
        rel_s = t_q[None, None, :, None] - key_pos
        bias_s = jnp.moveaxis(tab_g[t5_bucket(rel_s), g_ar], -1, 2)
        logits_s = jnp.einsum('bghqd,bgqkd->bghqk', qc, kg) * scale + bias_s
        p_s = masked_softmax(logits_s, (rel_s >= 0)[:, :, None])
        o_s = jnp.einsum('bghqk,bgqkd->bghqd', p_s.astype(vg.dtype), vg)
        kw = lax.dynamic_slice_in_dim(kwin, s0, span, axis=2)
        vw = lax.dynamic_slice_in_dim(vwin, s0, span, axis=2)
        key_pos_w = s0 - WINDOW + jnp.arange(span)
        rel = t_q[:, None] - key_pos_w[None, :]
        mask_w = (rel >= 0) & (rel < WINDOW) & (key_pos_w[None, :] >= 0)
        logits_w = jnp.einsum('bghqd,bgkd->bghqk', qc, kw) * scale + bias_w
        p_w = masked_softmax(logits_w, mask_w)
        o_w = jnp.einsum('bghqk,bgkd->bghqd', p_w.astype(vw.dtype), vw)
        return o_s, o_w

    o_s, o_w = lax.map(chunk, jnp.arange(seq // Q_CHUNK))

    def unchunk(o):
        return jnp.moveaxis(o, 0, 3).reshape(bsz, G, HPG, seq, DH)

    gates = jax.nn.sigmoid(gate_logits.reshape(bsz, seq, G, HPG, N_BRANCH)).transpose(0, 2, 3, 1, 4)
    o = gates[..., 0:1] * o_c + gates[..., 1:2] * unchunk(o_s) + gates[..., 2:3] * unchunk(o_w)
    return o.transpose(0, 3, 1, 2, 4).reshape(bsz, seq, D_ATTN)


def short_conv_mixer(b_gate, c_gate, xt, conv_w):
    seq = xt.shape[1]
    u = c_gate * xt
    u_pad = jnp.pad(u, ((0, 0), (CONV_WIDTH - 1, 0), (0, 0)))
    conv = sum(u_pad[:, k:k + seq] * conv_w[k] for k in range(CONV_WIDTH))
    return b_gate * conv


def swiglu(h, w1, w3, w2):
    return jnp.dot(jax.nn.silu(jnp.dot(h, w1)) * jnp.dot(h, w3), w2)


def setup_inputs(seed: int = 0) -> dict:
    key = jax.random.key(seed)
    ks = jax.random.split(key, 25)
    L = DEPTH

    def nrm(k, shape, s):
        return jax.random.normal(k, shape, jnp.float32) * s

    def gain(k, shape):
        return 1.0 + nrm(k, shape, 0.02)

    return {
        "x": nrm(ks[0], (BATCH, SEQ, D_MODEL), 1.0),
        "c": nrm(ks[1], (BATCH, D_MODEL), 1.0),
        "w_ada": nrm(ks[2], (L, D_MODEL, 6 * D_MODEL), 0.5 * D_MODEL ** -0.5),
        "b_ada": nrm(ks[3], (L, 6 * D_MODEL), 0.02),
        "norm1_gain": gain(ks[4], (L, D_MODEL)),
        "w_in": nrm(ks[5], (L, D_MODEL, D_IN_PROJ), D_MODEL ** -0.5),
        "q_gain": gain(ks[6], (L, HEAD_DIM)),
        "k_cmp_gain": gain(ks[7], (L, HEAD_DIM)),
        "k_sel_gain": gain(ks[8], (L, HEAD_DIM)),
        "k_win_gain": gain(ks[9], (L, HEAD_DIM)),
        "cmp_pos_k": nrm(ks[10], (L, CMP_BLOCK, HEAD_DIM), 0.1),
        "cmp_pos_v": nrm(ks[11], (L, CMP_BLOCK, HEAD_DIM), 0.1),
        "w_ck1": nrm(ks[12], (L, CMP_BLOCK * HEAD_DIM, CMP_HIDDEN), (CMP_BLOCK * HEAD_DIM) ** -0.5),
        "w_ck2": nrm(ks[13], (L, CMP_HIDDEN, HEAD_DIM), CMP_HIDDEN ** -0.5),
        "w_cv1": nrm(ks[14], (L, CMP_BLOCK * HEAD_DIM, CMP_HIDDEN), (CMP_BLOCK * HEAD_DIM) ** -0.5),
        "w_cv2": nrm(ks[15], (L, CMP_HIDDEN, HEAD_DIM), CMP_HIDDEN ** -0.5),
        "rel_bias_table": nrm(ks[16], (N_BUCKETS, N_ATTN_HEADS), 0.5),
        "conv_w": nrm(ks[17], (L, CONV_WIDTH, D_CONV), CONV_WIDTH ** -0.5),
        "attn_out_gain": gain(ks[18], (L, D_ATTN)),
        "conv_out_gain": gain(ks[19], (L, D_CONV)),
        "w_out": nrm(ks[20], (L, D_MIX, D_MODEL), D_MIX ** -0.5),
        "norm2_gain": gain(ks[21], (L, D_MODEL)),
        "w_ff1": nrm(ks[22], (L, D_MODEL, D_FF), D_MODEL ** -0.5),
        "w_ff3": nrm(ks[23], (L, D_MODEL, D_FF), D_MODEL ** -0.5),
        "w_ff2": nrm(ks[24], (L, D_FF, D_MODEL), D_FF ** -0.5),
    }


def reference(x, c, w_ada, b_ada, norm1_gain, w_in, q_gain, k_cmp_gain, k_sel_gain, k_win_gain,
              cmp_pos_k, cmp_pos_v, w_ck1, w_ck2, w_cv1, w_cv2, rel_bias_table, conv_w,
              attn_out_gain, conv_out_gain, w_out, norm2_gain, w_ff1, w_ff3, w_ff2):
    split_points = np.cumsum([D_ATTN] + [D_KV] * 6 + [N_BRANCH * N_ATTN_HEADS] + [D_CONV] * 3)[:-1].tolist()
    for layer in range(DEPTH):
        mod = jnp.dot(jax.nn.silu(c), w_ada[layer]) + b_ada[layer]
        shift1, scale1, gate1, shift2, scale2, gate2 = jnp.split(mod[:, None, :], 6, axis=-1)

        h = rms_norm(x, norm1_gain[layer]) * (1 + scale1) + shift1
        proj = jnp.dot(h, w_in[layer])
        q, k_c, v_c, k_s, v_s, k_w, v_w, gate_logits, b_gate, c_gate, xt = jnp.split(proj, split_points, axis=-1)
        y_attn = nsa_mixer(q, k_c, v_c, k_s, v_s, k_w, v_w, gate_logits, q_gain[layer],
                           k_cmp_gain[layer], k_sel_gain[layer], k_win_gain[layer],
                           cmp_pos_k[layer], cmp_pos_v[layer], w_ck1[layer], w_ck2[layer],
                           w_cv1[layer], w_cv2[layer], rel_bias_table)
        y_conv = short_conv_mixer(b_gate, c_gate, xt, conv_w[layer])
        y = jnp.concatenate([rms_norm(y_attn, attn_out_gain[layer]),
                             rms_norm(y_conv, conv_out_gain[layer])], axis=-1)
        x = x + gate1 * jnp.dot(y, w_out[layer])

        h2 = rms_norm(x, norm2_gain[layer]) * (1 + scale2) + shift2
        x = x + gate2 * swiglu(h2, w_ff1[layer], w_ff3[layer], w_ff2[layer])
    return x
```

```python
import functools
import math

import numpy as np
import jax
import jax.numpy as jnp
from jax import lax
from jax.experimental import pallas as pl
from jax.experimental.pallas import tpu as pltpu

F32 = jnp.float32
BF16 = jnp.bfloat16

D_MODEL = 1024
HEAD_DIM = 64
N_HEADS = 8
N_GROUPS = 2
HPG = N_HEADS // N_GROUPS
D_ATTN = N_HEADS * HEAD_DIM
D_KV = N_GROUPS * HEAD_DIM
N_BRANCH = 3
D_CONV = D_MODEL - D_ATTN
CONV_WIDTH = 3
CMP_BLOCK = 32
CMP_STRIDE = 16
CMP_HIDDEN = 256
SEL_BLOCK = 64
SEL_TOPK = 16
N_LOCAL_FORCED = 2
WINDOW = 512
N_BUCKETS = 32
MAX_DISTANCE = 128
EPS = 1e-6
NEG_INF = -1e30
FORCED_SCORE = 1e6

LANES = 128
TQ = 256
ROWS = HPG * TQ
N_CMP_PAD = 128
SEL_LANE0 = 64
VMEM_LIMIT = 56 * 1024 * 1024

_NT = (((1,), (1,)), ((), ()))


def _dot(a, b):
    return jnp.dot(a, b, preferred_element_type=F32)


def _silu(v):
    return v * (1.0 / (1.0 + jnp.exp(-v)))


def _sigmoid(v):
    return 1.0 / (1.0 + jnp.exp(-v))


def _seg_mean_sq(v, seg):
    sq = v * v
    hi = sq.astype(BF16)
    lo = (sq - hi.astype(F32)).astype(BF16)
    return _dot(hi, seg) + _dot(lo, seg)


def _adaln_kernel(c_ref, w_ref, b_ref, o_ref):
    sc = _silu(c_ref[...]).astype(BF16)
    o_ref[...] = _dot(sc, w_ref[...].astype(BF16)) + b_ref[...]


def _adaln(c, w_ada, b_ada):
    bsz, d = c.shape
    n = w_ada.shape[1]
    tn = 1536
    return pl.pallas_call(
        _adaln_kernel,
        grid=(n // tn,),
        in_specs=[pl.BlockSpec((bsz, d), lambda j: (0, 0)),
                  pl.BlockSpec((d, tn), lambda j: (0, j)),
                  pl.BlockSpec((1, tn), lambda j: (0, j))],
        out_specs=pl.BlockSpec((bsz, tn), lambda j: (0, j)),
        out_shape=jax.ShapeDtypeStruct((bsz, n), F32),
        compiler_params=pltpu.CompilerParams(
            dimension_semantics=("arbitrary",), vmem_limit_bytes=VMEM_LIMIT),
        name="adaln",
    )(c, w_ada, b_ada.reshape(1, n))


C_Q, C_KC, C_KS, C_VS, C_KW, C_VW, C_G0, C_BG, C_CG, C_XT, C_END = (
    0, 512, 768, 896, 1024, 1152, 1280, 1536, 2048, 2560, 3072)
TS_IN = 512


def _inproj_kernel(x_ref, mod_ref, g1_ref, w_ref, seg_ref, gq_ref, gks_ref, gkw_ref,
                   q_ref, kvc_ref, ks_ref, vs_ref, kw_ref, vw_ref, gate_ref, bg_ref, u_ref):
    ts = x_ref.shape[1]
    x = x_ref[0]
    ms = jnp.mean(x * x, axis=-1, keepdims=True)
    y = x * lax.rsqrt(ms + EPS) * g1_ref[...]
    h = y * (1.0 + mod_ref[0, 1:2, :]) + mod_ref[0, 0:1, :]
    hb = h.astype(BF16)
    seg = seg_ref[...]

    def proj(c0, c1):
        return _dot(hb, w_ref[:, c0:c1])

    zeros64 = jnp.zeros((ts, HEAD_DIM), F32)
    ones64 = jnp.ones((ts, HEAD_DIM), F32)

    for cpair in range(N_HEADS // 2):
        v = proj(C_Q + cpair * LANES, C_Q + (cpair + 1) * LANES)
        vn = v * lax.rsqrt(_seg_mean_sq(v, seg) + EPS) * gq_ref[...] * (HEAD_DIM ** -0.5)
        for half in range(2):
            head = cpair * 2 + half
            qh = jnp.concatenate([vn[:, half * HEAD_DIM:(half + 1) * HEAD_DIM], zeros64], axis=-1)
            q_ref[0, head] = qh.astype(BF16)

    kvc_ref[0, 0] = proj(C_KC, C_KC + D_KV)
    kvc_ref[0, 1] = proj(C_KC + D_KV, C_KS)

    row = pl.program_id(1) * ts + lax.broadcasted_iota(jnp.int32, (ts, HEAD_DIM), 0)
    lane = lax.broadcasted_iota(jnp.int32, (ts, HEAD_DIM), 1)
    onehot = jnp.where(lane == row // SEL_BLOCK, 1.0, 0.0).astype(F32)
    v = proj(C_KS, C_VS)
    vn = v * lax.rsqrt(_seg_mean_sq(v, seg) + EPS) * gks_ref[...]
    for g in range(N_GROUPS):
        ks_ref[0, g] = jnp.concatenate(
            [vn[:, g * HEAD_DIM:(g + 1) * HEAD_DIM], onehot], axis=-1).astype(BF16)
    v = proj(C_VS, C_KW)
    for g in range(N_GROUPS):
        vs_ref[0, g] = jnp.concatenate(
            [v[:, g * HEAD_DIM:(g + 1) * HEAD_DIM], ones64], axis=-1).astype(BF16)
    v = proj(C_KW, C_VW)
    vn = v * lax.rsqrt(_seg_mean_sq(v, seg) + EPS) * gkw_ref[...]
    for g in range(N_GROUPS):
        kw_ref[0, g] = jnp.concatenate(
            [vn[:, g * HEAD_DIM:(g + 1) * HEAD_DIM], zeros64], axis=-1).astype(BF16)
    v = proj(C_VW, C_G0)
    for g in range(N_GROUPS):
        vw_ref[0, g] = jnp.concatenate(
            [v[:, g * HEAD_DIM:(g + 1) * HEAD_DIM], ones64], axis=-1).astype(BF16)

    for g in range(N_GROUPS):
        gate_ref[0, g] = _sigmoid(proj(C_G0 + g * LANES, C_G0 + (g + 1) * LANES))

    bg_ref[0] = proj(C_BG, C_CG).astype(BF16)
    u_ref[0] = (proj(C_CG, C_XT) * proj(C_XT, C_END)).astype(BF16)


def _inproj(x, mod3, g1, w_p, seg, gq2, gks2, gkw2):
    bsz, seq, d = x.shape
    ts = TS_IN
    const2 = lambda b, i: (0, 0)
    kv_shape = jax.ShapeDtypeStruct((bsz, N_GROUPS, seq, LANES), BF16)
    kv_spec = pl.BlockSpec((1, N_GROUPS, ts, LANES), lambda b, i: (b, 0, i, 0))
    return pl.pallas_call(
        _inproj_kernel,
        grid=(bsz, seq // ts),
        in_specs=[pl.BlockSpec((1, ts, d), lambda b, i: (b, i, 0)),
                  pl.BlockSpec((1, 6, d), lambda b, i: (b, 0, 0)),
                  pl.BlockSpec((1, d), const2),
                  pl.BlockSpec(w_p.shape, const2, pipeline_mode=pl.Buffered(1)),
                  pl.BlockSpec((LANES, LANES), const2),
                  pl.BlockSpec((1, LANES), const2),
                  pl.BlockSpec((1, LANES), const2),
                  pl.BlockSpec((1, LANES), const2)],
        out_specs=[pl.BlockSpec((1, N_HEADS, ts, LANES), lambda b, i: (b, 0, i, 0)),
                   pl.BlockSpec((1, 2, ts, D_KV), lambda b, i: (b, 0, i, 0)),
                   kv_spec, kv_spec, kv_spec, kv_spec,
                   pl.BlockSpec((1, N_GROUPS, ts, LANES), lambda b, i: (b, 0, i, 0)),
                   pl.BlockSpec((1, ts, D_CONV), lambda b, i: (b, i, 0)),
                   pl.BlockSpec((1, ts, D_CONV), lambda b, i: (b, i, 0))],
        out_shape=[jax.ShapeDtypeStruct((bsz, N_HEADS, seq, LANES), BF16),
                   jax.ShapeDtypeStruct((bsz, 2, seq, D_KV), F32),
                   kv_shape, kv_shape, kv_shape, kv_shape,
                   jax.ShapeDtypeStruct((bsz, N_GROUPS, seq, LANES), F32),
                   jax.ShapeDtypeStruct((bsz, seq, D_CONV), BF16),
                   jax.ShapeDtypeStruct((bsz, seq, D_CONV), BF16)],
        compiler_params=pltpu.CompilerParams(
            dimension_semantics=("arbitrary", "arbitrary"), vmem_limit_bytes=VMEM_LIMIT),
        name="inproj",
    )(x, mod3, g1, w_p, seg, gq2, gks2, gkw2)


def _compress_kernel(kvc_ref, w1k_ref, w1v_ref, w2k_ref, w2v_ref, pk_ref, pv_ref, seg_ref,
                     gk_ref, kct_ref, vc_ref):
    n_rows = kvc_ref.shape[2] // CMP_STRIDE

    def branch(which, w1_ref, w2_ref, pos_ref):
        p1 = jnp.zeros((n_rows, N_GROUPS * CMP_HIDDEN), F32)
        p2 = jnp.zeros((n_rows, N_GROUPS * CMP_HIDDEN), F32)
        for r in range(CMP_STRIDE):
            rows = kvc_ref[0, which, pl.ds(r, n_rows, stride=CMP_STRIDE), :]
            a1 = (rows + pos_ref[r:r + 1, :]).astype(BF16)
            a2 = (rows + pos_ref[CMP_STRIDE + r:CMP_STRIDE + r + 1, :]).astype(BF16)
            p1 = p1 + _dot(a1, w1_ref[r])
            p2 = p2 + _dot(a2, w1_ref[CMP_STRIDE + r])
        hid = p1 + pltpu.roll(p2, n_rows - 1, 0)
        return _dot(_silu(hid).astype(BF16), w2_ref[...])

    kc = branch(0, w1k_ref, w2k_ref, pk_ref)
    kc = kc * lax.rsqrt(_seg_mean_sq(kc, seg_ref[...]) + EPS) * gk_ref[...]
    kct = kc.T
    zpad = jnp.zeros((HEAD_DIM, n_rows), F32)
    vcv = branch(1, w1v_ref, w2v_ref, pv_ref)
    zlane = jnp.zeros((n_rows, HEAD_DIM), F32)
    for g in range(N_GROUPS):
        kct_ref[0, g] = jnp.concatenate(
            [kct[g * HEAD_DIM:(g + 1) * HEAD_DIM, :], zpad], axis=0).astype(BF16)
        vc_ref[0, g] = jnp.concatenate(
            [vcv[:, g * HEAD_DIM:(g + 1) * HEAD_DIM], zlane], axis=-1).astype(BF16)


def _compress(kvc, w1k, w1v, w2k, w2v, pk2, pv2, seg, gk2):
    bsz, _, seq, _ = kvc.shape
    c3 = lambda b: (0, 0, 0)
    c2 = lambda b: (0, 0)
    out_shape = jax.ShapeDtypeStruct((bsz, N_GROUPS, N_CMP_PAD, LANES), BF16)
    out_spec = pl.BlockSpec((1, N_GROUPS, N_CMP_PAD, LANES), lambda b: (b, 0, 0, 0))
    return pl.pallas_call(
        _compress_kernel,
        grid=(bsz,),
        in_specs=[pl.BlockSpec((1, 2, seq, D_KV), lambda b: (b, 0, 0, 0)),
                  pl.BlockSpec(w1k.shape, c3), pl.BlockSpec(w1v.shape, c3),
                  pl.BlockSpec(w2k.shape, c2), pl.BlockSpec(w2v.shape, c2),
                  pl.BlockSpec(pk2.shape, c2), pl.BlockSpec(pv2.shape, c2),
                  pl.BlockSpec(seg.shape, c2), pl.BlockSpec(gk2.shape, c2)],
        out_specs=[out_spec, out_spec],
        out_shape=[out_shape, out_shape],
        compiler_params=pltpu.CompilerParams(
            dimension_semantics=("arbitrary",), vmem_limit_bytes=VMEM_LIMIT),
        name="compress",
    )(kvc, w1k, w1v, w2k, w2v, pk2, pv2, seg, gk2)


def _online_update(qa, k_tile, v_tile, bias, acc_ref, m_ref):
    s = lax.dot_general(qa, k_tile, _NT, preferred_element_type=F32)
    if bias is not None:
        s = s + bias
    m_prev = m_ref[...]
    m_new = jnp.maximum(m_prev, jnp.max(s, axis=-1, keepdims=True))
    alpha = jnp.exp(m_prev - m_new)
    p = jnp.exp(s - jnp.concatenate([m_new] * (TQ // LANES), axis=-1))
    acc_ref[...] = acc_ref[...] * alpha + _dot(p.astype(BF16), v_tile)
    m_ref[...] = m_new


def _attn_kernel(q_ref, kct_ref, vc_ref, ks_ref, vs_ref, kw_ref, vw_ref, gate_ref,
                 cb_ref, b0_ref, b1_ref, w2_ref, ovl_ref, o_ref,
                 qa_ref, acc_s, m_s, acc_w, m_w):
    qi = pl.program_id(2)
    q = q_ref[0].reshape(ROWS, LANES)

    s = _dot(q, kct_ref[0, 0]) + cb_ref[0, 0]
    m = jnp.max(s, axis=-1, keepdims=True)
    e = jnp.exp(s - m)
    l = jnp.sum(e, axis=-1, keepdims=True)
    pc = e * jnp.where(m > 0.5 * NEG_INF, 1.0 / l, 0.0)
    o_c = _dot(pc.astype(BF16), vc_ref[0, 0])

    psum = pc[0:TQ] + pc[TQ:2 * TQ] + pc[2 * TQ:3 * TQ] + pc[3 * TQ:4 * TQ]
    hi = psum.astype(BF16)
    lo = (psum - hi.astype(F32)).astype(BF16)
    pslc = _dot(hi, ovl_ref[...]) + _dot(lo, ovl_ref[...])

    lane = lax.broadcasted_iota(jnp.int32, (TQ, LANES), 1)
    j = lane - SEL_LANE0
    t = qi * TQ + lax.broadcasted_iota(jnp.int32, (TQ, LANES), 0)
    dist = t // SEL_BLOCK - j
    score = jnp.where(dist < N_LOCAL_FORCED, FORCED_SCORE, pslc)
    score = jnp.where(j == 0, FORCED_SCORE, score)
    score = jnp.where(dist >= 0, score, NEG_INF)
    n_sel = ks_ref.shape[2] // SEL_BLOCK
    cnt = jnp.zeros((TQ, LANES), F32)
    for i in range(n_sel):
        col = jnp.broadcast_to(score[:, SEL_LANE0 + i:SEL_LANE0 + i + 1], (TQ, LANES))
        beats = jnp.where(col > score, 1.0, jnp.where(col == score, jnp.where(j > i, 1.0, 0.0), 0.0))
        cnt = cnt + beats
    top_k = min(SEL_TOPK, n_sel)
    selbias = jnp.where(cnt < top_k, 0.0, NEG_INF)
    selbias = jnp.where(j >= 0, jnp.where(j < n_sel, selbias, 0.0), 0.0)
    for h in range(HPG):
        qa_ref[h * TQ:(h + 1) * TQ, :] = (q_ref[0, h].astype(F32) + selbias).astype(BF16)
    qa = qa_ref[...]

    def key_tile(ref, kt):
        return ref[0, 0, pl.ds(pl.multiple_of(kt * TQ, TQ), TQ), :]

    m_s[...] = jnp.full((ROWS, LANES), NEG_INF, F32)
    acc_s[...] = jnp.zeros((ROWS, LANES), F32)

    def sel_body(kt, carry):
        _online_update(qa, key_tile(ks_ref, kt), key_tile(vs_ref, kt), None, acc_s, m_s)
        return carry

    lax.fori_loop(0, qi - 1, sel_body, 0)

    @pl.when(qi >= 1)
    def _():
        _online_update(qa, key_tile(ks_ref, qi - 1), key_tile(vs_ref, qi - 1), b1_ref[0], acc_s, m_s)

    _online_update(qa, key_tile(ks_ref, qi), key_tile(vs_ref, qi), b0_ref[0], acc_s, m_s)

    m_w[...] = jnp.full((ROWS, LANES), NEG_INF, F32)
    acc_w[...] = jnp.zeros((ROWS, LANES), F32)

    @pl.when(qi >= 2)
    def _():
        _online_update(q, key_tile(kw_ref, qi - 2), key_tile(vw_ref, qi - 2), w2_ref[...], acc_w, m_w)

    @pl.when(qi >= 1)
    def _():
        _online_update(q, key_tile(kw_ref, qi - 1), key_tile(vw_ref, qi - 1), b1_ref[0], acc_w, m_w)

    _online_update(q, key_tile(kw_ref, qi), key_tile(vw_ref, qi), b0_ref[0], acc_w, m_w)

    a_s = acc_s[...]
    a_w = acc_w[...]
    o_s = a_s / pltpu.roll(a_s, HEAD_DIM, 1)
    o_w = a_w / pltpu.roll(a_w, HEAD_DIM, 1)
    gates = gate_ref[0, 0]
    outs = []
    for h in range(HPG):
        rows = slice(h * TQ, (h + 1) * TQ)
        g_c = gates[:, h * N_BRANCH + 0:h * N_BRANCH + 1]
        g_s = gates[:, h * N_BRANCH + 1:h * N_BRANCH + 2]
        g_w = gates[:, h * N_BRANCH + 2:h * N_BRANCH + 3]
        o_h = g_c * o_c[rows] + g_s * o_s[rows] + g_w * o_w[rows]
        outs.append(o_h[:, :HEAD_DIM])
    o_ref[0] = jnp.concatenate(outs, axis=-1)


def _attn(q, kct, vc, ks, vs, kw, vw, gate, cb, b0, b1, w2, ovl):
    bsz, _, seq, _ = q.shape
    nq = seq // TQ
    kv_spec = pl.BlockSpec((1, 1, seq, LANES), lambda g, b, i: (b, g, 0, 0))
    cmp_spec = pl.BlockSpec((1, 1, N_CMP_PAD, LANES), lambda g, b, i: (b, g, 0, 0))
    return pl.pallas_call(
        _attn_kernel,
        grid=(N_GROUPS, bsz, nq),
        in_specs=[pl.BlockSpec((1, HPG, TQ, LANES), lambda g, b, i: (b, g, i, 0)),
                  cmp_spec, cmp_spec, kv_spec, kv_spec, kv_spec, kv_spec,
                  pl.BlockSpec((1, 1, TQ, LANES), lambda g, b, i: (b, g, i, 0)),
                  pl.BlockSpec((1, 1, ROWS, N_CMP_PAD), lambda g, b, i: (g, i, 0, 0)),
                  pl.BlockSpec((1, ROWS, TQ), lambda g, b, i: (g, 0, 0)),
                  pl.BlockSpec((1, ROWS, TQ), lambda g, b, i: (g, 0, 0)),
                  pl.BlockSpec((ROWS, TQ), lambda g, b, i: (0, 0)),
                  pl.BlockSpec((N_CMP_PAD, LANES), lambda g, b, i: (0, 0))],
        out_specs=pl.BlockSpec((1, TQ, HPG * HEAD_DIM), lambda g, b, i: (b, i, g)),
        out_shape=jax.ShapeDtypeStruct((bsz, seq, D_ATTN), F32),
        scratch_shapes=[pltpu.VMEM((ROWS, LANES), BF16),
                        pltpu.VMEM((ROWS, LANES), F32), pltpu.VMEM((ROWS, LANES), F32),
                        pltpu.VMEM((ROWS, LANES), F32), pltpu.VMEM((ROWS, LANES), F32)],
        compiler_params=pltpu.CompilerParams(
            dimension_semantics=("arbitrary", "arbitrary", "arbitrary"),
            vmem_limit_bytes=VMEM_LIMIT),
        name="attn",
    )(q, kct, vc, ks, vs, kw, vw, gate, cb, b0, b1, w2, ovl)


TS_OUT = 512
HALO = 16


def _outproj_kernel(ya_ref, bg_ref, u_ref, uh_ref, cw_ref, ga_ref, gc_ref, w_ref, x_ref, mod_ref, o_ref):
    ts = u_ref.shape[1]
    u = u_ref[0].astype(F32)
    halo = uh_ref[0].astype(F32) * jnp.where(pl.program_id(1) > 0, 1.0, 0.0)
    h1 = halo[HALO - 1:HALO, :]
    h2 = halo[HALO - 2:HALO - 1, :]
    row = lax.broadcasted_iota(jnp.int32, u.shape, 0)
    u1 = jnp.where(row == 0, h1, pltpu.roll(u, 1, 0))
    u2 = jnp.where(row == 0, h2, jnp.where(row == 1, h1, pltpu.roll(u, 2, 0)))
    conv = u2 * cw_ref[0:1, :] + u1 * cw_ref[1:2, :] + u * cw_ref[2:3, :]
    yc = bg_ref[0].astype(F32) * conv
    yc = yc * lax.rsqrt(jnp.mean(yc * yc, axis=-1, keepdims=True) + EPS) * gc_ref[...]
    ya = ya_ref[0]
    ya = ya * lax.rsqrt(jnp.mean(ya * ya, axis=-1, keepdims=True) + EPS) * ga_ref[...]
    proj = _dot(ya.astype(BF16), w_ref[0:D_ATTN, :]) + _dot(yc.astype(BF16), w_ref[D_ATTN:, :])
    o_ref[0] = x_ref[0] + mod_ref[0, 2:3, :] * proj


def _outproj(ya, bg, u, conv_w, ga, gc, w_out, x, mod3):
    bsz, seq, d = x.shape
    ts = TS_OUT
    c2 = lambda b, i: (0, 0)
    row_spec = lambda width: pl.BlockSpec((1, ts, width), lambda b, i: (b, i, 0))
    return pl.pallas_call(
        _outproj_kernel,
        grid=(bsz, seq // ts),
        in_specs=[row_spec(D_ATTN), row_spec(D_CONV), row_spec(D_CONV),
                  pl.BlockSpec((1, HALO, D_CONV),
                               lambda b, i: (b, jnp.maximum(i * (ts // HALO) - 1, 0), 0)),
                  pl.BlockSpec(conv_w.shape, c2),
                  pl.BlockSpec((1, D_ATTN), c2), pl.BlockSpec((1, D_CONV), c2),
                  pl.BlockSpec(w_out.shape, c2, pipeline_mode=pl.Buffered(1)),
                  row_spec(d),
                  pl.BlockSpec((1, 6, d), lambda b, i: (b, 0, 0))],
        out_specs=row_spec(d),
        out_shape=jax.ShapeDtypeStruct((bsz, seq, d), F32),
        compiler_params=pltpu.CompilerParams(
            dimension_semantics=("arbitrary", "arbitrary"), vmem_limit_bytes=VMEM_LIMIT),
        name="outproj",
    )(ya, bg, u, u, conv_w, ga, gc, w_out, x, mod3)


TS_FFN = 512


def _ffn_kernel(x_ref, mod_ref, g2_ref, w1_ref, w3_ref, w2_ref, o_ref):
    x = x_ref[0]
    ms = jnp.mean(x * x, axis=-1, keepdims=True)
    h = x * lax.rsqrt(ms + EPS) * g2_ref[...] * (1.0 + mod_ref[0, 4:5, :]) + mod_ref[0, 3:4, :]
    hb = h.astype(BF16)
    a = _dot(hb, w1_ref[...])
    b = _dot(hb, w3_ref[...])
    act = (_silu(a) * b).astype(BF16)
    o_ref[0] = x + mod_ref[0, 5:6, :] * _dot(act, w2_ref[...])


def _ffn(x, mod3, g2, w1, w3, w2):
    bsz, seq, d = x.shape
    ts = TS_FFN
    c2 = lambda b, i: (0, 0)
    wspec = lambda w: pl.BlockSpec(w.shape, c2, pipeline_mode=pl.Buffered(1))
    return pl.pallas_call(
        _ffn_kernel,
        grid=(bsz, seq // ts),
        in_specs=[pl.BlockSpec((1, ts, d), lambda b, i: (b, i, 0)),
                  pl.BlockSpec((1, 6, d), lambda b, i: (b, 0, 0)),
                  pl.BlockSpec((1, d), c2),
                  wspec(w1), wspec(w3), wspec(w2)],
        out_specs=pl.BlockSpec((1, ts, d), lambda b, i: (b, i, 0)),
        out_shape=jax.ShapeDtypeStruct((bsz, seq, d), F32),
        compiler_params=pltpu.CompilerParams(
            dimension_semantics=("arbitrary", "arbitrary"), vmem_limit_bytes=VMEM_LIMIT),
        name="ffn",
    )(x, mod3, g2, w1, w3, w2)


def _t5_bucket_np(rel):
    n = np.maximum(rel, 0)
    max_exact = N_BUCKETS // 2
    nf = np.maximum(n, 1).astype(np.float32)
    large = max_exact + (np.log(nf / max_exact) / math.log(MAX_DISTANCE / max_exact)
                         * (N_BUCKETS - max_exact)).astype(np.int32)
    return np.where(n < max_exact, n, np.minimum(large, N_BUCKETS - 1)).astype(np.int32)


def _stack_heads(tab_rel, idx, mask):
    vals = jnp.moveaxis(tab_rel[idx], -1, 0)
    vals = jnp.where(mask, vals, NEG_INF)
    vals = vals.reshape((N_GROUPS, HPG) + vals.shape[1:])
    vals = jnp.moveaxis(vals, 1, -3)
    return vals.reshape(vals.shape[:-3] + (HPG * vals.shape[-2], vals.shape[-1]))


def _bias_tables(rel_bias_table, seq):
    tab_rel = rel_bias_table - rel_bias_table[N_BUCKETS - 1:N_BUCKETS, :]
    i = np.arange(TQ)[:, None]
    jj = np.arange(TQ)[None, :]
    rel0 = i - jj
    b0 = _stack_heads(tab_rel, _t5_bucket_np(rel0), rel0 >= 0)
    rel1 = TQ + i - jj
    b1 = _stack_heads(tab_rel, _t5_bucket_np(rel1), rel1 < WINDOW + TQ)
    w2 = jnp.asarray(np.tile(np.where(jj > i, 0.0, NEG_INF).astype(np.float32), (HPG, 1)))
    nq = seq // TQ
    n_cmp = (seq - CMP_BLOCK) // CMP_STRIDE + 1
    t = np.arange(seq)[:, None]
    n = np.arange(N_CMP_PAD)[None, :]
    relc = t - (n * CMP_STRIDE + CMP_BLOCK - 1)
    maskc = (relc >= 0) & (n < n_cmp)
    cb = _stack_heads(tab_rel, _t5_bucket_np(relc).reshape(nq, TQ, N_CMP_PAD),
                      maskc.reshape(nq, TQ, N_CMP_PAD))
    n_sel = seq // SEL_BLOCK
    cs = np.arange(n_cmp) * CMP_STRIDE
    ce = cs + CMP_BLOCK - 1
    ss = np.arange(n_sel) * SEL_BLOCK
    ov = np.clip(np.minimum(ce[:, None], ss[None, :] + SEL_BLOCK - 1)
                 - np.maximum(cs[:, None], ss[None, :]) + 1, 0, None) / CMP_STRIDE
    ovl = np.zeros((N_CMP_PAD, LANES), np.float32)
    ovl[:n_cmp, SEL_LANE0:SEL_LANE0 + n_sel] = ov
    return cb, b0, b1, w2, jnp.asarray(ovl, BF16)


def _blockdiag2(w):
    z = jnp.zeros_like(w)
    return jnp.concatenate([jnp.concatenate([w, z], axis=-1),
                            jnp.concatenate([z, w], axis=-1)], axis=-2)


def kernel(x, c, w_ada, b_ada, norm1_gain, w_in, q_gain, k_cmp_gain, k_sel_gain, k_win_gain,
           cmp_pos_k, cmp_pos_v, w_ck1, w_ck2, w_cv1, w_cv2, rel_bias_table, conv_w,
           attn_out_gain, conv_out_gain, w_out, norm2_gain, w_ff1, w_ff3, w_ff2):
    bsz, seq, d = x.shape
    assert d == D_MODEL and seq % TQ == 0 and seq // SEL_BLOCK <= LANES - SEL_LANE0 - 32
    assert (seq - CMP_BLOCK) // CMP_STRIDE + 1 <= N_CMP_PAD and seq // CMP_STRIDE == N_CMP_PAD

    seg = jnp.asarray(np.kron(np.eye(2), np.ones((HEAD_DIM, HEAD_DIM))) / HEAD_DIM, BF16)
    cb, b0, b1, w2m, ovl = _bias_tables(rel_bias_table, seq)
    two = lambda g: jnp.tile(g.reshape(1, HEAD_DIM), (1, 2))
    n_gate = N_BRANCH * HPG
    gate_pad = jnp.zeros((d, LANES - n_gate), F32)

    for layer in range(w_in.shape[0]):
        wi = w_in[layer]
        o_g = D_ATTN + 6 * D_KV
        w_p = jnp.concatenate(
            [wi[:, :o_g], wi[:, o_g:o_g + n_gate], gate_pad,
             wi[:, o_g + n_gate:o_g + 2 * n_gate], gate_pad, wi[:, o_g + 2 * n_gate:]],
            axis=-1).astype(BF16)
        w1k = _blockdiag2(w_ck1[layer].reshape(CMP_BLOCK, HEAD_DIM, CMP_HIDDEN)).astype(BF16)
        w1v = _blockdiag2(w_cv1[layer].reshape(CMP_BLOCK, HEAD_DIM, CMP_HIDDEN)).astype(BF16)
        w2k = _blockdiag2(w_ck2[layer]).astype(BF16)
        w2v = _blockdiag2(w_cv2[layer]).astype(BF16)
        pk2 = jnp.tile(cmp_pos_k[layer], (1, 2))
        pv2 = jnp.tile(cmp_pos_v[layer], (1, 2))

        mod3 = _adaln(c, w_ada[layer], b_ada[layer]).reshape(bsz, 6, d)
        q, kvc, ks, vs, kw, vw, gate, bg, u = _inproj(
            x, mod3, norm1_gain[layer].reshape(1, d), w_p, seg,
            two(q_gain[layer]), two(k_sel_gain[layer]), two(k_win_gain[layer]))
        kct, vc = _compress(kvc, w1k, w1v, w2k, w2v, pk2, pv2, seg, two(k_cmp_gain[layer]))
        y_attn = _attn(q, kct, vc, ks, vs, kw, vw, gate, cb, b0, b1, w2m, ovl)
        x = _outproj(y_attn, bg, u, conv_w[layer], attn_out_gain[layer].reshape(1, D_ATTN),
                     conv_out_gain[layer].reshape(1, D_CONV), w_out[layer].astype(BF16), x, mod3)
        x = _ffn(x, mod3, norm2_gain[layer].reshape(1, d), w_ff1[layer].astype(BF16),
                 w_ff3[layer].astype(BF16), w_ff2[layer].astype(BF16))
    return x
```

```python
import functools
import math

import numpy as np
import jax
import jax.numpy as jnp
from jax import lax
from jax.experimental import pallas as pl
from jax.experimental.pallas import tpu as pltpu

F32 = jnp.float32
BF16 = jnp.bfloat16

D_MODEL = 1024
HEAD_DIM = 64
N_HEADS = 8
N_GROUPS = 2
HPG = N_HEADS // N_GROUPS
D_ATTN = N_HEADS * HEAD_DIM
D_KV = N_GROUPS * HEAD_DIM
N_BRANCH = 3
D_CONV = D_MODEL - D_ATTN
CONV_WIDTH = 3
CMP_BLOCK = 32
CMP_STRIDE = 16
CMP_HIDDEN = 256
SEL_BLOCK = 64
SEL_TOPK = 16
N_LOCAL_FORCED = 2
WINDOW = 512
N_BUCKETS = 32
MAX_DISTANCE = 128
EPS = 1e-6
NEG_INF = -1e30
FORCED_SCORE = 1e6

LANES = 128
TQ = 256
ROWS = HPG * TQ
N_CMP_PAD = 128
SEL_LANE0 = 64
VMEM_LIMIT = 56 * 1024 * 1024

_NT = (((1,), (1,)), ((), ()))


def _dot(a, b):
    return jnp.dot(a, b, preferred_element_type=F32)


def _silu(v):
    return v * (1.0 / (1.0 + jnp.exp(-v)))


def _sigmoid(v):
    return 1.0 / (1.0 + jnp.exp(-v))


def _seg_mean_sq(v, seg):
    sq = v * v
    hi = sq.astype(BF16)
    lo = (sq - hi.astype(F32)).astype(BF16)
    return _dot(hi, seg) + _dot(lo, seg)


def _adaln_kernel(c_ref, w_ref, b_ref, o_ref):
    sc = _silu(c_ref[...]).astype(BF16)
    o_ref[...] = _dot(sc, w_ref[...].astype(BF16)) + b_ref[...]


def _adaln(c, w_ada, b_ada):
    bsz, d = c.shape
    n = w_ada.shape[1]
    tn = 1536
    return pl.pallas_call(
        _adaln_kernel,
        grid=(n // tn,),
        in_specs=[pl.BlockSpec((bsz, d), lambda j: (0, 0)),
                  pl.BlockSpec((d, tn), lambda j: (0, j)),
                  pl.BlockSpec((1, tn), lambda j: (0, j))],
        out_specs=pl.BlockSpec((bsz, tn), lambda j: (0, j)),
        out_shape=jax.ShapeDtypeStruct((bsz, n), F32),
        compiler_params=pltpu.CompilerParams(
            dimension_semantics=("arbitrary",), vmem_limit_bytes=VMEM_LIMIT),
        name="adaln",
    )(c, w_ada, b_ada.reshape(1, n))


C_Q, C_KC, C_KS, C_VS, C_KW, C_VW, C_G0, C_BG, C_CG, C_XT, C_END = (
    0, 512, 768, 896, 1024, 1152, 1280, 1536, 2048, 2560, 3072)
TS_IN = 512


def _inproj_kernel(x_ref, mod_ref, g1_ref, w_ref, seg_ref, gq_ref, gks_ref, gkw_ref,
                   q_ref, kvc_ref, ks_ref, vs_ref, kw_ref, vw_ref, gate_ref, bg_ref, u_ref):
    ts = x_ref.shape[1]
    x = x_ref[0]
    ms = jnp.mean(x * x, axis=-1, keepdims=True)
    y = x * lax.rsqrt(ms + EPS) * g1_ref[...]
    h = y * (1.0 + mod_ref[0, 1:2, :]) + mod_ref[0, 0:1, :]
    hb = h.astype(BF16)
    seg = seg_ref[...]

    def proj(c0, c1):
        return _dot(hb, w_ref[:, c0:c1])

    zeros64 = jnp.zeros((ts, HEAD_DIM), F32)
    ones64 = jnp.ones((ts, HEAD_DIM), F32)

    for cpair in range(N_HEADS // 2):
        v = proj(C_Q + cpair * LANES, C_Q + (cpair + 1) * LANES)
        vn = v * lax.rsqrt(_seg_mean_sq(v, seg) + EPS) * gq_ref[...] * (HEAD_DIM ** -0.5)
        for half in range(2):
            head = cpair * 2 + half
            qh = jnp.concatenate([vn[:, half * HEAD_DIM:(half + 1) * HEAD_DIM], zeros64], axis=-1)
            q_ref[0, head] = qh.astype(BF16)

    kvc_ref[0, 0] = proj(C_KC, C_KC + D_KV)
    kvc_ref[0, 1] = proj(C_KC + D_KV, C_KS)

    row = pl.program_id(1) * ts + lax.broadcasted_iota(jnp.int32, (ts, HEAD_DIM), 0)
    lane = lax.broadcasted_iota(jnp.int32, (ts, HEAD_DIM), 1)
    onehot = jnp.where(lane == row // SEL_BLOCK, 1.0, 0.0).astype(F32)
    v = proj(C_KS, C_VS)
    vn = v * lax.rsqrt(_seg_mean_sq(v, seg) + EPS) * gks_ref[...]
    for g in range(N_GROUPS):
        ks_ref[0, g] = jnp.concatenate(
            [vn[:, g * HEAD_DIM:(g + 1) * HEAD_DIM], onehot], axis=-1).astype(BF16)
    v = proj(C_VS, C_KW)
    for g in range(N_GROUPS):
        vs_ref[0, g] = jnp.concatenate(
            [v[:, g * HEAD_DIM:(g + 1) * HEAD_DIM], ones64], axis=-1).astype(BF16)
    v = proj(C_KW, C_VW)
    vn = v * lax.rsqrt(_seg_mean_sq(v, seg) + EPS) * gkw_ref[...]
    for g in range(N_GROUPS):
        kw_ref[0, g] = jnp.concatenate(
            [vn[:, g * HEAD_DIM:(g + 1) * HEAD_DIM], zeros64], axis=-1).astype(BF16)
    v = proj(C_VW, C_G0)
    for g in range(N_GROUPS):
        vw_ref[0, g] = jnp.concatenate(
            [v[:, g * HEAD_DIM:(g + 1) * HEAD_DIM], ones64], axis=-1).astype(BF16)

    for g in range(N_GROUPS):
        gate_ref[0, g] = _sigmoid(proj(C_G0 + g * LANES, C_G0 + (g + 1) * LANES))

    bg_ref[0] = proj(C_BG, C_CG).astype(BF16)
    u_ref[0] = (proj(C_CG, C_XT) * proj(C_XT, C_END)).astype(BF16)


def _inproj(x, mod3, g1, w_p, seg, gq2, gks2, gkw2):
    bsz, seq, d = x.shape
    ts = TS_IN
    const2 = lambda b, i: (0, 0)
    kv_shape = jax.ShapeDtypeStruct((bsz, N_GROUPS, seq, LANES), BF16)
    kv_spec = pl.BlockSpec((1, N_GROUPS, ts, LANES), lambda b, i: (b, 0, i, 0))
    return pl.pallas_call(
        _inproj_kernel,
        grid=(bsz, seq // ts),
        in_specs=[pl.BlockSpec((1, ts, d), lambda b, i: (b, i, 0)),
                  pl.BlockSpec((1, 6, d), lambda b, i: (b, 0, 0)),
                  pl.BlockSpec((1, d), const2),
                  pl.BlockSpec(w_p.shape, const2, pipeline_mode=pl.Buffered(1)),
                  pl.BlockSpec((LANES, LANES), const2),
                  pl.BlockSpec((1, LANES), const2),
                  pl.BlockSpec((1, LANES), const2),
                  pl.BlockSpec((1, LANES), const2)],
        out_specs=[pl.BlockSpec((1, N_HEADS, ts, LANES), lambda b, i: (b, 0, i, 0)),
                   pl.BlockSpec((1, 2, ts, D_KV), lambda b, i: (b, 0, i, 0)),
                   kv_spec, kv_spec, kv_spec, kv_spec,
                   pl.BlockSpec((1, N_GROUPS, ts, LANES), lambda b, i: (b, 0, i, 0)),
                   pl.BlockSpec((1, ts, D_CONV), lambda b, i: (b, i, 0)),
                   pl.BlockSpec((1, ts, D_CONV), lambda b, i: (b, i, 0))],
        out_shape=[jax.ShapeDtypeStruct((bsz, N_HEADS, seq, LANES), BF16),
                   jax.ShapeDtypeStruct((bsz, 2, seq, D_KV), F32),
                   kv_shape, kv_shape, kv_shape, kv_shape,
                   jax.ShapeDtypeStruct((bsz, N_GROUPS, seq, LANES), F32),
                   jax.ShapeDtypeStruct((bsz, seq, D_CONV), BF16),
                   jax.ShapeDtypeStruct((bsz, seq, D_CONV), BF16)],
        compiler_params=pltpu.CompilerParams(
            dimension_semantics=("arbitrary", "arbitrary"), vmem_limit_bytes=VMEM_LIMIT),
        name="inproj",
    )(x, mod3, g1, w_p, seg, gq2, gks2, gkw2)


def _compress_kernel(kvc_ref, w1k_ref, w1v_ref, w2k_ref, w2v_ref, pk_ref, pv_ref, seg_ref,
                     gk_ref, kct_ref, vc_ref):
    n_rows = kvc_ref.shape[2] // CMP_STRIDE

    def branch(which, w1_ref, w2_ref, pos_ref):
        p1 = jnp.zeros((n_rows, N_GROUPS * CMP_HIDDEN), F32)
        p2 = jnp.zeros((n_rows, N_GROUPS * CMP_HIDDEN), F32)
        for r in range(CMP_STRIDE):
            rows = kvc_ref[0, which, pl.ds(r, n_rows, stride=CMP_STRIDE), :]
            a1 = (rows + pos_ref[r:r + 1, :]).astype(BF16)
            a2 = (rows + pos_ref[CMP_STRIDE + r:CMP_STRIDE + r + 1, :]).astype(BF16)
            p1 = p1 + _dot(a1, w1_ref[r])
            p2 = p2 + _dot(a2, w1_ref[CMP_STRIDE + r])
        hid = p1 + pltpu.roll(p2, n_rows - 1, 0)
        return _dot(_silu(hid).astype(BF16), w2_ref[...])

    kc = branch(0, w1k_ref, w2k_ref, pk_ref)
    kc = kc * lax.rsqrt(_seg_mean_sq(kc, seg_ref[...]) + EPS) * gk_ref[...]
    kct = kc.T
    zpad = jnp.zeros((HEAD_DIM, n_rows), F32)
    vcv = branch(1, w1v_ref, w2v_ref, pv_ref)
    zlane = jnp.zeros((n_rows, HEAD_DIM), F32)
    for g in range(N_GROUPS):
        kct_ref[0, g] = jnp.concatenate(
            [kct[g * HEAD_DIM:(g + 1) * HEAD_DIM, :], zpad], axis=0).astype(BF16)
        vc_ref[0, g] = jnp.concatenate(
            [vcv[:, g * HEAD_DIM:(g + 1) * HEAD_DIM], zlane], axis=-1).astype(BF16)


def _compress(kvc, w1k, w1v, w2k, w2v, pk2, pv2, seg, gk2):
    bsz, _, seq, _ = kvc.shape
    c3 = lambda b: (0, 0, 0)
    c2 = lambda b: (0, 0)
    out_shape = jax.ShapeDtypeStruct((bsz, N_GROUPS, N_CMP_PAD, LANES), BF16)
    out_spec = pl.BlockSpec((1, N_GROUPS, N_CMP_PAD, LANES), lambda b: (b, 0, 0, 0))
    return pl.pallas_call(
        _compress_kernel,
        grid=(bsz,),
        in_specs=[pl.BlockSpec((1, 2, seq, D_KV), lambda b: (b, 0, 0, 0)),
                  pl.BlockSpec(w1k.shape, c3), pl.BlockSpec(w1v.shape, c3),
                  pl.BlockSpec(w2k.shape, c2), pl.BlockSpec(w2v.shape, c2),
                  pl.BlockSpec(pk2.shape, c2), pl.BlockSpec(pv2.shape, c2),
                  pl.BlockSpec(seg.shape, c2), pl.BlockSpec(gk2.shape, c2)],
        out_specs=[out_spec, out_spec],
        out_shape=[out_shape, out_shape],
        compiler_params=pltpu.CompilerParams(
            dimension_semantics=("arbitrary",), vmem_limit_bytes=VMEM_LIMIT),
        name="compress",
    )(kvc, w1k, w1v, w2k, w2v, pk2, pv2, seg, gk2)


def _online_update(qa, k_tile, v_tile, bias, acc_ref, m_ref):
    s = lax.dot_general(qa, k_tile, _NT, preferred_element_type=F32)
    if bias is not None:
        s = s + bias
    m_prev = m_ref[...]
    m_new = jnp.maximum(m_prev, jnp.max(s, axis=-1, keepdims=True))
    alpha = jnp.exp(m_prev - m_new)
    p = jnp.exp(s - jnp.concatenate([m_new] * (TQ // LANES), axis=-1))
    acc_ref[...] = acc_ref[...] * alpha + _dot(p.astype(BF16), v_tile)
    m_ref[...] = m_new


def _attn_kernel(q_ref, kct_ref, vc_ref, ks_ref, vs_ref, kw_ref, vw_ref, gate_ref,
                 cb_ref, b0_ref, b1_ref, w2_ref, ovl_ref, o_ref,
                 qa_ref, acc_s, m_s, acc_w, m_w):
    qi = pl.program_id(2)
    q = q_ref[0].reshape(ROWS, LANES)

    s = _dot(q, kct_ref[0, 0]) + cb_ref[0, 0]
    m = jnp.max(s, axis=-1, keepdims=True)
    e = jnp.exp(s - m)
    l = jnp.sum(e, axis=-1, keepdims=True)
    pc = e * jnp.where(m > 0.5 * NEG_INF, 1.0 / l, 0.0)
    o_c = _dot(pc.astype(BF16), vc_ref[0, 0])

    psum = pc[0:TQ] + pc[TQ:2 * TQ] + pc[2 * TQ:3 * TQ] + pc[3 * TQ:4 * TQ]
    hi = psum.astype(BF16)
    lo = (psum - hi.astype(F32)).astype(BF16)
    pslc = _dot(hi, ovl_ref[...]) + _dot(lo, ovl_ref[...])

    lane = lax.broadcasted_iota(jnp.int32, (TQ, LANES), 1)
    j = lane - SEL_LANE0
    t = qi * TQ + lax.broadcasted_iota(jnp.int32, (TQ, LANES), 0)
    dist = t // SEL_BLOCK - j
    score = jnp.where(dist < N_LOCAL_FORCED, FORCED_SCORE, pslc)
    score = jnp.where(j == 0, FORCED_SCORE, score)
    score = jnp.where(dist >= 0, score, NEG_INF)
    n_sel = ks_ref.shape[2] // SEL_BLOCK
    cnt = jnp.zeros((TQ, LANES), F32)
    for i in range(n_sel):
        col = jnp.broadcast_to(score[:, SEL_LANE0 + i:SEL_LANE0 + i + 1], (TQ, LANES))
        beats = jnp.where(col > score, 1.0, jnp.where(col == score, jnp.where(j > i, 1.0, 0.0), 0.0))
        cnt = cnt + beats
    top_k = min(SEL_TOPK, n_sel)
    selbias = jnp.where(cnt < top_k, 0.0, NEG_INF)
    selbias = jnp.where(j >= 0, jnp.where(j < n_sel, selbias, 0.0), 0.0)
    for h in range(HPG):
        qa_ref[h * TQ:(h + 1) * TQ, :] = (q_ref[0, h].astype(F32) + selbias).astype(BF16)
    qa = qa_ref[...]

    def key_tile(ref, kt):
        return ref[0, 0, pl.ds(pl.multiple_of(kt * TQ, TQ), TQ), :]

    m_s[...] = jnp.full((ROWS, LANES), NEG_INF, F32)
    acc_s[...] = jnp.zeros((ROWS, LANES), F32)

    def sel_body(kt, carry):
        _online_update(qa, key_tile(ks_ref, kt), key_tile(vs_ref, kt), None, acc_s, m_s)
        return carry

    lax.fori_loop(0, qi - 1, sel_body, 0)

    @pl.when(qi >= 1)
    def _():
        _online_update(qa, key_tile(ks_ref, qi - 1), key_tile(vs_ref, qi - 1), b1_ref[0], acc_s, m_s)

    _online_update(qa, key_tile(ks_ref, qi), key_tile(vs_ref, qi), b0_ref[0], acc_s, m_s)

    m_w[...] = jnp.full((ROWS, LANES), NEG_INF, F32)
    acc_w[...] = jnp.zeros((ROWS, LANES), F32)

    @pl.when(qi >= 2)
    def _():
        _online_update(q, key_tile(kw_ref, qi - 2), key_tile(vw_ref, qi - 2), w2_ref[...], acc_w, m_w)

    @pl.when(qi >= 1)
    def _():
        _online_update(q, key_tile(kw_ref, qi - 1), key_tile(vw_ref, qi - 1), b1_ref[0], acc_w, m_w)

    _online_update(q, key_tile(kw_ref, qi), key_tile(vw_ref, qi), b0_ref[0], acc_w, m_w)

    a_s = acc_s[...]
    a_w = acc_w[...]
    o_s = a_s / pltpu.roll(a_s, HEAD_DIM, 1)
    o_w = a_w / pltpu.roll(a_w, HEAD_DIM, 1)
    gates = gate_ref[0, 0]
    outs = []
    for h in range(HPG):
        rows = slice(h * TQ, (h + 1) * TQ)
        g_c = gates[:, h * N_BRANCH + 0:h * N_BRANCH + 1]
        g_s = gates[:, h * N_BRANCH + 1:h * N_BRANCH + 2]
        g_w = gates[:, h * N_BRANCH + 2:h * N_BRANCH + 3]
        o_h = g_c * o_c[rows] + g_s * o_s[rows] + g_w * o_w[rows]
        outs.append(o_h[:, :HEAD_DIM])
    o_ref[0] = jnp.concatenate(outs, axis=-1)


def _attn(q, kct, vc, ks, vs, kw, vw, gate, cb, b0, b1, w2, ovl):
    bsz, _, seq, _ = q.shape
    nq = seq // TQ
    kv_spec = pl.BlockSpec((1, 1, seq, LANES), lambda g, b, i: (b, g, 0, 0))
    cmp_spec = pl.BlockSpec((1, 1, N_CMP_PAD, LANES), lambda g, b, i: (b, g, 0, 0))
    return pl.pallas_call(
        _attn_kernel,
        grid=(N_GROUPS, bsz, nq),
        in_specs=[pl.BlockSpec((1, HPG, TQ, LANES), lambda g, b, i: (b, g, i, 0)),
                  cmp_spec, cmp_spec, kv_spec, kv_spec, kv_spec, kv_spec,
                  pl.BlockSpec((1, 1, TQ, LANES), lambda g, b, i: (b, g, i, 0)),
                  pl.BlockSpec((1, 1, ROWS, N_CMP_PAD), lambda g, b, i: (g, i, 0, 0)),
                  pl.BlockSpec((1, ROWS, TQ), lambda g, b, i: (g, 0, 0)),
                  pl.BlockSpec((1, ROWS, TQ), lambda g, b, i: (g, 0, 0)),
                  pl.BlockSpec((ROWS, TQ), lambda g, b, i: (0, 0)),
                  pl.BlockSpec((N_CMP_PAD, LANES), lambda g, b, i: (0, 0))],
        out_specs=pl.BlockSpec((1, TQ, HPG * HEAD_DIM), lambda g, b, i: (b, i, g)),
        out_shape=jax.ShapeDtypeStruct((bsz, seq, D_ATTN), F32),
        scratch_shapes=[pltpu.VMEM((ROWS, LANES), BF16),
                        pltpu.VMEM((ROWS, LANES), F32), pltpu.VMEM((ROWS, LANES), F32),
                        pltpu.VMEM((ROWS, LANES), F32), pltpu.VMEM((ROWS, LANES), F32)],
        compiler_params=pltpu.CompilerParams(
            dimension_semantics=("arbitrary", "arbitrary", "arbitrary"),
            vmem_limit_bytes=VMEM_LIMIT),
        name="attn",
    )(q, kct, vc, ks, vs, kw, vw, gate, cb, b0, b1, w2, ovl)


TS_OUT = 512
HALO = 16


def _outproj_kernel(ya_ref, bg_ref, u_ref, uh_ref, cw_ref, ga_ref, gc_ref, w_ref, x_ref, mod_ref, o_ref):
    ts = u_ref.shape[1]
    u = u_ref[0].astype(F32)
    halo = uh_ref[0].astype(F32) * jnp.where(pl.program_id(1) > 0, 1.0, 0.0)
    h1 = halo[HALO - 1:HALO, :]
    h2 = halo[HALO - 2:HALO - 1, :]
    row = lax.broadcasted_iota(jnp.int32, u.shape, 0)
    u1 = jnp.where(row == 0, h1, pltpu.roll(u, 1, 0))
    u2 = jnp.where(row == 0, h2, jnp.where(row == 1, h1, pltpu.roll(u, 2, 0)))
    conv = u2 * cw_ref[0:1, :] + u1 * cw_ref[1:2, :] + u * cw_ref[2:3, :]
    yc = bg_ref[0].astype(F32) * conv
    yc = yc * lax.rsqrt(jnp.mean(yc * yc, axis=-1, keepdims=True) + EPS) * gc_ref[...]
    ya = ya_ref[0]
    ya = ya * lax.rsqrt(jnp.mean(ya * ya, axis=-1, keepdims=True) + EPS) * ga_ref[...]
    proj = _dot(ya.astype(BF16), w_ref[0:D_ATTN, :]) + _dot(yc.astype(BF16), w_ref[D_ATTN:, :])
    o_ref[0] = x_ref[0] + mod_ref[0, 2:3, :] * proj


def _outproj(ya, bg, u, conv_w, ga, gc, w_out, x, mod3):
    bsz, seq, d = x.shape
    ts = TS_OUT
    c2 = lambda b, i: (0, 0)
    row_spec = lambda width: pl.BlockSpec((1, ts, width), lambda b, i: (b, i, 0))
    return pl.pallas_call(
        _outproj_kernel,
        grid=(bsz, seq // ts),
        in_specs=[row_spec(D_ATTN), row_spec(D_CONV), row_spec(D_CONV),
                  pl.BlockSpec((1, HALO, D_CONV),
                               lambda b, i: (b, jnp.maximum(i * (ts // HALO) - 1, 0), 0)),
                  pl.BlockSpec(conv_w.shape, c2),
                  pl.BlockSpec((1, D_ATTN), c2), pl.BlockSpec((1, D_CONV), c2),
                  pl.BlockSpec(w_out.shape, c2, pipeline_mode=pl.Buffered(1)),
                  row_spec(d),
                  pl.BlockSpec((1, 6, d), lambda b, i: (b, 0, 0))],
        out_specs=row_spec(d),
        out_shape=jax.ShapeDtypeStruct((bsz, seq, d), F32),
        compiler_params=pltpu.CompilerParams(
            dimension_semantics=("arbitrary", "arbitrary"), vmem_limit_bytes=VMEM_LIMIT),
        name="outproj",
    )(ya, bg, u, u, conv_w, ga, gc, w_out, x, mod3)


TS_FFN = 512


def _ffn_kernel(x_ref, mod_ref, g2_ref, w1_ref, w3_ref, w2_ref, o_ref):
    x = x_ref[0]
    ms = jnp.mean(x * x, axis=-1, keepdims=True)
    h = x * lax.rsqrt(ms + EPS) * g2_ref[...] * (1.0 + mod_ref[0, 4:5, :]) + mod_ref[0, 3:4, :]
    hb = h.astype(BF16)
    a = _dot(hb, w1_ref[...])
    b = _dot(hb, w3_ref[...])
    act = (_silu(a) * b).astype(BF16)
    o_ref[0] = x + mod_ref[0, 5:6, :] * _dot(act, w2_ref[...])


def _ffn(x, mod3, g2, w1, w3, w2):
    bsz, seq, d = x.shape
    ts = TS_FFN
    c2 = lambda b, i: (0, 0)
    wspec = lambda w: pl.BlockSpec(w.shape, c2, pipeline_mode=pl.Buffered(1))
    return pl.pallas_call(
        _ffn_kernel,
        grid=(bsz, seq // ts),
        in_specs=[pl.BlockSpec((1, ts, d), lambda b, i: (b, i, 0)),
                  pl.BlockSpec((1, 6, d), lambda b, i: (b, 0, 0)),
                  pl.BlockSpec((1, d), c2),
                  wspec(w1), wspec(w3), wspec(w2)],
        out_specs=pl.BlockSpec((1, ts, d), lambda b, i: (b, i, 0)),
        out_shape=jax.ShapeDtypeStruct((bsz, seq, d), F32),
        compiler_params=pltpu.CompilerParams(
            dimension_semantics=("arbitrary", "arbitrary"), vmem_limit_bytes=VMEM_LIMIT),
        name="ffn",
    )(x, mod3, g2, w1, w3, w2)


def _t5_bucket_np(rel):
    n = np.maximum(rel, 0)
    max_exact = N_BUCKETS // 2
    nf = np.maximum(n, 1).astype(np.float32)
    large = max_exact + (np.log(nf / max_exact) / math.log(MAX_DISTANCE / max_exact)
                         * (N_BUCKETS - max_exact)).astype(np.int32)
    return np.where(n < max_exact, n, np.minimum(large, N_BUCKETS - 1)).astype(np.int32)


def _stack_heads(tab_rel, idx, mask):
    idx = jnp.asarray(idx)
    tab_b = tab_rel.T.reshape((N_HEADS, N_BUCKETS) + (1,) * idx.ndim)
    vals = jnp.zeros((N_HEADS,) + idx.shape, F32)
    for k in range(N_BUCKETS):
        vals = jnp.where(idx == k, tab_b[:, k], vals)
    vals = jnp.where(mask, vals, NEG_INF)
    vals = vals.reshape((N_GROUPS, HPG) + vals.shape[1:])
    vals = jnp.moveaxis(vals, 1, -3)
    return vals.reshape(vals.shape[:-3] + (HPG * vals.shape[-2], vals.shape[-1]))


def _bias_tables(rel_bias_table, seq):
    tab_rel = rel_bias_table - rel_bias_table[N_BUCKETS - 1:N_BUCKETS, :]
    i = np.arange(TQ)[:, None]
    jj = np.arange(TQ)[None, :]
    rel0 = i - jj
    b0 = _stack_heads(tab_rel, _t5_bucket_np(rel0), rel0 >= 0)
    rel1 = TQ + i - jj
    b1 = _stack_heads(tab_rel, _t5_bucket_np(rel1), rel1 < WINDOW + TQ)
    w2 = jnp.asarray(np.tile(np.where(jj > i, 0.0, NEG_INF).astype(np.float32), (HPG, 1)))
    nq = seq // TQ
    n_cmp = (seq - CMP_BLOCK) // CMP_STRIDE + 1
    shift = TQ // CMP_STRIDE
    assert nq * shift <= N_CMP_PAD and n_cmp * CMP_STRIDE + CMP_BLOCK - 1 > seq
    n_rel = np.arange(2 * N_CMP_PAD)[None, :] - N_CMP_PAD
    relc = i - (n_rel * CMP_STRIDE + CMP_BLOCK - 1)
    pat = _stack_heads(tab_rel, _t5_bucket_np(relc), relc >= 0)
    cb = jnp.stack([pat[:, :, N_CMP_PAD - qi * shift:2 * N_CMP_PAD - qi * shift]
                    for qi in range(nq)], axis=1)
    n_sel = seq // SEL_BLOCK
    cs = np.arange(n_cmp) * CMP_STRIDE
    ce = cs + CMP_BLOCK - 1
    ss = np.arange(n_sel) * SEL_BLOCK
    ov = np.clip(np.minimum(ce[:, None], ss[None, :] + SEL_BLOCK - 1)
                 - np.maximum(cs[:, None], ss[None, :]) + 1, 0, None) / CMP_STRIDE
    ovl = np.zeros((N_CMP_PAD, LANES), np.float32)
    ovl[:n_cmp, SEL_LANE0:SEL_LANE0 + n_sel] = ov
    return cb, b0, b1, w2, jnp.asarray(ovl, BF16)


def _blockdiag2(w):
    z = jnp.zeros_like(w)
    return jnp.concatenate([jnp.concatenate([w, z], axis=-1),
                            jnp.concatenate([z, w], axis=-1)], axis=-2)


def kernel(x, c, w_ada, b_ada, norm1_gain, w_in, q_gain, k_cmp_gain, k_sel_gain, k_win_gain,
           cmp_pos_k, cmp_pos_v, w_ck1, w_ck2, w_cv1, w_cv2, rel_bias_table, conv_w,
           attn_out_gain, conv_out_gain, w_out, norm2_gain, w_ff1, w_ff3, w_ff2):
    bsz, seq, d = x.shape
    assert d == D_MODEL and seq % TQ == 0 and seq // SEL_BLOCK <= LANES - SEL_LANE0 - 32
    assert (seq - CMP_BLOCK) // CMP_STRIDE + 1 <= N_CMP_PAD and seq // CMP_STRIDE == N_CMP_PAD

    seg = jnp.asarray(np.kron(np.eye(2), np.ones((HEAD_DIM, HEAD_DIM))) / HEAD_DIM, BF16)
    cb, b0, b1, w2m, ovl = _bias_tables(rel_bias_table, seq)
    two = lambda g: jnp.tile(g.reshape(1, HEAD_DIM), (1, 2))
    n_gate = N_BRANCH * HPG
    gate_pad = jnp.zeros((d, LANES - n_gate), F32)

    for layer in range(w_in.shape[0]):
        wi = w_in[layer]
        o_g = D_ATTN + 6 * D_KV
        w_p = jnp.concatenate(
            [wi[:, :o_g], wi[:, o_g:o_g + n_gate], gate_pad,
             wi[:, o_g + n_gate:o_g + 2 * n_gate], gate_pad, wi[:, o_g + 2 * n_gate:]],
            axis=-1).astype(BF16)
        w1k = _blockdiag2(w_ck1[layer].reshape(CMP_BLOCK, HEAD_DIM, CMP_HIDDEN)).astype(BF16)
        w1v = _blockdiag2(w_cv1[layer].reshape(CMP_BLOCK, HEAD_DIM, CMP_HIDDEN)).astype(BF16)
        w2k = _blockdiag2(w_ck2[layer]).astype(BF16)
        w2v = _blockdiag2(w_cv2[layer]).astype(BF16)
        pk2 = jnp.tile(cmp_pos_k[layer], (1, 2))
        pv2 = jnp.tile(cmp_pos_v[layer], (1, 2))

        mod3 = _adaln(c, w_ada[layer], b_ada[layer]).reshape(bsz, 6, d)
        q, kvc, ks, vs, kw, vw, gate, bg, u = _inproj(
            x, mod3, norm1_gain[layer].reshape(1, d), w_p, seg,
            two(q_gain[layer]), two(k_sel_gain[layer]), two(k_win_gain[layer]))
        kct, vc = _compress(kvc, w1k, w1v, w2k, w2v, pk2, pv2, seg, two(k_cmp_gain[layer]))
        y_attn = _attn(q, kct, vc, ks, vs, kw, vw, gate, cb, b0, b1, w2m, ovl)
        x = _outproj(y_attn, bg, u, conv_w[layer], attn_out_gain[layer].reshape(1, D_ATTN),
                     conv_out_gain[layer].reshape(1, D_CONV), w_out[layer].astype(BF16), x, mod3)
        x = _ffn(x, mod3, norm2_gain[layer].reshape(1, d), w_ff1[layer].astype(BF16),
                 w_ff3[layer].astype(BF16), w_ff2[layer].astype(BF16))
    return x
```

```python
import math

import numpy as np
import jax
import jax.numpy as jnp
from jax import lax
from jax.experimental import pallas as pl
from jax.experimental.pallas import tpu as pltpu

F32 = jnp.float32
BF16 = jnp.bfloat16

D_MODEL = 1024
HEAD_DIM = 64
N_HEADS = 8
N_GROUPS = 2
HPG = N_HEADS // N_GROUPS
D_ATTN = N_HEADS * HEAD_DIM
D_KV = N_GROUPS * HEAD_DIM
N_BRANCH = 3
D_CONV = D_MODEL - D_ATTN
CONV_WIDTH = 3
CMP_BLOCK = 32
CMP_STRIDE = 16
CMP_HIDDEN = 256
SEL_BLOCK = 64
SEL_TOPK = 16
N_LOCAL_FORCED = 2
WINDOW = 512
N_BUCKETS = 32
MAX_DISTANCE = 128
EPS = 1e-6
NEG_INF = -1e30
FORCED_SCORE = 1e6

LANES = 128
TQ = 256
ROWS = HPG * TQ
N_CMP_PAD = 128
SEL_LANE0 = 64
SHIFT_LANE = 96
V_LANES = 256
LOG2E = math.log2(math.e)
MAX_SHIFTED_RANGE = 100.0
VMEM_LIMIT = 56 * 1024 * 1024

_NT = (((1,), (1,)), ((), ()))


def _dot(a, b):
    return jnp.dot(a, b, preferred_element_type=F32)


def _silu(v):
    return v * (1.0 / (1.0 + jnp.exp(-v)))


def _sigmoid(v):
    return 1.0 / (1.0 + jnp.exp(-v))


def _seg_mean_sq(v, seg):
    sq = v * v
    hi = sq.astype(BF16)
    lo = (sq - hi.astype(F32)).astype(BF16)
    return _dot(hi, seg) + _dot(lo, seg)


def _adaln_kernel(c_ref, w_ref, b_ref, o_ref):
    sc = _silu(c_ref[...]).astype(BF16)
    o_ref[...] = _dot(sc, w_ref[...].astype(BF16)) + b_ref[...]


def _adaln(c, w_ada, b_ada):
    bsz, d = c.shape
    n = w_ada.shape[1]
    tn = 1536
    return pl.pallas_call(
        _adaln_kernel,
        grid=(n // tn,),
        in_specs=[pl.BlockSpec((bsz, d), lambda j: (0, 0)),
                  pl.BlockSpec((d, tn), lambda j: (0, j)),
                  pl.BlockSpec((1, tn), lambda j: (0, j))],
        out_specs=pl.BlockSpec((bsz, tn), lambda j: (0, j)),
        out_shape=jax.ShapeDtypeStruct((bsz, n), F32),
        compiler_params=pltpu.CompilerParams(
            dimension_semantics=("arbitrary",), vmem_limit_bytes=VMEM_LIMIT),
        name="adaln",
    )(c, w_ada, b_ada.reshape(1, n))


C_Q, C_KC, C_KS, C_VS, C_KW, C_VW, C_G0, C_BG, C_CG, C_XT, C_END = (
    0, 512, 768, 896, 1024, 1152, 1280, 1536, 2048, 2560, 3072)
TS_IN = 512


def _inproj_kernel(x_ref, mod_ref, g1_ref, w_ref, seg_ref, gq_ref, gks_ref, gkw_ref, tail_ref,
                   q_ref, kvc_ref, ks_ref, vs_ref, kw_ref, vw_ref, gate_ref, bg_ref, u_ref):
    ts = x_ref.shape[1]
    x = x_ref[0]
    ms = jnp.mean(x * x, axis=-1, keepdims=True)
    y = x * lax.rsqrt(ms + EPS) * g1_ref[...]
    h = y * (1.0 + mod_ref[0, 1:2, :]) + mod_ref[0, 0:1, :]
    hb = h.astype(BF16)
    seg = seg_ref[...]

    def proj(c0, c1):
        return _dot(hb, w_ref[:, c0:c1])

    lane = lax.broadcasted_iota(jnp.int32, (ts, HEAD_DIM), 1)
    ones192 = jnp.ones((ts, V_LANES - HEAD_DIM), F32)
    q_tail = jnp.where(lane == SHIFT_LANE - HEAD_DIM, 1.0, 0.0).astype(F32)

    for cpair in range(N_HEADS // 2):
        v = proj(C_Q + cpair * LANES, C_Q + (cpair + 1) * LANES)
        vn = v * lax.rsqrt(_seg_mean_sq(v, seg) + EPS) * gq_ref[...] * (HEAD_DIM ** -0.5 * LOG2E)
        for half in range(2):
            head = cpair * 2 + half
            qh = jnp.concatenate([vn[:, half * HEAD_DIM:(half + 1) * HEAD_DIM], q_tail], axis=-1)
            q_ref[0, head] = qh.astype(BF16)

    kvc_ref[0, 0] = proj(C_KC, C_KC + D_KV)
    kvc_ref[0, 1] = proj(C_KC + D_KV, C_KS)

    row = pl.program_id(1) * ts + lax.broadcasted_iota(jnp.int32, (ts, HEAD_DIM), 0)
    onehot = jnp.where(lane == row // SEL_BLOCK, 1.0, 0.0).astype(F32) + tail_ref[0:1, :]
    v = proj(C_KS, C_VS)
    vn = v * lax.rsqrt(_seg_mean_sq(v, seg) + EPS) * gks_ref[...]
    for g in range(N_GROUPS):
        ks_ref[0, g] = jnp.concatenate(
            [vn[:, g * HEAD_DIM:(g + 1) * HEAD_DIM], onehot], axis=-1).astype(BF16)
    v = proj(C_VS, C_KW)
    for g in range(N_GROUPS):
        vs_ref[0, g] = jnp.concatenate(
            [v[:, g * HEAD_DIM:(g + 1) * HEAD_DIM], ones192], axis=-1).astype(BF16)
    v = proj(C_KW, C_VW)
    vn = v * lax.rsqrt(_seg_mean_sq(v, seg) + EPS) * gkw_ref[...]
    for g in range(N_GROUPS):
        kw_ref[0, g] = jnp.concatenate(
            [vn[:, g * HEAD_DIM:(g + 1) * HEAD_DIM],
             jnp.broadcast_to(tail_ref[1:2, :], (ts, HEAD_DIM))], axis=-1).astype(BF16)
    v = proj(C_VW, C_G0)
    for g in range(N_GROUPS):
        vw_ref[0, g] = jnp.concatenate(
            [v[:, g * HEAD_DIM:(g + 1) * HEAD_DIM], ones192], axis=-1).astype(BF16)

    for g in range(N_GROUPS):
        gate_ref[0, g] = _sigmoid(proj(C_G0 + g * LANES, C_G0 + (g + 1) * LANES))

    bg_ref[0] = proj(C_BG, C_CG).astype(BF16)
    u_ref[0] = (proj(C_CG, C_XT) * proj(C_XT, C_END)).astype(BF16)


def _inproj(x, mod3, g1, w_p, seg, gq2, gks2, gkw2, tails):
    bsz, seq, d = x.shape
    ts = TS_IN
    const2 = lambda b, i: (0, 0)
    k_shape = jax.ShapeDtypeStruct((bsz, N_GROUPS, seq, LANES), BF16)
    k_spec = pl.BlockSpec((1, N_GROUPS, ts, LANES), lambda b, i: (b, 0, i, 0))
    v_shape = jax.ShapeDtypeStruct((bsz, N_GROUPS, seq, V_LANES), BF16)
    v_spec = pl.BlockSpec((1, N_GROUPS, ts, V_LANES), lambda b, i: (b, 0, i, 0))
    return pl.pallas_call(
        _inproj_kernel,
        grid=(bsz, seq // ts),
        in_specs=[pl.BlockSpec((1, ts, d), lambda b, i: (b, i, 0)),
                  pl.BlockSpec((1, 6, d), lambda b, i: (b, 0, 0)),
                  pl.BlockSpec((1, d), const2),
                  pl.BlockSpec(w_p.shape, const2, pipeline_mode=pl.Buffered(1)),
                  pl.BlockSpec((LANES, LANES), const2),
                  pl.BlockSpec((1, LANES), const2),
                  pl.BlockSpec((1, LANES), const2),
                  pl.BlockSpec((1, LANES), const2),
                  pl.BlockSpec(tails.shape, const2)],
        out_specs=[pl.BlockSpec((1, N_HEADS, ts, LANES), lambda b, i: (b, 0, i, 0)),
                   pl.BlockSpec((1, 2, ts, D_KV), lambda b, i: (b, 0, i, 0)),
                   k_spec, v_spec, k_spec, v_spec,
                   pl.BlockSpec((1, N_GROUPS, ts, LANES), lambda b, i: (b, 0, i, 0)),
                   pl.BlockSpec((1, ts, D_CONV), lambda b, i: (b, i, 0)),
                   pl.BlockSpec((1, ts, D_CONV), lambda b, i: (b, i, 0))],
        out_shape=[jax.ShapeDtypeStruct((bsz, N_HEADS, seq, LANES), BF16),
                   jax.ShapeDtypeStruct((bsz, 2, seq, D_KV), F32),
                   k_shape, v_shape, k_shape, v_shape,
                   jax.ShapeDtypeStruct((bsz, N_GROUPS, seq, LANES), F32),
                   jax.ShapeDtypeStruct((bsz, seq, D_CONV), BF16),
                   jax.ShapeDtypeStruct((bsz, seq, D_CONV), BF16)],
        compiler_params=pltpu.CompilerParams(
            dimension_semantics=("arbitrary", "arbitrary"), vmem_limit_bytes=VMEM_LIMIT),
        name="inproj",
    )(x, mod3, g1, w_p, seg, gq2, gks2, gkw2, tails)


def _compress_kernel(kvc_ref, w1k_ref, w1v_ref, w2k_ref, w2v_ref, pk_ref, pv_ref, seg_ref,
                     gk_ref, kct_ref, vc_ref):
    n_rows = kvc_ref.shape[2] // CMP_STRIDE

    def branch(which, w1_ref, w2_ref, pos_ref):
        p1 = jnp.zeros((n_rows, N_GROUPS * CMP_HIDDEN), F32)
        p2 = jnp.zeros((n_rows, N_GROUPS * CMP_HIDDEN), F32)
        for r in range(CMP_STRIDE):
            rows = kvc_ref[0, which, pl.ds(r, n_rows, stride=CMP_STRIDE), :]
            a1 = (rows + pos_ref[r:r + 1, :]).astype(BF16)
            a2 = (rows + pos_ref[CMP_STRIDE + r:CMP_STRIDE + r + 1, :]).astype(BF16)
            p1 = p1 + _dot(a1, w1_ref[r])
            p2 = p2 + _dot(a2, w1_ref[CMP_STRIDE + r])
        hid = p1 + pltpu.roll(p2, n_rows - 1, 0)
        return _dot(_silu(hid).astype(BF16), w2_ref[...])

    kc = branch(0, w1k_ref, w2k_ref, pk_ref)
    kc = kc * lax.rsqrt(_seg_mean_sq(kc, seg_ref[...]) + EPS) * gk_ref[...]
    kct = kc.T
    zpad = jnp.zeros((HEAD_DIM, n_rows), F32)
    vcv = branch(1, w1v_ref, w2v_ref, pv_ref)
    zlane = jnp.zeros((n_rows, HEAD_DIM), F32)
    for g in range(N_GROUPS):
        kct_ref[0, g] = jnp.concatenate(
            [kct[g * HEAD_DIM:(g + 1) * HEAD_DIM, :], zpad], axis=0).astype(BF16)
        vc_ref[0, g] = jnp.concatenate(
            [vcv[:, g * HEAD_DIM:(g + 1) * HEAD_DIM], zlane], axis=-1).astype(BF16)


def _compress(kvc, w1k, w1v, w2k, w2v, pk2, pv2, seg, gk2):
    bsz, _, seq, _ = kvc.shape
    c3 = lambda b: (0, 0, 0)
    c2 = lambda b: (0, 0)
    out_shape = jax.ShapeDtypeStruct((bsz, N_GROUPS, N_CMP_PAD, LANES), BF16)
    out_spec = pl.BlockSpec((1, N_GROUPS, N_CMP_PAD, LANES), lambda b: (b, 0, 0, 0))
    return pl.pallas_call(
        _compress_kernel,
        grid=(bsz,),
        in_specs=[pl.BlockSpec((1, 2, seq, D_KV), lambda b: (b, 0, 0, 0)),
                  pl.BlockSpec(w1k.shape, c3), pl.BlockSpec(w1v.shape, c3),
                  pl.BlockSpec(w2k.shape, c2), pl.BlockSpec(w2v.shape, c2),
                  pl.BlockSpec(pk2.shape, c2), pl.BlockSpec(pv2.shape, c2),
                  pl.BlockSpec(seg.shape, c2), pl.BlockSpec(gk2.shape, c2)],
        out_specs=[out_spec, out_spec],
        out_shape=[out_shape, out_shape],
        compiler_params=pltpu.CompilerParams(
            dimension_semantics=("arbitrary",), vmem_limit_bytes=VMEM_LIMIT),
        name="compress",
    )(kvc, w1k, w1v, w2k, w2v, pk2, pv2, seg, gk2)


def _logits(qa, k_rows, bias):
    s = lax.dot_general(qa, k_rows, _NT, preferred_element_type=F32)
    return s if bias is None else s + bias


def _shifted_pv(qa, k_rows, v_rows, bias, keep=None):
    if keep is not None:
        v_rows = v_rows * keep.astype(BF16)
    return _dot(jnp.exp2(_logits(qa, k_rows, bias)).astype(BF16), v_rows)


def _online_update(qa, k_tile, v_tile, bias, acc_ref, m_ref):
    s = _logits(qa, k_tile, bias)
    m_prev = m_ref[...]
    m_new = jnp.maximum(m_prev, jnp.max(s, axis=-1, keepdims=True))
    alpha = jnp.exp2(m_prev - m_new)
    p = jnp.exp2(s - jnp.concatenate([m_new] * (TQ // LANES), axis=-1))
    acc_ref[...] = (acc_ref[...] * jnp.concatenate([alpha] * (V_LANES // LANES), axis=-1)
                    + _dot(p.astype(BF16), v_tile))
    m_ref[...] = m_new


def _selection_bias(pc, qi, ovlt, n_sel):
    psum = pc[0:TQ] + pc[TQ:2 * TQ] + pc[2 * TQ:3 * TQ] + pc[3 * TQ:4 * TQ]
    hi = psum.astype(BF16)
    lo = (psum - hi.astype(F32)).astype(BF16)
    pslc_t = (lax.dot_general(ovlt, hi, _NT, preferred_element_type=F32)
              + lax.dot_general(ovlt, lo, _NT, preferred_element_type=F32))
    top_k = min(SEL_TOPK, n_sel)
    sub = 8
    j = lax.broadcasted_iota(jnp.int32, (n_sel, TQ), 0)
    t = qi * TQ + lax.broadcasted_iota(jnp.int32, (n_sel, TQ), 1)
    dist = jnp.right_shift(t, int(math.log2(SEL_BLOCK))) - j
    score = jnp.where(dist < N_LOCAL_FORCED, FORCED_SCORE, pslc_t[SEL_LANE0:SEL_LANE0 + n_sel, :])
    score = jnp.where(j == 0, FORCED_SCORE, score)
    score = jnp.where(dist >= 0, score, NEG_INF)
    groups = [score[a * sub:(a + 1) * sub] for a in range(n_sel // sub)]
    cnts = [jnp.zeros((sub, TQ), F32) for _ in groups]
    j_sub = lax.broadcasted_iota(jnp.int32, (sub, TQ), 0)
    for i in range(n_sel):
        row = jnp.broadcast_to(score[i:i + 1, :], (sub, TQ))
        for a, grp in enumerate(groups):
            if a * sub > i:
                beats = jnp.where(row >= grp, 1.0, 0.0)
            elif a * sub + sub - 1 <= i:
                beats = jnp.where(row > grp, 1.0, 0.0)
            else:
                beats = jnp.where(j_sub + a * sub > i, jnp.where(row >= grp, 1.0, 0.0),
                                  jnp.where(row > grp, 1.0, 0.0))
            cnts[a] = cnts[a] + beats
    sel_t = jnp.where(jnp.concatenate(cnts, axis=0) < top_k, 0.0, NEG_INF)
    return jnp.concatenate(
        [jnp.zeros((SEL_LANE0, TQ), F32), sel_t,
         jnp.zeros((LANES - SEL_LANE0 - n_sel, TQ), F32)], axis=0).T


def _attn_step(shifted, q_ref, kct_ref, vc_ref, ks_ref, vs_ref, kw_ref, vw_ref, gate_ref,
               cb_ref, b0_ref, b1_ref, w2_ref, ovlt_ref, o_ref, qa_ref, acc_s, m_s, acc_w, m_w):
    qi = pl.program_id(2)
    q = q_ref[0].reshape(ROWS, LANES)

    def key_rows(ref, kt, n_tiles=1):
        return ref[0, 0, pl.ds(pl.multiple_of(kt * TQ, TQ), n_tiles * TQ), :]

    s = _dot(q, kct_ref[0, 0]) + cb_ref[0, 0]
    m = jnp.max(s, axis=-1, keepdims=True)
    e = jnp.exp2(s - m)
    l = jnp.sum(e, axis=-1, keepdims=True)
    pc = e * jnp.where(m > 0.5 * NEG_INF, 1.0 / l, 0.0)
    o_c = _dot(pc.astype(BF16), vc_ref[0, 0])

    selbias = _selection_bias(pc, qi, ovlt_ref[...], ks_ref.shape[2] // SEL_BLOCK)
    for h in range(HPG):
        qa_ref[h * TQ:(h + 1) * TQ, :] = (q_ref[0, h].astype(F32) + selbias).astype(BF16)
    qa = qa_ref[...]

    kt1 = jnp.maximum(qi - 1, 0)
    kt2 = jnp.maximum(qi - 2, 0)
    if shifted:
        has1 = jnp.where(qi >= 1, 1.0, 0.0)
        has2 = jnp.where(qi >= 2, 1.0, 0.0)
        a_w = (_shifted_pv(q, key_rows(kw_ref, kt2), key_rows(vw_ref, kt2), w2_ref[...], has2)
               + _shifted_pv(q, key_rows(kw_ref, kt1), key_rows(vw_ref, kt1), b1_ref[0], has1)
               + _shifted_pv(q, key_rows(kw_ref, qi), key_rows(vw_ref, qi), b0_ref[0]))
        acc_s[...] = jnp.zeros((ROWS, V_LANES), F32)
        n_plain = jnp.maximum(qi - 1, 0)

        def pair_body(it, carry):
            acc_s[...] += _shifted_pv(qa, key_rows(ks_ref, 2 * it, 2), key_rows(vs_ref, 2 * it, 2), None)
            return carry

        lax.fori_loop(0, n_plain // 2, pair_body, 0)
        odd = jnp.where(n_plain % 2 == 1, 1.0, 0.0)
        a_s = (acc_s[...]
               + _shifted_pv(qa, key_rows(ks_ref, kt2), key_rows(vs_ref, kt2), None, odd)
               + _shifted_pv(qa, key_rows(ks_ref, kt1), key_rows(vs_ref, kt1), b1_ref[0], has1)
               + _shifted_pv(qa, key_rows(ks_ref, qi), key_rows(vs_ref, qi), b0_ref[0]))
    else:
        for acc, m_ref in ((acc_s, m_s), (acc_w, m_w)):
            acc[...] = jnp.zeros((ROWS, V_LANES), F32)
            m_ref[...] = jnp.full((ROWS, LANES), NEG_INF, F32)

        def sel_body(kt, carry):
            _online_update(qa, key_rows(ks_ref, kt), key_rows(vs_ref, kt), None, acc_s, m_s)
            return carry

        lax.fori_loop(0, qi - 1, sel_body, 0)

        @pl.when(qi >= 2)
        def _():
            _online_update(q, key_rows(kw_ref, kt2), key_rows(vw_ref, kt2), w2_ref[...], acc_w, m_w)

        @pl.when(qi >= 1)
        def _():
            _online_update(qa, key_rows(ks_ref, kt1), key_rows(vs_ref, kt1), b1_ref[0], acc_s, m_s)
            _online_update(q, key_rows(kw_ref, kt1), key_rows(vw_ref, kt1), b1_ref[0], acc_w, m_w)

        _online_update(qa, key_rows(ks_ref, qi), key_rows(vs_ref, qi), b0_ref[0], acc_s, m_s)
        _online_update(q, key_rows(kw_ref, qi), key_rows(vw_ref, qi), b0_ref[0], acc_w, m_w)
        a_s = acc_s[...]
        a_w = acc_w[...]

    o_s = a_s[:, :LANES] / a_s[:, LANES:]
    o_w = a_w[:, :LANES] / a_w[:, LANES:]
    gates = gate_ref[0, 0]
    outs = []
    for h in range(HPG):
        rows = slice(h * TQ, (h + 1) * TQ)
        g_c = gates[:, h * N_BRANCH + 0:h * N_BRANCH + 1]
        g_s = gates[:, h * N_BRANCH + 1:h * N_BRANCH + 2]
        g_w = gates[:, h * N_BRANCH + 2:h * N_BRANCH + 3]
        o_h = g_c * o_c[rows] + g_s * o_s[rows] + g_w * o_w[rows]
        outs.append(o_h[:, :HEAD_DIM])
    o_ref[0] = jnp.concatenate(outs, axis=-1)


def _attn_kernel(flag_ref, *refs):
    use_shift = flag_ref[0] == 1

    @pl.when(use_shift)
    def _():
        _attn_step(True, *refs)

    @pl.when(jnp.logical_not(use_shift))
    def _():
        _attn_step(False, *refs)


def _attn(flag, q, kct, vc, ks, vs, kw, vw, gate, cb, b0, b1, w2, ovlt):
    bsz, _, seq, _ = q.shape
    nq = seq // TQ
    k_spec = pl.BlockSpec((1, 1, seq, LANES), lambda g, b, i: (b, g, 0, 0))
    v_spec = pl.BlockSpec((1, 1, seq, V_LANES), lambda g, b, i: (b, g, 0, 0))
    cmp_spec = pl.BlockSpec((1, 1, N_CMP_PAD, LANES), lambda g, b, i: (b, g, 0, 0))
    return pl.pallas_call(
        _attn_kernel,
        grid=(N_GROUPS, bsz, nq),
        in_specs=[pl.BlockSpec(memory_space=pltpu.SMEM),
                  pl.BlockSpec((1, HPG, TQ, LANES), lambda g, b, i: (b, g, i, 0)),
                  cmp_spec, cmp_spec, k_spec, v_spec, k_spec, v_spec,
                  pl.BlockSpec((1, 1, TQ, LANES), lambda g, b, i: (b, g, i, 0)),
                  pl.BlockSpec((1, 1, ROWS, N_CMP_PAD), lambda g, b, i: (g, i, 0, 0)),
                  pl.BlockSpec((1, ROWS, TQ), lambda g, b, i: (g, 0, 0)),
                  pl.BlockSpec((1, ROWS, TQ), lambda g, b, i: (g, 0, 0)),
                  pl.BlockSpec((ROWS, TQ), lambda g, b, i: (0, 0)),
                  pl.BlockSpec((LANES, N_CMP_PAD), lambda g, b, i: (0, 0))],
        out_specs=pl.BlockSpec((1, TQ, HPG * HEAD_DIM), lambda g, b, i: (b, i, g)),
        out_shape=jax.ShapeDtypeStruct((bsz, seq, D_ATTN), F32),
        scratch_shapes=[pltpu.VMEM((ROWS, LANES), BF16),
                        pltpu.VMEM((ROWS, V_LANES), F32), pltpu.VMEM((ROWS, LANES), F32),
                        pltpu.VMEM((ROWS, V_LANES), F32), pltpu.VMEM((ROWS, LANES), F32)],
        compiler_params=pltpu.CompilerParams(
            dimension_semantics=("arbitrary", "arbitrary", "arbitrary"),
            vmem_limit_bytes=VMEM_LIMIT),
        name="attn",
    )(flag, q, kct, vc, ks, vs, kw, vw, gate, cb, b0, b1, w2, ovlt)


TS_OUT = 512
HALO = 16


def _outproj_kernel(ya_ref, bg_ref, u_ref, uh_ref, cw_ref, ga_ref, gc_ref, w_ref, x_ref, mod_ref, o_ref):
    u = u_ref[0].astype(F32)
    halo = uh_ref[0].astype(F32) * jnp.where(pl.program_id(1) > 0, 1.0, 0.0)
    h1 = halo[HALO - 1:HALO, :]
    h2 = halo[HALO - 2:HALO - 1, :]
    row = lax.broadcasted_iota(jnp.int32, u.shape, 0)
    u1 = jnp.where(row == 0, h1, pltpu.roll(u, 1, 0))
    u2 = jnp.where(row == 0, h2, jnp.where(row == 1, h1, pltpu.roll(u, 2, 0)))
    conv = u2 * cw_ref[0:1, :] + u1 * cw_ref[1:2, :] + u * cw_ref[2:3, :]
    yc = bg_ref[0].astype(F32) * conv
    yc = yc * lax.rsqrt(jnp.mean(yc * yc, axis=-1, keepdims=True) + EPS) * gc_ref[...]
    ya = ya_ref[0]
    ya = ya * lax.rsqrt(jnp.mean(ya * ya, axis=-1, keepdims=True) + EPS) * ga_ref[...]
    proj = _dot(ya.astype(BF16), w_ref[0:D_ATTN, :]) + _dot(yc.astype(BF16), w_ref[D_ATTN:, :])
    o_ref[0] = x_ref[0] + mod_ref[0, 2:3, :] * proj


def _outproj(ya, bg, u, conv_w, ga, gc, w_out, x, mod3):
    bsz, seq, d = x.shape
    ts = TS_OUT
    c2 = lambda b, i: (0, 0)
    row_spec = lambda width: pl.BlockSpec((1, ts, width), lambda b, i: (b, i, 0))
    return pl.pallas_call(
        _outproj_kernel,
        grid=(bsz, seq // ts),
        in_specs=[row_spec(D_ATTN), row_spec(D_CONV), row_spec(D_CONV),
                  pl.BlockSpec((1, HALO, D_CONV),
                               lambda b, i: (b, jnp.maximum(i * (ts // HALO) - 1, 0), 0)),
                  pl.BlockSpec(conv_w.shape, c2),
                  pl.BlockSpec((1, D_ATTN), c2), pl.BlockSpec((1, D_CONV), c2),
                  pl.BlockSpec(w_out.shape, c2, pipeline_mode=pl.Buffered(1)),
                  row_spec(d),
                  pl.BlockSpec((1, 6, d), lambda b, i: (b, 0, 0))],
        out_specs=row_spec(d),
        out_shape=jax.ShapeDtypeStruct((bsz, seq, d), F32),
        compiler_params=pltpu.CompilerParams(
            dimension_semantics=("arbitrary", "arbitrary"), vmem_limit_bytes=VMEM_LIMIT),
        name="outproj",
    )(ya, bg, u, u, conv_w, ga, gc, w_out, x, mod3)


TS_FFN = 512


def _ffn_kernel(x_ref, mod_ref, g2_ref, w1_ref, w3_ref, w2_ref, o_ref):
    x = x_ref[0]
    ms = jnp.mean(x * x, axis=-1, keepdims=True)
    h = x * lax.rsqrt(ms + EPS) * g2_ref[...] * (1.0 + mod_ref[0, 4:5, :]) + mod_ref[0, 3:4, :]
    hb = h.astype(BF16)
    a = _dot(hb, w1_ref[...])
    b = _dot(hb, w3_ref[...])
    act = (_silu(a) * b).astype(BF16)
    o_ref[0] = x + mod_ref[0, 5:6, :] * _dot(act, w2_ref[...])


def _ffn(x, mod3, g2, w1, w3, w2):
    bsz, seq, d = x.shape
    ts = TS_FFN
    c2 = lambda b, i: (0, 0)
    wspec = lambda w: pl.BlockSpec(w.shape, c2, pipeline_mode=pl.Buffered(1))
    return pl.pallas_call(
        _ffn_kernel,
        grid=(bsz, seq // ts),
        in_specs=[pl.BlockSpec((1, ts, d), lambda b, i: (b, i, 0)),
                  pl.BlockSpec((1, 6, d), lambda b, i: (b, 0, 0)),
                  pl.BlockSpec((1, d), c2),
                  wspec(w1), wspec(w3), wspec(w2)],
        out_specs=pl.BlockSpec((1, ts, d), lambda b, i: (b, i, 0)),
        out_shape=jax.ShapeDtypeStruct((bsz, seq, d), F32),
        compiler_params=pltpu.CompilerParams(
            dimension_semantics=("arbitrary", "arbitrary"), vmem_limit_bytes=VMEM_LIMIT),
        name="ffn",
    )(x, mod3, g2, w1, w3, w2)


def _t5_bucket_np(rel):
    n = np.maximum(rel, 0)
    max_exact = N_BUCKETS // 2
    nf = np.maximum(n, 1).astype(np.float32)
    large = max_exact + (np.log(nf / max_exact) / math.log(MAX_DISTANCE / max_exact)
                         * (N_BUCKETS - max_exact)).astype(np.int32)
    return np.where(n < max_exact, n, np.minimum(large, N_BUCKETS - 1)).astype(np.int32)


def _stack_heads(tab_rel, idx, mask):
    idx = jnp.asarray(idx)
    tab_b = tab_rel.T.reshape((N_HEADS, N_BUCKETS) + (1,) * idx.ndim)
    vals = jnp.zeros((N_HEADS,) + idx.shape, F32)
    for k in range(N_BUCKETS):
        vals = jnp.where(idx == k, tab_b[:, k], vals)
    vals = jnp.where(mask, vals, NEG_INF)
    vals = vals.reshape((N_GROUPS, HPG) + vals.shape[1:])
    vals = jnp.moveaxis(vals, 1, -3)
    return vals.reshape(vals.shape[:-3] + (HPG * vals.shape[-2], vals.shape[-1]))


def _bias_tables(rel_bias_table, seq):
    tab_rel = (rel_bias_table - rel_bias_table[N_BUCKETS - 1:N_BUCKETS, :]) * LOG2E
    i = np.arange(TQ)[:, None]
    jj = np.arange(TQ)[None, :]
    rel0 = i - jj
    b0 = _stack_heads(tab_rel, _t5_bucket_np(rel0), rel0 >= 0)
    rel1 = TQ + i - jj
    b1 = _stack_heads(tab_rel, _t5_bucket_np(rel1), rel1 < WINDOW + TQ)
    w2 = jnp.asarray(np.tile(np.where(jj > i, 0.0, NEG_INF).astype(np.float32), (HPG, 1)))
    nq = seq // TQ
    n_cmp = (seq - CMP_BLOCK) // CMP_STRIDE + 1
    shift = TQ // CMP_STRIDE
    assert nq * shift <= N_CMP_PAD and n_cmp * CMP_STRIDE + CMP_BLOCK - 1 > seq
    n_rel = np.arange(2 * N_CMP_PAD)[None, :] - N_CMP_PAD
    relc = i - (n_rel * CMP_STRIDE + CMP_BLOCK - 1)
    pat = _stack_heads(tab_rel, _t5_bucket_np(relc), relc >= 0)
    cb = jnp.stack([pat[:, :, N_CMP_PAD - qi * shift:2 * N_CMP_PAD - qi * shift]
                    for qi in range(nq)], axis=1)
    n_sel = seq // SEL_BLOCK
    cs = np.arange(n_cmp) * CMP_STRIDE
    ce = cs + CMP_BLOCK - 1
    ss = np.arange(n_sel) * SEL_BLOCK
    ov = np.clip(np.minimum(ce[:, None], ss[None, :] + SEL_BLOCK - 1)
                 - np.maximum(cs[:, None], ss[None, :]) + 1, 0, None) / CMP_STRIDE
    ovlt = np.zeros((LANES, N_CMP_PAD), np.float32)
    ovlt[SEL_LANE0:SEL_LANE0 + n_sel, :n_cmp] = ov.T
    return tab_rel, cb, b0, b1, w2, jnp.asarray(ovlt, BF16)


def _blockdiag2(w):
    z = jnp.zeros_like(w)
    return jnp.concatenate([jnp.concatenate([w, z], axis=-1),
                            jnp.concatenate([z, w], axis=-1)], axis=-2)


def kernel(x, c, w_ada, b_ada, norm1_gain, w_in, q_gain, k_cmp_gain, k_sel_gain, k_win_gain,
           cmp_pos_k, cmp_pos_v, w_ck1, w_ck2, w_cv1, w_cv2, rel_bias_table, conv_w,
           attn_out_gain, conv_out_gain, w_out, norm2_gain, w_ff1, w_ff3, w_ff2):
    bsz, seq, d = x.shape
    assert d == D_MODEL and seq % TQ == 0 and SEL_LANE0 + seq // SEL_BLOCK <= SHIFT_LANE
    assert (seq - CMP_BLOCK) // CMP_STRIDE + 1 <= N_CMP_PAD and seq // CMP_STRIDE == N_CMP_PAD

    seg = jnp.asarray(np.kron(np.eye(2), np.ones((HEAD_DIM, HEAD_DIM))) / HEAD_DIM, BF16)
    tab_rel, cb, b0, b1, w2m, ovlt = _bias_tables(rel_bias_table, seq)
    bias_max = jnp.max(jnp.abs(tab_rel))
    two = lambda g: jnp.tile(g.reshape(1, HEAD_DIM), (1, 2))
    n_gate = N_BRANCH * HPG
    gate_pad = jnp.zeros((d, LANES - n_gate), F32)

    for layer in range(w_in.shape[0]):
        wi = w_in[layer]
        o_g = D_ATTN + 6 * D_KV
        w_p = jnp.concatenate(
            [wi[:, :o_g], wi[:, o_g:o_g + n_gate], gate_pad,
             wi[:, o_g + n_gate:o_g + 2 * n_gate], gate_pad, wi[:, o_g + 2 * n_gate:]],
            axis=-1).astype(BF16)
        w1k = _blockdiag2(w_ck1[layer].reshape(CMP_BLOCK, HEAD_DIM, CMP_HIDDEN)).astype(BF16)
        w1v = _blockdiag2(w_cv1[layer].reshape(CMP_BLOCK, HEAD_DIM, CMP_HIDDEN)).astype(BF16)
        w2k = _blockdiag2(w_ck2[layer]).astype(BF16)
        w2v = _blockdiag2(w_cv2[layer]).astype(BF16)
        pk2 = jnp.tile(cmp_pos_k[layer], (1, 2))
        pv2 = jnp.tile(cmp_pos_v[layer], (1, 2))

        gq_max = jnp.max(jnp.abs(q_gain[layer]))
        bound = lambda gk: (HEAD_DIM ** 0.5 * LOG2E * 1.01) * gq_max * jnp.max(jnp.abs(gk)) + bias_max
        c_sel, c_win = bound(k_sel_gain[layer]), bound(k_win_gain[layer])
        flag = (2.0 * jnp.maximum(c_sel, c_win) <= MAX_SHIFTED_RANGE).astype(jnp.int32).reshape(1)
        tails = jnp.zeros((2, HEAD_DIM), F32).at[:, SHIFT_LANE - HEAD_DIM].set(
            -jnp.stack([c_sel, c_win]))

        mod3 = _adaln(c, w_ada[layer], b_ada[layer]).reshape(bsz, 6, d)
        q, kvc, ks, vs, kw, vw, gate, bg, u = _inproj(
            x, mod3, norm1_gain[layer].reshape(1, d), w_p, seg,
            two(q_gain[layer]), two(k_sel_gain[layer]), two(k_win_gain[layer]), tails)
        kct, vc = _compress(kvc, w1k, w1v, w2k, w2v, pk2, pv2, seg, two(k_cmp_gain[layer]))
        y_attn = _attn(flag, q, kct, vc, ks, vs, kw, vw, gate, cb, b0, b1, w2m, ovlt)
        x = _outproj(y_attn, bg, u, conv_w[layer], attn_out_gain[layer].reshape(1, D_ATTN),
                     conv_out_gain[layer].reshape(1, D_CONV), w_out[layer].astype(BF16), x, mod3)
        x = _ffn(x, mod3, norm2_gain[layer].reshape(1, d), w_ff1[layer].astype(BF16),
                 w_ff3[layer].astype(BF16), w_ff2[layer].astype(BF16))
    return x
```

```python
import math

import numpy as np
import jax
import jax.numpy as jnp
from jax import lax
from jax.experimental import pallas as pl
from jax.experimental.pallas import tpu as pltpu

F32 = jnp.float32
BF16 = jnp.bfloat16

D_MODEL = 1024
HEAD_DIM = 64
N_HEADS = 8
N_GROUPS = 2
HPG = N_HEADS // N_GROUPS
D_ATTN = N_HEADS * HEAD_DIM
D_KV = N_GROUPS * HEAD_DIM
N_BRANCH = 3
D_CONV = D_MODEL - D_ATTN
CONV_WIDTH = 3
CMP_BLOCK = 32
CMP_STRIDE = 16
CMP_HIDDEN = 256
SEL_BLOCK = 64
SEL_TOPK = 16
N_LOCAL_FORCED = 2
WINDOW = 512
N_BUCKETS = 32
MAX_DISTANCE = 128
EPS = 1e-6
NEG_INF = -1e30
FORCED_SCORE = 1e6

LANES = 128
TQ = 256
ROWS = HPG * TQ
N_CMP_PAD = 128
SEL_LANE0 = 64
SHIFT_LANE = 96
V_LANES = 256
LOG2E = math.log2(math.e)
MAX_SHIFTED_RANGE = 100.0
VMEM_LIMIT = 56 * 1024 * 1024

_NT = (((1,), (1,)), ((), ()))


def _dot(a, b):
    return jnp.dot(a, b, preferred_element_type=F32)


def _silu(v):
    return v * (1.0 / (1.0 + jnp.exp(-v)))


def _sigmoid(v):
    return 1.0 / (1.0 + jnp.exp(-v))


def _seg_mean_sq(v, seg):
    sq = v * v
    hi = sq.astype(BF16)
    lo = (sq - hi.astype(F32)).astype(BF16)
    return _dot(hi, seg) + _dot(lo, seg)


def _adaln_kernel(c_ref, w_ref, b_ref, o_ref):
    sc = _silu(c_ref[...]).astype(BF16)
    o_ref[...] = _dot(sc, w_ref[...].astype(BF16)) + b_ref[...]


def _adaln(c, w_ada, b_ada):
    bsz, d = c.shape
    n = w_ada.shape[1]
    tn = 1536
    return pl.pallas_call(
        _adaln_kernel,
        grid=(n // tn,),
        in_specs=[pl.BlockSpec((bsz, d), lambda j: (0, 0)),
                  pl.BlockSpec((d, tn), lambda j: (0, j)),
                  pl.BlockSpec((1, tn), lambda j: (0, j))],
        out_specs=pl.BlockSpec((bsz, tn), lambda j: (0, j)),
        out_shape=jax.ShapeDtypeStruct((bsz, n), F32),
        compiler_params=pltpu.CompilerParams(
            dimension_semantics=("arbitrary",), vmem_limit_bytes=VMEM_LIMIT),
        name="adaln",
    )(c, w_ada, b_ada.reshape(1, n))


C_Q, C_KVC, C_KK, C_VV, C_GATE, C_BG, C_CG, C_XT, C_END = (
    0, 512, 768, 1024, 1280, 1536, 2048, 2560, 3072)
MXU_N = 256
TS_IN = 512


def _inproj_kernel(x_ref, mod_ref, g1_ref, w_ref, seg_ref, gq_ref, gkk_ref, tail_ref,
                   q_ref, kvc_ref, ks_ref, vs_ref, kw_ref, vw_ref, gate_ref, bg_ref, u_ref):
    ts = x_ref.shape[1]
    x = x_ref[0]
    ms = jnp.mean(x * x, axis=-1, keepdims=True)
    y = x * lax.rsqrt(ms + EPS) * g1_ref[...]
    h = y * (1.0 + mod_ref[0, 1:2, :]) + mod_ref[0, 0:1, :]
    hb = h.astype(BF16)
    seg = seg_ref[...]

    def proj(c0, c1):
        return _dot(hb, w_ref[:, c0:c1])

    lane = lax.broadcasted_iota(jnp.int32, (ts, HEAD_DIM), 1)
    ones192 = jnp.ones((ts, V_LANES - HEAD_DIM), F32)
    q_tail = jnp.where(lane == SHIFT_LANE - HEAD_DIM, 1.0, 0.0).astype(F32)

    heads_per_tile = MXU_N // HEAD_DIM
    for c in range(D_ATTN // MXU_N):
        v = proj(C_Q + c * MXU_N, C_Q + (c + 1) * MXU_N)
        vn = v * lax.rsqrt(_seg_mean_sq(v, seg) + EPS) * gq_ref[...] * (HEAD_DIM ** -0.5 * LOG2E)
        for hh in range(heads_per_tile):
            qh = jnp.concatenate([vn[:, hh * HEAD_DIM:(hh + 1) * HEAD_DIM], q_tail], axis=-1)
            q_ref[0, c * heads_per_tile + hh] = qh.astype(BF16)

    v = proj(C_KVC, C_KK)
    kvc_ref[0, 0] = v[:, :D_KV]
    kvc_ref[0, 1] = v[:, D_KV:]

    row = pl.program_id(1) * ts + lax.broadcasted_iota(jnp.int32, (ts, HEAD_DIM), 0)
    onehot = jnp.where(lane == row // SEL_BLOCK, 1.0, 0.0).astype(F32) + tail_ref[0:1, :]
    win_tail = jnp.broadcast_to(tail_ref[1:2, :], (ts, HEAD_DIM))
    v = proj(C_KK, C_VV)
    vn = v * lax.rsqrt(_seg_mean_sq(v, seg) + EPS) * gkk_ref[...]
    for g in range(N_GROUPS):
        ks_ref[0, g] = jnp.concatenate(
            [vn[:, g * HEAD_DIM:(g + 1) * HEAD_DIM], onehot], axis=-1).astype(BF16)
        kw_ref[0, g] = jnp.concatenate(
            [vn[:, D_KV + g * HEAD_DIM:D_KV + (g + 1) * HEAD_DIM], win_tail], axis=-1).astype(BF16)
    v = proj(C_VV, C_GATE)
    for g in range(N_GROUPS):
        vs_ref[0, g] = jnp.concatenate(
            [v[:, g * HEAD_DIM:(g + 1) * HEAD_DIM], ones192], axis=-1).astype(BF16)
        vw_ref[0, g] = jnp.concatenate(
            [v[:, D_KV + g * HEAD_DIM:D_KV + (g + 1) * HEAD_DIM], ones192], axis=-1).astype(BF16)

    v = _sigmoid(proj(C_GATE, C_BG))
    for g in range(N_GROUPS):
        gate_ref[0, g] = v[:, g * LANES:(g + 1) * LANES]

    bg_ref[0] = proj(C_BG, C_CG).astype(BF16)
    u_ref[0] = (proj(C_CG, C_XT) * proj(C_XT, C_END)).astype(BF16)


def _inproj(x, mod3, g1, w_p, seg, gq4, gkk4, tails):
    bsz, seq, d = x.shape
    ts = TS_IN
    const2 = lambda b, i: (0, 0)
    k_shape = jax.ShapeDtypeStruct((bsz, N_GROUPS, seq, LANES), BF16)
    k_spec = pl.BlockSpec((1, N_GROUPS, ts, LANES), lambda b, i: (b, 0, i, 0))
    v_shape = jax.ShapeDtypeStruct((bsz, N_GROUPS, seq, V_LANES), BF16)
    v_spec = pl.BlockSpec((1, N_GROUPS, ts, V_LANES), lambda b, i: (b, 0, i, 0))
    return pl.pallas_call(
        _inproj_kernel,
        grid=(bsz, seq // ts),
        in_specs=[pl.BlockSpec((1, ts, d), lambda b, i: (b, i, 0)),
                  pl.BlockSpec((1, 6, d), lambda b, i: (b, 0, 0)),
                  pl.BlockSpec((1, d), const2),
                  pl.BlockSpec(w_p.shape, const2, pipeline_mode=pl.Buffered(1)),
                  pl.BlockSpec(seg.shape, const2),
                  pl.BlockSpec(gq4.shape, const2),
                  pl.BlockSpec(gkk4.shape, const2),
                  pl.BlockSpec(tails.shape, const2)],
        out_specs=[pl.BlockSpec((1, N_HEADS, ts, LANES), lambda b, i: (b, 0, i, 0)),
                   pl.BlockSpec((1, 2, ts, D_KV), lambda b, i: (b, 0, i, 0)),
                   k_spec, v_spec, k_spec, v_spec,
                   pl.BlockSpec((1, N_GROUPS, ts, LANES), lambda b, i: (b, 0, i, 0)),
                   pl.BlockSpec((1, ts, D_CONV), lambda b, i: (b, i, 0)),
                   pl.BlockSpec((1, ts, D_CONV), lambda b, i: (b, i, 0))],
        out_shape=[jax.ShapeDtypeStruct((bsz, N_HEADS, seq, LANES), BF16),
                   jax.ShapeDtypeStruct((bsz, 2, seq, D_KV), F32),
                   k_shape, v_shape, k_shape, v_shape,
                   jax.ShapeDtypeStruct((bsz, N_GROUPS, seq, LANES), F32),
                   jax.ShapeDtypeStruct((bsz, seq, D_CONV), BF16),
                   jax.ShapeDtypeStruct((bsz, seq, D_CONV), BF16)],
        compiler_params=pltpu.CompilerParams(
            dimension_semantics=("arbitrary", "arbitrary"), vmem_limit_bytes=VMEM_LIMIT),
        name="inproj",
    )(x, mod3, g1, w_p, seg, gq4, gkk4, tails)


def _compress_kernel(kvc_ref, w1k_ref, w1v_ref, w2k_ref, w2v_ref, pk_ref, pv_ref, seg_ref,
                     gk_ref, kct_ref, vc_ref):
    n_rows = kvc_ref.shape[2] // CMP_STRIDE

    def branch(which, w1_ref, w2_ref, pos_ref):
        p1 = jnp.zeros((n_rows, N_GROUPS * CMP_HIDDEN), F32)
        p2 = jnp.zeros((n_rows, N_GROUPS * CMP_HIDDEN), F32)
        for r in range(CMP_STRIDE):
            rows = kvc_ref[0, which, pl.ds(r, n_rows, stride=CMP_STRIDE), :]
            a1 = (rows + pos_ref[r:r + 1, :]).astype(BF16)
            a2 = (rows + pos_ref[CMP_STRIDE + r:CMP_STRIDE + r + 1, :]).astype(BF16)
            p1 = p1 + _dot(a1, w1_ref[r])
            p2 = p2 + _dot(a2, w1_ref[CMP_STRIDE + r])
        hid = p1 + pltpu.roll(p2, n_rows - 1, 0)
        return _dot(_silu(hid).astype(BF16), w2_ref[...])

    kc = branch(0, w1k_ref, w2k_ref, pk_ref)
    kc = kc * lax.rsqrt(_seg_mean_sq(kc, seg_ref[...]) + EPS) * gk_ref[...]
    kct = kc.T
    zpad = jnp.zeros((HEAD_DIM, n_rows), F32)
    vcv = branch(1, w1v_ref, w2v_ref, pv_ref)
    zlane = jnp.zeros((n_rows, HEAD_DIM), F32)
    for g in range(N_GROUPS):
        kct_ref[0, g] = jnp.concatenate(
            [kct[g * HEAD_DIM:(g + 1) * HEAD_DIM, :], zpad], axis=0).astype(BF16)
        vc_ref[0, g] = jnp.concatenate(
            [vcv[:, g * HEAD_DIM:(g + 1) * HEAD_DIM], zlane], axis=-1).astype(BF16)


def _compress(kvc, w1k, w1v, w2k, w2v, pk2, pv2, seg, gk2):
    bsz, _, seq, _ = kvc.shape
    c3 = lambda b: (0, 0, 0)
    c2 = lambda b: (0, 0)
    out_shape = jax.ShapeDtypeStruct((bsz, N_GROUPS, N_CMP_PAD, LANES), BF16)
    out_spec = pl.BlockSpec((1, N_GROUPS, N_CMP_PAD, LANES), lambda b: (b, 0, 0, 0))
    return pl.pallas_call(
        _compress_kernel,
        grid=(bsz,),
        in_specs=[pl.BlockSpec((1, 2, seq, D_KV), lambda b: (b, 0, 0, 0)),
                  pl.BlockSpec(w1k.shape, c3), pl.BlockSpec(w1v.shape, c3),
                  pl.BlockSpec(w2k.shape, c2), pl.BlockSpec(w2v.shape, c2),
                  pl.BlockSpec(pk2.shape, c2), pl.BlockSpec(pv2.shape, c2),
                  pl.BlockSpec(seg.shape, c2), pl.BlockSpec(gk2.shape, c2)],
        out_specs=[out_spec, out_spec],
        out_shape=[out_shape, out_shape],
        compiler_params=pltpu.CompilerParams(
            dimension_semantics=("arbitrary",), vmem_limit_bytes=VMEM_LIMIT),
        name="compress",
    )(kvc, w1k, w1v, w2k, w2v, pk2, pv2, seg, gk2)


def _logits(qa, k_rows, bias):
    s = lax.dot_general(qa, k_rows, _NT, preferred_element_type=F32)
    return s if bias is None else s + bias


def _shifted_pv(qa, k_rows, v_rows, bias, keep=None):
    if keep is not None:
        v_rows = v_rows * keep.astype(BF16)
    return _dot(jnp.exp2(_logits(qa, k_rows, bias)).astype(BF16), v_rows)


def _online_update(qa, k_tile, v_tile, bias, acc_ref, m_ref):
    s = _logits(qa, k_tile, bias)
    m_prev = m_ref[...]
    m_new = jnp.maximum(m_prev, jnp.max(s, axis=-1, keepdims=True))
    alpha = jnp.exp2(m_prev - m_new)
    p = jnp.exp2(s - jnp.concatenate([m_new] * (TQ // LANES), axis=-1))
    acc_ref[...] = (acc_ref[...] * jnp.concatenate([alpha] * (V_LANES // LANES), axis=-1)
                    + _dot(p.astype(BF16), v_tile))
    m_ref[...] = m_new


def _selection_bias(pc, qi, ovlt, n_sel):
    psum = pc[0:TQ] + pc[TQ:2 * TQ] + pc[2 * TQ:3 * TQ] + pc[3 * TQ:4 * TQ]
    hi = psum.astype(BF16)
    lo = (psum - hi.astype(F32)).astype(BF16)
    pslc_t = (lax.dot_general(ovlt, hi, _NT, preferred_element_type=F32)
              + lax.dot_general(ovlt, lo, _NT, preferred_element_type=F32))
    top_k = min(SEL_TOPK, n_sel)
    sub = 8
    j = lax.broadcasted_iota(jnp.int32, (n_sel, TQ), 0)
    t = qi * TQ + lax.broadcasted_iota(jnp.int32, (n_sel, TQ), 1)
    dist = jnp.right_shift(t, int(math.log2(SEL_BLOCK))) - j
    score = jnp.where(dist < N_LOCAL_FORCED, FORCED_SCORE, pslc_t[SEL_LANE0:SEL_LANE0 + n_sel, :])
    score = jnp.where(j == 0, FORCED_SCORE, score)
    score = jnp.where(dist >= 0, score, NEG_INF)
    groups = [score[a * sub:(a + 1) * sub] for a in range(n_sel // sub)]
    cnts = [jnp.zeros((sub, TQ), F32) for _ in groups]
    j_sub = lax.broadcasted_iota(jnp.int32, (sub, TQ), 0)
    for i in range(n_sel):
        row = jnp.broadcast_to(score[i:i + 1, :], (sub, TQ))
        for a, grp in enumerate(groups):
            if a * sub > i:
                beats = jnp.where(row >= grp, 1.0, 0.0)
            elif a * sub + sub - 1 <= i:
                beats = jnp.where(row > grp, 1.0, 0.0)
            else:
                beats = jnp.where(j_sub + a * sub > i, jnp.where(row >= grp, 1.0, 0.0),
                                  jnp.where(row > grp, 1.0, 0.0))
            cnts[a] = cnts[a] + beats
    sel_t = jnp.where(jnp.concatenate(cnts, axis=0) < top_k, 0.0, NEG_INF)
    return jnp.concatenate(
        [jnp.zeros((SEL_LANE0, TQ), F32), sel_t,
         jnp.zeros((LANES - SEL_LANE0 - n_sel, TQ), F32)], axis=0).T


def _attn_step(shifted, q_ref, kct_ref, vc_ref, ks_ref, vs_ref, kw_ref, vw_ref, gate_ref,
               cb_ref, b0_ref, b1_ref, w2_ref, ovlt_ref, o_ref, qa_ref, acc_s, m_s, acc_w, m_w):
    qi = pl.program_id(1)
    groups = range(N_GROUPS)
    q = [q_ref[0, g * HPG:(g + 1) * HPG].reshape(ROWS, LANES) for g in groups]

    def key_rows(ref, g, kt, n_tiles=1):
        return ref[0, g, pl.ds(pl.multiple_of(kt * TQ, TQ), n_tiles * TQ), :]

    o_c = []
    for g in groups:
        s = _dot(q[g], kct_ref[0, g]) + cb_ref[g, 0]
        m = jnp.max(s, axis=-1, keepdims=True)
        e = jnp.exp2(s - m)
        l = jnp.sum(e, axis=-1, keepdims=True)
        pc = e * jnp.where(m > 0.5 * NEG_INF, 1.0 / l, 0.0)
        o_c.append(_dot(pc.astype(BF16), vc_ref[0, g]))
        selbias = _selection_bias(pc, qi, ovlt_ref[...], ks_ref.shape[2] // SEL_BLOCK)
        for h in range(HPG):
            qa_ref[g, h * TQ:(h + 1) * TQ, :] = (
                q_ref[0, g * HPG + h].astype(F32) + selbias).astype(BF16)

    kt1 = jnp.maximum(qi - 1, 0)
    kt2 = jnp.maximum(qi - 2, 0)
    if shifted:
        has1 = jnp.where(qi >= 1, 1.0, 0.0)
        has2 = jnp.where(qi >= 2, 1.0, 0.0)
        a_w = [_shifted_pv(q[g], key_rows(kw_ref, g, kt2), key_rows(vw_ref, g, kt2), w2_ref[...], has2)
               + _shifted_pv(q[g], key_rows(kw_ref, g, kt1), key_rows(vw_ref, g, kt1), b1_ref[g], has1)
               + _shifted_pv(q[g], key_rows(kw_ref, g, qi), key_rows(vw_ref, g, qi), b0_ref[g])
               for g in groups]
        acc_s[...] = jnp.zeros(acc_s.shape, F32)
        n_plain = jnp.maximum(qi - 1, 0)

        def pair_body(it, carry):
            for g in groups:
                acc_s[g] += _shifted_pv(qa_ref[g], key_rows(ks_ref, g, 2 * it, 2),
                                        key_rows(vs_ref, g, 2 * it, 2), None)
            return carry

        lax.fori_loop(0, n_plain // 2, pair_body, 0)
        odd = jnp.where(n_plain % 2 == 1, 1.0, 0.0)
        a_s = [acc_s[g]
               + _shifted_pv(qa_ref[g], key_rows(ks_ref, g, kt2), key_rows(vs_ref, g, kt2), None, odd)
               + _shifted_pv(qa_ref[g], key_rows(ks_ref, g, kt1), key_rows(vs_ref, g, kt1), b1_ref[g], has1)
               + _shifted_pv(qa_ref[g], key_rows(ks_ref, g, qi), key_rows(vs_ref, g, qi), b0_ref[g])
               for g in groups]
    else:
        a_s, a_w = [], []
        for g in groups:
            qa = qa_ref[g]
            sel = (acc_s.at[g], m_s.at[g])
            win = (acc_w.at[g], m_w.at[g])
            for acc, m_ref in (sel, win):
                acc[...] = jnp.zeros(acc.shape, F32)
                m_ref[...] = jnp.full(m_ref.shape, NEG_INF, F32)

            def sel_body(kt, carry, g=g, qa=qa, sel=sel):
                _online_update(qa, key_rows(ks_ref, g, kt), key_rows(vs_ref, g, kt), None, *sel)
                return carry

            lax.fori_loop(0, qi - 1, sel_body, 0)

            @pl.when(qi >= 2)
            def _(g=g, win=win):
                _online_update(q[g], key_rows(kw_ref, g, kt2), key_rows(vw_ref, g, kt2), w2_ref[...], *win)

            @pl.when(qi >= 1)
            def _(g=g, qa=qa, sel=sel, win=win):
                _online_update(qa, key_rows(ks_ref, g, kt1), key_rows(vs_ref, g, kt1), b1_ref[g], *sel)
                _online_update(q[g], key_rows(kw_ref, g, kt1), key_rows(vw_ref, g, kt1), b1_ref[g], *win)

            _online_update(qa, key_rows(ks_ref, g, qi), key_rows(vs_ref, g, qi), b0_ref[g], *sel)
            _online_update(q[g], key_rows(kw_ref, g, qi), key_rows(vw_ref, g, qi), b0_ref[g], *win)
            a_s.append(acc_s[g])
            a_w.append(acc_w[g])

    outs = []
    for g in groups:
        o_s = a_s[g][:, :LANES] / a_s[g][:, LANES:]
        o_w = a_w[g][:, :LANES] / a_w[g][:, LANES:]
        gates = gate_ref[0, g]
        for h in range(HPG):
            rows = slice(h * TQ, (h + 1) * TQ)
            g_c = gates[:, h * N_BRANCH + 0:h * N_BRANCH + 1]
            g_s = gates[:, h * N_BRANCH + 1:h * N_BRANCH + 2]
            g_w = gates[:, h * N_BRANCH + 2:h * N_BRANCH + 3]
            o_h = g_c * o_c[g][rows] + g_s * o_s[rows] + g_w * o_w[rows]
            outs.append(o_h[:, :HEAD_DIM])
    o_ref[0] = jnp.concatenate(outs, axis=-1)


def _attn_kernel(flag_ref, *refs):
    use_shift = flag_ref[0] == 1

    @pl.when(use_shift)
    def _():
        _attn_step(True, *refs)

    @pl.when(jnp.logical_not(use_shift))
    def _():
        _attn_step(False, *refs)


def _attn(flag, q, kct, vc, ks, vs, kw, vw, gate, cb, b0, b1, w2, ovlt):
    bsz, _, seq, _ = q.shape
    nq = seq // TQ
    per_batch = lambda b, i: (b, 0, 0, 0)
    k_spec = pl.BlockSpec((1, N_GROUPS, seq, LANES), per_batch)
    v_spec = pl.BlockSpec((1, N_GROUPS, seq, V_LANES), per_batch)
    cmp_spec = pl.BlockSpec((1, N_GROUPS, N_CMP_PAD, LANES), per_batch)
    const = lambda shape: pl.BlockSpec(shape, lambda b, i: (0,) * len(shape),
                                       pipeline_mode=pl.Buffered(1))
    return pl.pallas_call(
        _attn_kernel,
        grid=(bsz, nq),
        in_specs=[pl.BlockSpec(memory_space=pltpu.SMEM),
                  pl.BlockSpec((1, N_HEADS, TQ, LANES), lambda b, i: (b, 0, i, 0)),
                  cmp_spec, cmp_spec, k_spec, v_spec, k_spec, v_spec,
                  pl.BlockSpec((1, N_GROUPS, TQ, LANES), lambda b, i: (b, 0, i, 0)),
                  pl.BlockSpec((N_GROUPS, 1, ROWS, N_CMP_PAD), lambda b, i: (0, i, 0, 0)),
                  const(b0.shape), const(b1.shape), const(w2.shape), const(ovlt.shape)],
        out_specs=pl.BlockSpec((1, TQ, D_ATTN), lambda b, i: (b, i, 0)),
        out_shape=jax.ShapeDtypeStruct((bsz, seq, D_ATTN), F32),
        scratch_shapes=[pltpu.VMEM((N_GROUPS, ROWS, LANES), BF16),
                        pltpu.VMEM((N_GROUPS, ROWS, V_LANES), F32),
                        pltpu.VMEM((N_GROUPS, ROWS, LANES), F32),
                        pltpu.VMEM((N_GROUPS, ROWS, V_LANES), F32),
                        pltpu.VMEM((N_GROUPS, ROWS, LANES), F32)],
        compiler_params=pltpu.CompilerParams(
            dimension_semantics=("arbitrary", "arbitrary"),
            vmem_limit_bytes=VMEM_LIMIT),
        name="attn",
    )(flag, q, kct, vc, ks, vs, kw, vw, gate, cb, b0, b1, w2, ovlt)


TS_OUT = 512
HALO = 16


def _outproj_ffn_kernel(ya_ref, bg_ref, u_ref, uh_ref, cw_ref, ga_ref, gc_ref, w_ref, x_ref, mod_ref,
                        g2_ref, w1_ref, w3_ref, w2_ref, o_ref):
    u = u_ref[0].astype(F32)
    halo = uh_ref[0].astype(F32) * jnp.where(pl.program_id(1) > 0, 1.0, 0.0)
    h1 = halo[HALO - 1:HALO, :]
    h2 = halo[HALO - 2:HALO - 1, :]
    row = lax.broadcasted_iota(jnp.int32, u.shape, 0)
    u1 = jnp.where(row == 0, h1, pltpu.roll(u, 1, 0))
    u2 = jnp.where(row == 0, h2, jnp.where(row == 1, h1, pltpu.roll(u, 2, 0)))
    conv = u2 * cw_ref[0:1, :] + u1 * cw_ref[1:2, :] + u * cw_ref[2:3, :]
    yc = bg_ref[0].astype(F32) * conv
    yc = yc * lax.rsqrt(jnp.mean(yc * yc, axis=-1, keepdims=True) + EPS) * gc_ref[...]
    ya = ya_ref[0]
    ya = ya * lax.rsqrt(jnp.mean(ya * ya, axis=-1, keepdims=True) + EPS) * ga_ref[...]
    proj = _dot(ya.astype(BF16), w_ref[0:D_ATTN, :]) + _dot(yc.astype(BF16), w_ref[D_ATTN:, :])
    x = x_ref[0] + mod_ref[0, 2:3, :] * proj

    ms = jnp.mean(x * x, axis=-1, keepdims=True)
    h = x * lax.rsqrt(ms + EPS) * g2_ref[...] * (1.0 + mod_ref[0, 4:5, :]) + mod_ref[0, 3:4, :]
    hb = h.astype(BF16)
    a = _dot(hb, w1_ref[...])
    b = _dot(hb, w3_ref[...])
    act = (_silu(a) * b).astype(BF16)
    o_ref[0] = x + mod_ref[0, 5:6, :] * _dot(act, w2_ref[...])


def _outproj_ffn(ya, bg, u, conv_w, ga, gc, w_out, x, mod3, g2, w1, w3, w2):
    bsz, seq, d = x.shape
    ts = TS_OUT
    c2 = lambda b, i: (0, 0)
    row_spec = lambda width: pl.BlockSpec((1, ts, width), lambda b, i: (b, i, 0))
    wspec = lambda w: pl.BlockSpec(w.shape, c2, pipeline_mode=pl.Buffered(1))
    return pl.pallas_call(
        _outproj_ffn_kernel,
        grid=(bsz, seq // ts),
        in_specs=[row_spec(D_ATTN), row_spec(D_CONV), row_spec(D_CONV),
                  pl.BlockSpec((1, HALO, D_CONV),
                               lambda b, i: (b, jnp.maximum(i * (ts // HALO) - 1, 0), 0)),
                  pl.BlockSpec(conv_w.shape, c2),
                  pl.BlockSpec((1, D_ATTN), c2), pl.BlockSpec((1, D_CONV), c2),
                  wspec(w_out),
                  row_spec(d),
                  pl.BlockSpec((1, 6, d), lambda b, i: (b, 0, 0)),
                  pl.BlockSpec((1, d), c2),
                  wspec(w1), wspec(w3), wspec(w2)],
        out_specs=row_spec(d),
        out_shape=jax.ShapeDtypeStruct((bsz, seq, d), F32),
        compiler_params=pltpu.CompilerParams(
            dimension_semantics=("arbitrary", "arbitrary"), vmem_limit_bytes=VMEM_LIMIT),
        name="outproj_ffn",
    )(ya, bg, u, u, conv_w, ga, gc, w_out, x, mod3, g2, w1, w3, w2)


def _t5_bucket_np(rel):
    n = np.maximum(rel, 0)
    max_exact = N_BUCKETS // 2
    nf = np.maximum(n, 1).astype(np.float32)
    large = max_exact + (np.log(nf / max_exact) / math.log(MAX_DISTANCE / max_exact)
                         * (N_BUCKETS - max_exact)).astype(np.int32)
    return np.where(n < max_exact, n, np.minimum(large, N_BUCKETS - 1)).astype(np.int32)


def _stack_heads(tab_rel, idx, mask):
    idx = jnp.asarray(idx)
    tab_b = tab_rel.T.reshape((N_HEADS, N_BUCKETS) + (1,) * idx.ndim)
    vals = jnp.zeros((N_HEADS,) + idx.shape, F32)
    for k in range(N_BUCKETS):
        vals = jnp.where(idx == k, tab_b[:, k], vals)
    vals = jnp.where(mask, vals, NEG_INF)
    vals = vals.reshape((N_GROUPS, HPG) + vals.shape[1:])
    vals = jnp.moveaxis(vals, 1, -3)
    return vals.reshape(vals.shape[:-3] + (HPG * vals.shape[-2], vals.shape[-1]))


def _bias_tables(rel_bias_table, seq):
    tab_rel = (rel_bias_table - rel_bias_table[N_BUCKETS - 1:N_BUCKETS, :]) * LOG2E
    i = np.arange(TQ)[:, None]
    jj = np.arange(TQ)[None, :]
    rel0 = i - jj
    b0 = _stack_heads(tab_rel, _t5_bucket_np(rel0), rel0 >= 0)
    rel1 = TQ + i - jj
    b1 = _stack_heads(tab_rel, _t5_bucket_np(rel1), rel1 < WINDOW + TQ)
    w2 = jnp.asarray(np.tile(np.where(jj > i, 0.0, NEG_INF).astype(np.float32), (HPG, 1)))
    nq = seq // TQ
    n_cmp = (seq - CMP_BLOCK) // CMP_STRIDE + 1
    shift = TQ // CMP_STRIDE
    assert nq * shift <= N_CMP_PAD and n_cmp * CMP_STRIDE + CMP_BLOCK - 1 > seq
    n_rel = np.arange(2 * N_CMP_PAD)[None, :] - N_CMP_PAD
    relc = i - (n_rel * CMP_STRIDE + CMP_BLOCK - 1)
    pat = _stack_heads(tab_rel, _t5_bucket_np(relc), relc >= 0)
    cb = jnp.stack([pat[:, :, N_CMP_PAD - qi * shift:2 * N_CMP_PAD - qi * shift]
                    for qi in range(nq)], axis=1)
    n_sel = seq // SEL_BLOCK
    cs = np.arange(n_cmp) * CMP_STRIDE
    ce = cs + CMP_BLOCK - 1
    ss = np.arange(n_sel) * SEL_BLOCK
    ov = np.clip(np.minimum(ce[:, None], ss[None, :] + SEL_BLOCK - 1)
                 - np.maximum(cs[:, None], ss[None, :]) + 1, 0, None) / CMP_STRIDE
    ovlt = np.zeros((LANES, N_CMP_PAD), np.float32)
    ovlt[SEL_LANE0:SEL_LANE0 + n_sel, :n_cmp] = ov.T
    return tab_rel, cb, b0, b1, w2, jnp.asarray(ovlt, BF16)


def _blockdiag2(w):
    z = jnp.zeros_like(w)
    return jnp.concatenate([jnp.concatenate([w, z], axis=-1),
                            jnp.concatenate([z, w], axis=-1)], axis=-2)


def kernel(x, c, w_ada, b_ada, norm1_gain, w_in, q_gain, k_cmp_gain, k_sel_gain, k_win_gain,
           cmp_pos_k, cmp_pos_v, w_ck1, w_ck2, w_cv1, w_cv2, rel_bias_table, conv_w,
           attn_out_gain, conv_out_gain, w_out, norm2_gain, w_ff1, w_ff3, w_ff2):
    bsz, seq, d = x.shape
    assert d == D_MODEL and seq % TQ == 0 and SEL_LANE0 + seq // SEL_BLOCK <= SHIFT_LANE
    assert (seq - CMP_BLOCK) // CMP_STRIDE + 1 <= N_CMP_PAD and seq // CMP_STRIDE == N_CMP_PAD

    seg_of = lambda n: jnp.asarray(np.kron(np.eye(n), np.ones((HEAD_DIM, HEAD_DIM))) / HEAD_DIM, BF16)
    seg, seg4 = seg_of(LANES // HEAD_DIM), seg_of(MXU_N // HEAD_DIM)
    tab_rel, cb, b0, b1, w2m, ovlt = _bias_tables(rel_bias_table, seq)
    bias_max = jnp.max(jnp.abs(tab_rel))
    two = lambda g: jnp.tile(g.reshape(1, HEAD_DIM), (1, 2))
    n_gate = N_BRANCH * HPG
    gate_pad = jnp.zeros((d, LANES - n_gate), F32)

    for layer in range(w_in.shape[0]):
        wi = w_in[layer]
        o_g = D_ATTN + 6 * D_KV
        kv = lambda n: wi[:, D_ATTN + n * D_KV:D_ATTN + (n + 1) * D_KV]
        w_p = jnp.concatenate(
            [wi[:, :D_ATTN], kv(0), kv(1), kv(2), kv(4), kv(3), kv(5),
             wi[:, o_g:o_g + n_gate], gate_pad,
             wi[:, o_g + n_gate:o_g + 2 * n_gate], gate_pad, wi[:, o_g + 2 * n_gate:]],
            axis=-1).astype(BF16)
        w1k = _blockdiag2(w_ck1[layer].reshape(CMP_BLOCK, HEAD_DIM, CMP_HIDDEN)).astype(BF16)
        w1v = _blockdiag2(w_cv1[layer].reshape(CMP_BLOCK, HEAD_DIM, CMP_HIDDEN)).astype(BF16)
        w2k = _blockdiag2(w_ck2[layer]).astype(BF16)
        w2v = _blockdiag2(w_cv2[layer]).astype(BF16)
        pk2 = jnp.tile(cmp_pos_k[layer], (1, 2))
        pv2 = jnp.tile(cmp_pos_v[layer], (1, 2))

        gq_max = jnp.max(jnp.abs(q_gain[layer]))
        bound = lambda gk: (HEAD_DIM ** 0.5 * LOG2E * 1.01) * gq_max * jnp.max(jnp.abs(gk)) + bias_max
        c_sel, c_win = bound(k_sel_gain[layer]), bound(k_win_gain[layer])
        flag = (2.0 * jnp.maximum(c_sel, c_win) <= MAX_SHIFTED_RANGE).astype(jnp.int32).reshape(1)
        tails = jnp.zeros((2, HEAD_DIM), F32).at[:, SHIFT_LANE - HEAD_DIM].set(
            -jnp.stack([c_sel, c_win]))

        mod3 = _adaln(c, w_ada[layer], b_ada[layer]).reshape(bsz, 6, d)
        q, kvc, ks, vs, kw, vw, gate, bg, u = _inproj(
            x, mod3, norm1_gain[layer].reshape(1, d), w_p, seg4,
            jnp.tile(q_gain[layer].reshape(1, HEAD_DIM), (1, MXU_N // HEAD_DIM)),
            jnp.concatenate([two(k_sel_gain[layer]), two(k_win_gain[layer])], axis=-1), tails)
        kct, vc = _compress(kvc, w1k, w1v, w2k, w2v, pk2, pv2, seg, two(k_cmp_gain[layer]))
        y_attn = _attn(flag, q, kct, vc, ks, vs, kw, vw, gate, cb, b0, b1, w2m, ovlt)
        x = _outproj_ffn(y_attn, bg, u, conv_w[layer], attn_out_gain[layer].reshape(1, D_ATTN),
                         conv_out_gain[layer].reshape(1, D_CONV), w_out[layer].astype(BF16), x, mod3,
                         norm2_gain[layer].reshape(1, d), w_ff1[layer].astype(BF16),
                         w_ff3[layer].astype(BF16), w_ff2[layer].astype(BF16))
    return x
```

```python
import math

import numpy as np
import jax
import jax.numpy as jnp
from jax import lax
from jax.experimental import pallas as pl
from jax.experimental.pallas import tpu as pltpu

F32 = jnp.float32
BF16 = jnp.bfloat16

D_MODEL = 1024
HEAD_DIM = 64
N_HEADS = 8
N_GROUPS = 2
HPG = N_HEADS // N_GROUPS
D_ATTN = N_HEADS * HEAD_DIM
D_KV = N_GROUPS * HEAD_DIM
N_BRANCH = 3
D_CONV = D_MODEL - D_ATTN
CONV_WIDTH = 3
CMP_BLOCK = 32
CMP_STRIDE = 16
CMP_HIDDEN = 256
SEL_BLOCK = 64
SEL_TOPK = 16
N_LOCAL_FORCED = 2
WINDOW = 512
N_BUCKETS = 32
MAX_DISTANCE = 128
EPS = 1e-6
NEG_INF = -1e30
FORCED_SCORE = 1e6

LANES = 128
TQ = 256
ROWS = HPG * TQ
N_CMP_PAD = 128
SEL_LANE0 = 64
SHIFT_LANE = 96
V_LANES = 256
LOG2E = math.log2(math.e)
MAX_SHIFTED_RANGE = 100.0
VMEM_LIMIT = 56 * 1024 * 1024

_NT = (((1,), (1,)), ((), ()))


def _dot(a, b):
    return jnp.dot(a, b, preferred_element_type=F32)


def _silu(v):
    return v * (1.0 / (1.0 + jnp.exp(-v)))


def _sigmoid(v):
    return 1.0 / (1.0 + jnp.exp(-v))


def _seg_mean_sq(v, seg):
    sq = v * v
    hi = sq.astype(BF16)
    lo = (sq - hi.astype(F32)).astype(BF16)
    return _dot(hi, seg) + _dot(lo, seg)


def _adaln_kernel(c_ref, w_ref, b_ref, o_ref):
    sc = _silu(c_ref[...]).astype(BF16)
    o_ref[...] = _dot(sc, w_ref[...].astype(BF16)) + b_ref[...]


def _adaln(c, w_ada, b_ada):
    bsz, d = c.shape
    n = w_ada.shape[1]
    tn = 1536
    return pl.pallas_call(
        _adaln_kernel,
        grid=(n // tn,),
        in_specs=[pl.BlockSpec((bsz, d), lambda j: (0, 0)),
                  pl.BlockSpec((d, tn), lambda j: (0, j)),
                  pl.BlockSpec((1, tn), lambda j: (0, j))],
        out_specs=pl.BlockSpec((bsz, tn), lambda j: (0, j)),
        out_shape=jax.ShapeDtypeStruct((bsz, n), F32),
        compiler_params=pltpu.CompilerParams(
            dimension_semantics=("arbitrary",), vmem_limit_bytes=VMEM_LIMIT),
        name="adaln",
    )(c, w_ada, b_ada.reshape(1, n))


C_Q, C_KVC, C_KK, C_VV, C_GATE, C_BG, C_CG, C_XT, C_END = (
    0, 512, 768, 1024, 1280, 1536, 2048, 2560, 3072)
MXU_N = 256
TS_IN = 512


def _inproj_kernel(x_ref, mod_ref, g1_ref, w_ref, seg_ref, gq_ref, gkk_ref, tail_ref,
                   q_ref, kvc_ref, ks_ref, vs_ref, kw_ref, vw_ref, gate_ref, bg_ref, u_ref):
    ts = x_ref.shape[1]
    x = x_ref[0]
    ms = jnp.mean(x * x, axis=-1, keepdims=True)
    y = x * lax.rsqrt(ms + EPS) * g1_ref[...]
    h = y * (1.0 + mod_ref[0, 1:2, :]) + mod_ref[0, 0:1, :]
    hb = h.astype(BF16)
    seg = seg_ref[...]

    def proj(c0, c1):
        return _dot(hb, w_ref[:, c0:c1])

    lane = lax.broadcasted_iota(jnp.int32, (ts, HEAD_DIM), 1)
    ones192 = jnp.ones((ts, V_LANES - HEAD_DIM), F32)
    q_tail = jnp.where(lane == SHIFT_LANE - HEAD_DIM, 1.0, 0.0).astype(F32)

    heads_per_tile = MXU_N // HEAD_DIM
    for c in range(D_ATTN // MXU_N):
        v = proj(C_Q + c * MXU_N, C_Q + (c + 1) * MXU_N)
        vn = v * lax.rsqrt(_seg_mean_sq(v, seg) + EPS) * gq_ref[...] * (HEAD_DIM ** -0.5 * LOG2E)
        for hh in range(heads_per_tile):
            qh = jnp.concatenate([vn[:, hh * HEAD_DIM:(hh + 1) * HEAD_DIM], q_tail], axis=-1)
            q_ref[0, c * heads_per_tile + hh] = qh.astype(BF16)

    v = proj(C_KVC, C_KK)
    kvc_ref[0, 0] = v[:, :D_KV]
    kvc_ref[0, 1] = v[:, D_KV:]

    row = pl.program_id(1) * ts + lax.broadcasted_iota(jnp.int32, (ts, HEAD_DIM), 0)
    onehot = jnp.where(lane == row // SEL_BLOCK, 1.0, 0.0).astype(F32) + tail_ref[0:1, :]
    win_tail = jnp.broadcast_to(tail_ref[1:2, :], (ts, HEAD_DIM))
    v = proj(C_KK, C_VV)
    vn = v * lax.rsqrt(_seg_mean_sq(v, seg) + EPS) * gkk_ref[...]
    for g in range(N_GROUPS):
        ks_ref[0, g] = jnp.concatenate(
            [vn[:, g * HEAD_DIM:(g + 1) * HEAD_DIM], onehot], axis=-1).astype(BF16)
        kw_ref[0, g] = jnp.concatenate(
            [vn[:, D_KV + g * HEAD_DIM:D_KV + (g + 1) * HEAD_DIM], win_tail], axis=-1).astype(BF16)
    v = proj(C_VV, C_GATE)
    for g in range(N_GROUPS):
        vs_ref[0, g] = jnp.concatenate(
            [v[:, g * HEAD_DIM:(g + 1) * HEAD_DIM], ones192], axis=-1).astype(BF16)
        vw_ref[0, g] = jnp.concatenate(
            [v[:, D_KV + g * HEAD_DIM:D_KV + (g + 1) * HEAD_DIM], ones192], axis=-1).astype(BF16)

    v = _sigmoid(proj(C_GATE, C_BG))
    for g in range(N_GROUPS):
        gate_ref[0, g] = v[:, g * LANES:(g + 1) * LANES]

    bg_ref[0] = proj(C_BG, C_CG).astype(BF16)
    u_ref[0] = (proj(C_CG, C_XT) * proj(C_XT, C_END)).astype(BF16)


def _inproj(x, mod3, g1, w_p, seg, gq4, gkk4, tails):
    bsz, seq, d = x.shape
    ts = TS_IN
    const2 = lambda b, i: (0, 0)
    k_shape = jax.ShapeDtypeStruct((bsz, N_GROUPS, seq, LANES), BF16)
    k_spec = pl.BlockSpec((1, N_GROUPS, ts, LANES), lambda b, i: (b, 0, i, 0))
    v_shape = jax.ShapeDtypeStruct((bsz, N_GROUPS, seq, V_LANES), BF16)
    v_spec = pl.BlockSpec((1, N_GROUPS, ts, V_LANES), lambda b, i: (b, 0, i, 0))
    return pl.pallas_call(
        _inproj_kernel,
        grid=(bsz, seq // ts),
        in_specs=[pl.BlockSpec((1, ts, d), lambda b, i: (b, i, 0)),
                  pl.BlockSpec((1, 6, d), lambda b, i: (b, 0, 0)),
                  pl.BlockSpec((1, d), const2),
                  pl.BlockSpec(w_p.shape, const2, pipeline_mode=pl.Buffered(1)),
                  pl.BlockSpec(seg.shape, const2),
                  pl.BlockSpec(gq4.shape, const2),
                  pl.BlockSpec(gkk4.shape, const2),
                  pl.BlockSpec(tails.shape, const2)],
        out_specs=[pl.BlockSpec((1, N_HEADS, ts, LANES), lambda b, i: (b, 0, i, 0)),
                   pl.BlockSpec((1, 2, ts, D_KV), lambda b, i: (b, 0, i, 0)),
                   k_spec, v_spec, k_spec, v_spec,
                   pl.BlockSpec((1, N_GROUPS, ts, LANES), lambda b, i: (b, 0, i, 0)),
                   pl.BlockSpec((1, ts, D_CONV), lambda b, i: (b, i, 0)),
                   pl.BlockSpec((1, ts, D_CONV), lambda b, i: (b, i, 0))],
        out_shape=[jax.ShapeDtypeStruct((bsz, N_HEADS, seq, LANES), BF16),
                   jax.ShapeDtypeStruct((bsz, 2, seq, D_KV), F32),
                   k_shape, v_shape, k_shape, v_shape,
                   jax.ShapeDtypeStruct((bsz, N_GROUPS, seq, LANES), F32),
                   jax.ShapeDtypeStruct((bsz, seq, D_CONV), BF16),
                   jax.ShapeDtypeStruct((bsz, seq, D_CONV), BF16)],
        compiler_params=pltpu.CompilerParams(
            dimension_semantics=("arbitrary", "arbitrary"), vmem_limit_bytes=VMEM_LIMIT),
        name="inproj",
    )(x, mod3, g1, w_p, seg, gq4, gkk4, tails)


def _compress_kernel(kvc_ref, w1k_ref, w1v_ref, w2k_ref, w2v_ref, pk_ref, pv_ref, seg_ref,
                     gk_ref, kct_ref, vc_ref):
    n_rows = kvc_ref.shape[2] // CMP_STRIDE

    def branch(which, w1_ref, w2_ref, pos_ref):
        p1 = jnp.zeros((n_rows, N_GROUPS * CMP_HIDDEN), F32)
        p2 = jnp.zeros((n_rows, N_GROUPS * CMP_HIDDEN), F32)
        for r in range(CMP_STRIDE):
            rows = kvc_ref[0, which, pl.ds(r, n_rows, stride=CMP_STRIDE), :]
            a1 = (rows + pos_ref[r:r + 1, :]).astype(BF16)
            a2 = (rows + pos_ref[CMP_STRIDE + r:CMP_STRIDE + r + 1, :]).astype(BF16)
            p1 = p1 + _dot(a1, w1_ref[r])
            p2 = p2 + _dot(a2, w1_ref[CMP_STRIDE + r])
        hid = p1 + pltpu.roll(p2, n_rows - 1, 0)
        return _dot(_silu(hid).astype(BF16), w2_ref[...])

    kc = branch(0, w1k_ref, w2k_ref, pk_ref)
    kc = kc * lax.rsqrt(_seg_mean_sq(kc, seg_ref[...]) + EPS) * gk_ref[...]
    kct = kc.T
    zpad = jnp.zeros((HEAD_DIM, n_rows), F32)
    vcv = branch(1, w1v_ref, w2v_ref, pv_ref)
    zlane = jnp.zeros((n_rows, HEAD_DIM), F32)
    for g in range(N_GROUPS):
        kct_ref[0, g] = jnp.concatenate(
            [kct[g * HEAD_DIM:(g + 1) * HEAD_DIM, :], zpad], axis=0).astype(BF16)
        vc_ref[0, g] = jnp.concatenate(
            [vcv[:, g * HEAD_DIM:(g + 1) * HEAD_DIM], zlane], axis=-1).astype(BF16)


def _compress(kvc, w1k, w1v, w2k, w2v, pk2, pv2, seg, gk2):
    bsz, _, seq, _ = kvc.shape
    c3 = lambda b: (0, 0, 0)
    c2 = lambda b: (0, 0)
    out_shape = jax.ShapeDtypeStruct((bsz, N_GROUPS, N_CMP_PAD, LANES), BF16)
    out_spec = pl.BlockSpec((1, N_GROUPS, N_CMP_PAD, LANES), lambda b: (b, 0, 0, 0))
    return pl.pallas_call(
        _compress_kernel,
        grid=(bsz,),
        in_specs=[pl.BlockSpec((1, 2, seq, D_KV), lambda b: (b, 0, 0, 0)),
                  pl.BlockSpec(w1k.shape, c3), pl.BlockSpec(w1v.shape, c3),
                  pl.BlockSpec(w2k.shape, c2), pl.BlockSpec(w2v.shape, c2),
                  pl.BlockSpec(pk2.shape, c2), pl.BlockSpec(pv2.shape, c2),
                  pl.BlockSpec(seg.shape, c2), pl.BlockSpec(gk2.shape, c2)],
        out_specs=[out_spec, out_spec],
        out_shape=[out_shape, out_shape],
        compiler_params=pltpu.CompilerParams(
            dimension_semantics=("arbitrary",), vmem_limit_bytes=VMEM_LIMIT),
        name="compress",
    )(kvc, w1k, w1v, w2k, w2v, pk2, pv2, seg, gk2)


def _logits(qa, k_rows, bias):
    s = lax.dot_general(qa, k_rows, _NT, preferred_element_type=F32)
    return s if bias is None else s + bias


def _shifted_pv(qa, k_rows, v_rows, bias, keep=None):
    if keep is not None:
        v_rows = v_rows * keep.astype(BF16)
    return _dot(jnp.exp2(_logits(qa, k_rows, bias)).astype(BF16), v_rows)


def _online_update(qa, k_tile, v_tile, bias, acc_ref, m_ref):
    s = _logits(qa, k_tile, bias)
    m_prev = m_ref[...]
    m_new = jnp.maximum(m_prev, jnp.max(s, axis=-1, keepdims=True))
    alpha = jnp.exp2(m_prev - m_new)
    p = jnp.exp2(s - jnp.concatenate([m_new] * (TQ // LANES), axis=-1))
    acc_ref[...] = (acc_ref[...] * jnp.concatenate([alpha] * (V_LANES // LANES), axis=-1)
                    + _dot(p.astype(BF16), v_tile))
    m_ref[...] = m_new


def _selection_bias(pc, qi, ovlt, n_sel):
    psum = pc[0:TQ] + pc[TQ:2 * TQ] + pc[2 * TQ:3 * TQ] + pc[3 * TQ:4 * TQ]
    hi = psum.astype(BF16)
    lo = (psum - hi.astype(F32)).astype(BF16)
    pslc_t = (lax.dot_general(ovlt, hi, _NT, preferred_element_type=F32)
              + lax.dot_general(ovlt, lo, _NT, preferred_element_type=F32))
    top_k = min(SEL_TOPK, n_sel)
    sub = 8
    j = lax.broadcasted_iota(jnp.int32, (n_sel, TQ), 0)
    t = qi * TQ + lax.broadcasted_iota(jnp.int32, (n_sel, TQ), 1)
    dist = jnp.right_shift(t, int(math.log2(SEL_BLOCK))) - j
    score = jnp.where(dist < N_LOCAL_FORCED, FORCED_SCORE, pslc_t[SEL_LANE0:SEL_LANE0 + n_sel, :])
    score = jnp.where(j == 0, FORCED_SCORE, score)
    score = jnp.where(dist >= 0, score, NEG_INF)
    groups = [score[a * sub:(a + 1) * sub] for a in range(n_sel // sub)]
    cnts = [jnp.zeros((sub, TQ), F32) for _ in groups]
    j_sub = lax.broadcasted_iota(jnp.int32, (sub, TQ), 0)
    for i in range(n_sel):
        row = jnp.broadcast_to(score[i:i + 1, :], (sub, TQ))
        for a, grp in enumerate(groups):
            if a * sub > i:
                beats = jnp.where(row >= grp, 1.0, 0.0)
            elif a * sub + sub - 1 <= i:
                beats = jnp.where(row > grp, 1.0, 0.0)
            else:
                beats = jnp.where(j_sub + a * sub > i, jnp.where(row >= grp, 1.0, 0.0),
                                  jnp.where(row > grp, 1.0, 0.0))
            cnts[a] = cnts[a] + beats
    sel_t = jnp.where(jnp.concatenate(cnts, axis=0) < top_k, 0.0, NEG_INF)
    return jnp.concatenate(
        [jnp.zeros((SEL_LANE0, TQ), F32), sel_t,
         jnp.zeros((LANES - SEL_LANE0 - n_sel, TQ), F32)], axis=0).T


def _attn_step(shifted, q_ref, kct_ref, vc_ref, ks_ref, vs_ref, kw_ref, vw_ref, gate_ref,
               cb_ref, b0_ref, b1_ref, w2_ref, ovlt_ref, o_ref, qa_ref, acc_s, m_s, acc_w, m_w):
    qi = pl.program_id(1)
    groups = range(N_GROUPS)
    q = [q_ref[0, g * HPG:(g + 1) * HPG].reshape(ROWS, LANES) for g in groups]

    def key_rows(ref, g, kt, n_tiles=1):
        return ref[0, g, pl.ds(pl.multiple_of(kt * TQ, TQ), n_tiles * TQ), :]

    def compressed_and_selection(which=groups):
        o_c = []
        for g in which:
            s = _dot(q[g], kct_ref[0, g]) + cb_ref[g, 0]
            m = jnp.max(s, axis=-1, keepdims=True)
            e = jnp.exp2(s - m)
            l = jnp.sum(e, axis=-1, keepdims=True)
            pc = e * jnp.where(m > 0.5 * NEG_INF, 1.0 / l, 0.0)
            o_c.append(_dot(pc.astype(BF16), vc_ref[0, g]))
            selbias = _selection_bias(pc, qi, ovlt_ref[...], ks_ref.shape[2] // SEL_BLOCK)
            for h in range(HPG):
                qa_ref[g, h * TQ:(h + 1) * TQ, :] = (
                    q_ref[0, g * HPG + h].astype(F32) + selbias).astype(BF16)
        return o_c

    kt1 = jnp.maximum(qi - 1, 0)
    kt2 = jnp.maximum(qi - 2, 0)
    if shifted:
        has1 = jnp.where(qi >= 1, 1.0, 0.0)
        has2 = jnp.where(qi >= 2, 1.0, 0.0)
        a_w, o_c = [], []
        for g in groups:
            a_w.append(
                _shifted_pv(q[g], key_rows(kw_ref, g, kt2), key_rows(vw_ref, g, kt2), w2_ref[...], has2)
                + _shifted_pv(q[g], key_rows(kw_ref, g, kt1), key_rows(vw_ref, g, kt1), b1_ref[g], has1)
                + _shifted_pv(q[g], key_rows(kw_ref, g, qi), key_rows(vw_ref, g, qi), b0_ref[g]))
            o_c += compressed_and_selection([g])
        acc_s[...] = jnp.zeros(acc_s.shape, F32)
        n_plain = jnp.maximum(qi - 1, 0)

        def pair_body(it, carry):
            for g in groups:
                acc_s[g] += _shifted_pv(qa_ref[g], key_rows(ks_ref, g, 2 * it, 2),
                                        key_rows(vs_ref, g, 2 * it, 2), None)
            return carry

        lax.fori_loop(0, n_plain // 2, pair_body, 0)
        odd = jnp.where(n_plain % 2 == 1, 1.0, 0.0)
        a_s = [acc_s[g]
               + _shifted_pv(qa_ref[g], key_rows(ks_ref, g, kt2), key_rows(vs_ref, g, kt2), None, odd)
               + _shifted_pv(qa_ref[g], key_rows(ks_ref, g, kt1), key_rows(vs_ref, g, kt1), b1_ref[g], has1)
               + _shifted_pv(qa_ref[g], key_rows(ks_ref, g, qi), key_rows(vs_ref, g, qi), b0_ref[g])
               for g in groups]
    else:
        o_c = compressed_and_selection()
        a_s, a_w = [], []
        for g in groups:
            qa = qa_ref[g]
            sel = (acc_s.at[g], m_s.at[g])
            win = (acc_w.at[g], m_w.at[g])
            for acc, m_ref in (sel, win):
                acc[...] = jnp.zeros(acc.shape, F32)
                m_ref[...] = jnp.full(m_ref.shape, NEG_INF, F32)

            def sel_body(kt, carry, g=g, qa=qa, sel=sel):
                _online_update(qa, key_rows(ks_ref, g, kt), key_rows(vs_ref, g, kt), None, *sel)
                return carry

            lax.fori_loop(0, qi - 1, sel_body, 0)

            @pl.when(qi >= 2)
            def _(g=g, win=win):
                _online_update(q[g], key_rows(kw_ref, g, kt2), key_rows(vw_ref, g, kt2), w2_ref[...], *win)

            @pl.when(qi >= 1)
            def _(g=g, qa=qa, sel=sel, win=win):
                _online_update(qa, key_rows(ks_ref, g, kt1), key_rows(vs_ref, g, kt1), b1_ref[g], *sel)
                _online_update(q[g], key_rows(kw_ref, g, kt1), key_rows(vw_ref, g, kt1), b1_ref[g], *win)

            _online_update(qa, key_rows(ks_ref, g, qi), key_rows(vs_ref, g, qi), b0_ref[g], *sel)
            _online_update(q[g], key_rows(kw_ref, g, qi), key_rows(vw_ref, g, qi), b0_ref[g], *win)
            a_s.append(acc_s[g])
            a_w.append(acc_w[g])

    outs = []
    for g in groups:
        o_s = a_s[g][:, :LANES] / a_s[g][:, LANES:]
        o_w = a_w[g][:, :LANES] / a_w[g][:, LANES:]
        gates = gate_ref[0, g]
        for h in range(HPG):
            rows = slice(h * TQ, (h + 1) * TQ)
            g_c = gates[:, h * N_BRANCH + 0:h * N_BRANCH + 1]
            g_s = gates[:, h * N_BRANCH + 1:h * N_BRANCH + 2]
            g_w = gates[:, h * N_BRANCH + 2:h * N_BRANCH + 3]
            o_h = g_c * o_c[g][rows] + g_s * o_s[rows] + g_w * o_w[rows]
            outs.append(o_h[:, :HEAD_DIM])
    o_ref[0] = jnp.concatenate(outs, axis=-1)


def _attn_kernel(flag_ref, *refs):
    use_shift = flag_ref[0] == 1

    @pl.when(use_shift)
    def _():
        _attn_step(True, *refs)

    @pl.when(jnp.logical_not(use_shift))
    def _():
        _attn_step(False, *refs)


def _attn(flag, q, kct, vc, ks, vs, kw, vw, gate, cb, b0, b1, w2, ovlt):
    bsz, _, seq, _ = q.shape
    nq = seq // TQ
    per_batch = lambda b, i: (b, 0, 0, 0)
    k_spec = pl.BlockSpec((1, N_GROUPS, seq, LANES), per_batch)
    v_spec = pl.BlockSpec((1, N_GROUPS, seq, V_LANES), per_batch)
    cmp_spec = pl.BlockSpec((1, N_GROUPS, N_CMP_PAD, LANES), per_batch)
    const = lambda shape: pl.BlockSpec(shape, lambda b, i: (0,) * len(shape),
                                       pipeline_mode=pl.Buffered(1))
    return pl.pallas_call(
        _attn_kernel,
        grid=(bsz, nq),
        in_specs=[pl.BlockSpec(memory_space=pltpu.SMEM),
                  pl.BlockSpec((1, N_HEADS, TQ, LANES), lambda b, i: (b, 0, i, 0)),
                  cmp_spec, cmp_spec, k_spec, v_spec, k_spec, v_spec,
                  pl.BlockSpec((1, N_GROUPS, TQ, LANES), lambda b, i: (b, 0, i, 0)),
                  pl.BlockSpec((N_GROUPS, 1, ROWS, N_CMP_PAD), lambda b, i: (0, i, 0, 0)),
                  const(b0.shape), const(b1.shape), const(w2.shape), const(ovlt.shape)],
        out_specs=pl.BlockSpec((1, TQ, D_ATTN), lambda b, i: (b, i, 0)),
        out_shape=jax.ShapeDtypeStruct((bsz, seq, D_ATTN), F32),
        scratch_shapes=[pltpu.VMEM((N_GROUPS, ROWS, LANES), BF16),
                        pltpu.VMEM((N_GROUPS, ROWS, V_LANES), F32),
                        pltpu.VMEM((N_GROUPS, ROWS, LANES), F32),
                        pltpu.VMEM((N_GROUPS, ROWS, V_LANES), F32),
                        pltpu.VMEM((N_GROUPS, ROWS, LANES), F32)],
        compiler_params=pltpu.CompilerParams(
            dimension_semantics=("arbitrary", "arbitrary"),
            vmem_limit_bytes=VMEM_LIMIT),
        name="attn",
    )(flag, q, kct, vc, ks, vs, kw, vw, gate, cb, b0, b1, w2, ovlt)


TS_OUT = 512
HALO = 16


def _outproj_ffn_kernel(ya_ref, bg_ref, u_ref, uh_ref, cw_ref, ga_ref, gc_ref, w_ref, x_ref, mod_ref,
                        g2_ref, w1_ref, w3_ref, w2_ref, o_ref):
    u = u_ref[0].astype(F32)
    halo = uh_ref[0].astype(F32) * jnp.where(pl.program_id(1) > 0, 1.0, 0.0)
    h1 = halo[HALO - 1:HALO, :]
    h2 = halo[HALO - 2:HALO - 1, :]
    row = lax.broadcasted_iota(jnp.int32, u.shape, 0)
    u1 = jnp.where(row == 0, h1, pltpu.roll(u, 1, 0))
    u2 = jnp.where(row == 0, h2, jnp.where(row == 1, h1, pltpu.roll(u, 2, 0)))
    conv = u2 * cw_ref[0:1, :] + u1 * cw_ref[1:2, :] + u * cw_ref[2:3, :]
    yc = bg_ref[0].astype(F32) * conv
    yc = yc * lax.rsqrt(jnp.mean(yc * yc, axis=-1, keepdims=True) + EPS) * gc_ref[...]
    ya = ya_ref[0]
    ya = ya * lax.rsqrt(jnp.mean(ya * ya, axis=-1, keepdims=True) + EPS) * ga_ref[...]
    proj = _dot(ya.astype(BF16), w_ref[0:D_ATTN, :]) + _dot(yc.astype(BF16), w_ref[D_ATTN:, :])
    x = x_ref[0] + mod_ref[0, 2:3, :] * proj

    ms = jnp.mean(x * x, axis=-1, keepdims=True)
    h = x * lax.rsqrt(ms + EPS) * g2_ref[...] * (1.0 + mod_ref[0, 4:5, :]) + mod_ref[0, 3:4, :]
    hb = h.astype(BF16)
    a = _dot(hb, w1_ref[...])
    b = _dot(hb, w3_ref[...])
    act = (_silu(a) * b).astype(BF16)
    o_ref[0] = x + mod_ref[0, 5:6, :] * _dot(act, w2_ref[...])


def _outproj_ffn(ya, bg, u, conv_w, ga, gc, w_out, x, mod3, g2, w1, w3, w2):
    bsz, seq, d = x.shape
    ts = TS_OUT
    c2 = lambda b, i: (0, 0)
    row_spec = lambda width: pl.BlockSpec((1, ts, width), lambda b, i: (b, i, 0))
    wspec = lambda w: pl.BlockSpec(w.shape, c2, pipeline_mode=pl.Buffered(1))
    return pl.pallas_call(
        _outproj_ffn_kernel,
        grid=(bsz, seq // ts),
        in_specs=[row_spec(D_ATTN), row_spec(D_CONV), row_spec(D_CONV),
                  pl.BlockSpec((1, HALO, D_CONV),
                               lambda b, i: (b, jnp.maximum(i * (ts // HALO) - 1, 0), 0)),
                  pl.BlockSpec(conv_w.shape, c2),
                  pl.BlockSpec((1, D_ATTN), c2), pl.BlockSpec((1, D_CONV), c2),
                  wspec(w_out),
                  row_spec(d),
                  pl.BlockSpec((1, 6, d), lambda b, i: (b, 0, 0)),
                  pl.BlockSpec((1, d), c2),
                  wspec(w1), wspec(w3), wspec(w2)],
        out_specs=row_spec(d),
        out_shape=jax.ShapeDtypeStruct((bsz, seq, d), F32),
        compiler_params=pltpu.CompilerParams(
            dimension_semantics=("arbitrary", "arbitrary"), vmem_limit_bytes=VMEM_LIMIT),
        name="outproj_ffn",
    )(ya, bg, u, u, conv_w, ga, gc, w_out, x, mod3, g2, w1, w3, w2)


def _t5_bucket_np(rel):
    n = np.maximum(rel, 0)
    max_exact = N_BUCKETS // 2
    nf = np.maximum(n, 1).astype(np.float32)
    large = max_exact + (np.log(nf / max_exact) / math.log(MAX_DISTANCE / max_exact)
                         * (N_BUCKETS - max_exact)).astype(np.int32)
    return np.where(n < max_exact, n, np.minimum(large, N_BUCKETS - 1)).astype(np.int32)


def _bias_of_rel(tab_rel, rel):
    idx = jnp.asarray(_t5_bucket_np(rel))
    tab_b = tab_rel.T.reshape((N_HEADS, N_BUCKETS) + (1,) * idx.ndim)
    vals = jnp.zeros((N_HEADS,) + idx.shape, F32)
    for k in range(N_BUCKETS):
        vals = jnp.where(idx == k, tab_b[:, k], vals)
    return jnp.where(jnp.asarray(rel >= 0), vals, NEG_INF)


def _toeplitz(vec, n_rows, n_cols):
    length = vec.shape[-1]
    assert length >= n_rows + n_cols - 1
    reps = -(-(n_rows * (length + 1)) // length)
    m = jnp.tile(vec, (1,) * (vec.ndim - 1) + (reps,))[..., :n_rows * (length + 1)]
    m = m.reshape(vec.shape[:-1] + (n_rows, length + 1))[..., :n_cols]
    return m[..., ::-1, :]


def _stack_heads(vals):
    return vals.reshape(N_GROUPS, HPG * vals.shape[-2], vals.shape[-1])


def _bias_tables(rel_bias_table, seq):
    tab_rel = (rel_bias_table - rel_bias_table[N_BUCKETS - 1:N_BUCKETS, :]) * LOG2E
    m = np.arange(2 * TQ - 1)
    b0 = _stack_heads(_toeplitz(_bias_of_rel(tab_rel, TQ - 1 - m), TQ, TQ))
    b1 = _stack_heads(_toeplitz(_bias_of_rel(tab_rel, 2 * TQ - 1 - m), TQ, TQ))
    i = np.arange(TQ)[:, None]
    jj = np.arange(TQ)[None, :]
    w2 = jnp.asarray(np.tile(np.where(jj > i, 0.0, NEG_INF).astype(np.float32), (HPG, 1)))
    nq = seq // TQ
    n_cmp = (seq - CMP_BLOCK) // CMP_STRIDE + 1
    shift = TQ // CMP_STRIDE
    assert nq * shift <= N_CMP_PAD and n_cmp * CMP_STRIDE + CMP_BLOCK - 1 > seq
    off = N_CMP_PAD * CMP_STRIDE - (CMP_BLOCK - 1)
    m = np.arange(shift + 2 * N_CMP_PAD - 1)[None, :]
    r = np.arange(CMP_STRIDE)[:, None]
    pat = _toeplitz(_bias_of_rel(tab_rel, CMP_STRIDE * (shift - 1 - m) + r + off),
                    shift, 2 * N_CMP_PAD)
    pat = _stack_heads(jnp.swapaxes(pat, 1, 2).reshape(N_HEADS, TQ, 2 * N_CMP_PAD))
    cb = jnp.stack([pat[:, :, N_CMP_PAD - qi * shift:2 * N_CMP_PAD - qi * shift]
                    for qi in range(nq)], axis=1)
    n_sel = seq // SEL_BLOCK
    cs = np.arange(n_cmp) * CMP_STRIDE
    ce = cs + CMP_BLOCK - 1
    ss = np.arange(n_sel) * SEL_BLOCK
    ov = np.clip(np.minimum(ce[:, None], ss[None, :] + SEL_BLOCK - 1)
                 - np.maximum(cs[:, None], ss[None, :]) + 1, 0, None) / CMP_STRIDE
    ovlt = np.zeros((LANES, N_CMP_PAD), np.float32)
    ovlt[SEL_LANE0:SEL_LANE0 + n_sel, :n_cmp] = ov.T
    return tab_rel, cb, b0, b1, w2, jnp.asarray(ovlt, BF16)


def _blockdiag2(w):
    z = jnp.zeros_like(w)
    return jnp.concatenate([jnp.concatenate([w, z], axis=-1),
                            jnp.concatenate([z, w], axis=-1)], axis=-2)


def kernel(x, c, w_ada, b_ada, norm1_gain, w_in, q_gain, k_cmp_gain, k_sel_gain, k_win_gain,
           cmp_pos_k, cmp_pos_v, w_ck1, w_ck2, w_cv1, w_cv2, rel_bias_table, conv_w,
           attn_out_gain, conv_out_gain, w_out, norm2_gain, w_ff1, w_ff3, w_ff2):
    bsz, seq, d = x.shape
    assert d == D_MODEL and seq % TQ == 0 and SEL_LANE0 + seq // SEL_BLOCK <= SHIFT_LANE
    assert (seq - CMP_BLOCK) // CMP_STRIDE + 1 <= N_CMP_PAD and seq // CMP_STRIDE == N_CMP_PAD

    seg_of = lambda n: jnp.asarray(np.kron(np.eye(n), np.ones((HEAD_DIM, HEAD_DIM))) / HEAD_DIM, BF16)
    seg, seg4 = seg_of(LANES // HEAD_DIM), seg_of(MXU_N // HEAD_DIM)
    tab_rel, cb, b0, b1, w2m, ovlt = _bias_tables(rel_bias_table, seq)
    bias_max = jnp.max(jnp.abs(tab_rel))
    two = lambda g: jnp.tile(g.reshape(1, HEAD_DIM), (1, 2))
    n_gate = N_BRANCH * HPG
    gate_pad = jnp.zeros((d, LANES - n_gate), F32)

    for layer in range(w_in.shape[0]):
        wi = w_in[layer]
        o_g = D_ATTN + 6 * D_KV
        kv = lambda n: wi[:, D_ATTN + n * D_KV:D_ATTN + (n + 1) * D_KV]
        w_p = jnp.concatenate(
            [wi[:, :D_ATTN], kv(0), kv(1), kv(2), kv(4), kv(3), kv(5),
             wi[:, o_g:o_g + n_gate], gate_pad,
             wi[:, o_g + n_gate:o_g + 2 * n_gate], gate_pad, wi[:, o_g + 2 * n_gate:]],
            axis=-1).astype(BF16)
        w1k = _blockdiag2(w_ck1[layer].reshape(CMP_BLOCK, HEAD_DIM, CMP_HIDDEN)).astype(BF16)
        w1v = _blockdiag2(w_cv1[layer].reshape(CMP_BLOCK, HEAD_DIM, CMP_HIDDEN)).astype(BF16)
        w2k = _blockdiag2(w_ck2[layer]).astype(BF16)
        w2v = _blockdiag2(w_cv2[layer]).astype(BF16)
        pk2 = jnp.tile(cmp_pos_k[layer], (1, 2))
        pv2 = jnp.tile(cmp_pos_v[layer], (1, 2))

        gq_max = jnp.max(jnp.abs(q_gain[layer]))
        bound = lambda gk: (HEAD_DIM ** 0.5 * LOG2E * 1.01) * gq_max * jnp.max(jnp.abs(gk)) + bias_max
        c_sel, c_win = bound(k_sel_gain[layer]), bound(k_win_gain[layer])
        flag = (2.0 * jnp.maximum(c_sel, c_win) <= MAX_SHIFTED_RANGE).astype(jnp.int32).reshape(1)
        tails = jnp.zeros((2, HEAD_DIM), F32).at[:, SHIFT_LANE - HEAD_DIM].set(
            -jnp.stack([c_sel, c_win]))

        mod3 = _adaln(c, w_ada[layer], b_ada[layer]).reshape(bsz, 6, d)
        q, kvc, ks, vs, kw, vw, gate, bg, u = _inproj(
            x, mod3, norm1_gain[layer].reshape(1, d), w_p, seg4,
            jnp.tile(q_gain[layer].reshape(1, HEAD_DIM), (1, MXU_N // HEAD_DIM)),
            jnp.concatenate([two(k_sel_gain[layer]), two(k_win_gain[layer])], axis=-1), tails)
        kct, vc = _compress(kvc, w1k, w1v, w2k, w2v, pk2, pv2, seg, two(k_cmp_gain[layer]))
        y_attn = _attn(flag, q, kct, vc, ks, vs, kw, vw, gate, cb, b0, b1, w2m, ovlt)
        x = _outproj_ffn(y_attn, bg, u, conv_w[layer], attn_out_gain[layer].reshape(1, D_ATTN),
                         conv_out_gain[layer].reshape(1, D_CONV), w_out[layer].astype(BF16), x, mod3,
                         norm2_gain[layer].reshape(1, d), w_ff1[layer].astype(BF16),
                         w_ff3[layer].astype(BF16), w_ff2[layer].astype(BF16))
    return x
```

```python
import math

import numpy as np
import jax
import jax.numpy as jnp
from jax import lax
from jax.experimental import pallas as pl
from jax.experimental.pallas import tpu as pltpu

F32 = jnp.float32
BF16 = jnp.bfloat16

D_MODEL = 1024
HEAD_DIM = 64
N_HEADS = 8
N_GROUPS = 2
HPG = N_HEADS // N_GROUPS
D_ATTN = N_HEADS * HEAD_DIM
D_KV = N_GROUPS * HEAD_DIM
N_BRANCH = 3
D_CONV = D_MODEL - D_ATTN
CONV_WIDTH = 3
CMP_BLOCK = 32
CMP_STRIDE = 16
CMP_HIDDEN = 256
SEL_BLOCK = 64
SEL_TOPK = 16
N_LOCAL_FORCED = 2
WINDOW = 512
N_BUCKETS = 32
MAX_DISTANCE = 128
EPS = 1e-6
NEG_INF = -1e30
FORCED_SCORE = 1e6

LANES = 128
TQ = 256
ROWS = HPG * TQ
N_CMP_PAD = 128
SEL_LANE0 = 64
SHIFT_LANE = 96
V_LANES = 256
LOG2E = math.log2(math.e)
MAX_SHIFTED_RANGE = 100.0
VMEM_LIMIT = 56 * 1024 * 1024

_NT = (((1,), (1,)), ((), ()))


def _dot(a, b):
    return jnp.dot(a, b, preferred_element_type=F32)


def _silu(v):
    return v * (1.0 / (1.0 + jnp.exp(-v)))


def _sigmoid(v):
    return 1.0 / (1.0 + jnp.exp(-v))


def _seg_mean_sq(v, seg):
    sq = v * v
    hi = sq.astype(BF16)
    lo = (sq - hi.astype(F32)).astype(BF16)
    return _dot(hi, seg) + _dot(lo, seg)


def _adaln_kernel(c_ref, w_ref, b_ref, o_ref):
    sc = _silu(c_ref[...]).astype(BF16)
    o_ref[...] = _dot(sc, w_ref[...].astype(BF16)) + b_ref[...]


def _adaln(c, w_ada, b_ada):
    bsz, d = c.shape
    n = w_ada.shape[1]
    tn = 1536
    return pl.pallas_call(
        _adaln_kernel,
        grid=(n // tn,),
        in_specs=[pl.BlockSpec((bsz, d), lambda j: (0, 0)),
                  pl.BlockSpec((d, tn), lambda j: (0, j)),
                  pl.BlockSpec((1, tn), lambda j: (0, j))],
        out_specs=pl.BlockSpec((bsz, tn), lambda j: (0, j)),
        out_shape=jax.ShapeDtypeStruct((bsz, n), F32),
        compiler_params=pltpu.CompilerParams(
            dimension_semantics=("arbitrary",), vmem_limit_bytes=VMEM_LIMIT),
        name="adaln",
    )(c, w_ada, b_ada.reshape(1, n))


C_Q, C_KVC, C_KK, C_VV, C_GATE, C_BG, C_CG, C_XT, C_END = (
    0, 512, 768, 1024, 1280, 1536, 2048, 2560, 3072)
MXU_N = 256
TS_IN = 512


def _inproj_kernel(x_ref, mod_ref, g1_ref, w_ref, seg_ref, gq_ref, gkk_ref, tail_ref,
                   q_ref, kvc_ref, ks_ref, vs_ref, kw_ref, vw_ref, gate_ref, bg_ref, u_ref):
    ts = x_ref.shape[1]
    x = x_ref[0]
    ms = jnp.mean(x * x, axis=-1, keepdims=True)
    y = x * lax.rsqrt(ms + EPS) * g1_ref[...]
    h = y * (1.0 + mod_ref[0, 1:2, :]) + mod_ref[0, 0:1, :]
    hb = h.astype(BF16)
    seg = seg_ref[...]

    def proj(c0, c1):
        return _dot(hb, w_ref[:, c0:c1])

    lane = lax.broadcasted_iota(jnp.int32, (ts, HEAD_DIM), 1)
    ones192 = jnp.ones((ts, V_LANES - HEAD_DIM), F32)
    q_tail = jnp.where(lane == SHIFT_LANE - HEAD_DIM, 1.0, 0.0).astype(F32)

    heads_per_tile = MXU_N // HEAD_DIM
    row = pl.program_id(1) * ts + lax.broadcasted_iota(jnp.int32, (ts, HEAD_DIM), 0)
    onehot = jnp.where(lane == row // SEL_BLOCK, 1.0, 0.0).astype(F32) + tail_ref[0:1, :]
    win_tail = jnp.broadcast_to(tail_ref[1:2, :], (ts, HEAD_DIM))

    def do_q(c):
        v = proj(C_Q + c * MXU_N, C_Q + (c + 1) * MXU_N)
        vn = v * lax.rsqrt(_seg_mean_sq(v, seg) + EPS) * gq_ref[...] * (HEAD_DIM ** -0.5 * LOG2E)
        for hh in range(heads_per_tile):
            qh = jnp.concatenate([vn[:, hh * HEAD_DIM:(hh + 1) * HEAD_DIM], q_tail], axis=-1)
            q_ref[0, c * heads_per_tile + hh] = qh.astype(BF16)

    def do_kvc():
        v = proj(C_KVC, C_KK)
        kvc_ref[0, 0] = v[:, :D_KV]
        kvc_ref[0, 1] = v[:, D_KV:]

    def do_kk():
        v = proj(C_KK, C_VV)
        vn = v * lax.rsqrt(_seg_mean_sq(v, seg) + EPS) * gkk_ref[...]
        for g in range(N_GROUPS):
            ks_ref[0, g] = jnp.concatenate(
                [vn[:, g * HEAD_DIM:(g + 1) * HEAD_DIM], onehot], axis=-1).astype(BF16)
            kw_ref[0, g] = jnp.concatenate(
                [vn[:, D_KV + g * HEAD_DIM:D_KV + (g + 1) * HEAD_DIM], win_tail], axis=-1).astype(BF16)

    def do_vv():
        v = proj(C_VV, C_GATE)
        for g in range(N_GROUPS):
            vs_ref[0, g] = jnp.concatenate(
                [v[:, g * HEAD_DIM:(g + 1) * HEAD_DIM], ones192], axis=-1).astype(BF16)
            vw_ref[0, g] = jnp.concatenate(
                [v[:, D_KV + g * HEAD_DIM:D_KV + (g + 1) * HEAD_DIM], ones192], axis=-1).astype(BF16)

    def do_gate():
        v = _sigmoid(proj(C_GATE, C_BG))
        for g in range(N_GROUPS):
            gate_ref[0, g] = v[:, g * LANES:(g + 1) * LANES]

    def do_bg(c):
        cols = slice(c * MXU_N, (c + 1) * MXU_N)
        bg_ref[0, :, cols] = proj(C_BG + c * MXU_N, C_BG + (c + 1) * MXU_N).astype(BF16)

    def do_u(c):
        cols = slice(c * MXU_N, (c + 1) * MXU_N)
        u_ref[0, :, cols] = (proj(C_CG + c * MXU_N, C_CG + (c + 1) * MXU_N)
                             * proj(C_XT + c * MXU_N, C_XT + (c + 1) * MXU_N)).astype(BF16)

    do_q(0)
    do_bg(0)
    do_q(1)
    do_bg(1)
    do_kvc()
    do_u(0)
    do_kk()
    do_u(1)
    do_gate()
    do_vv()


def _inproj(x, mod3, g1, w_p, seg, gq4, gkk4, tails):
    bsz, seq, d = x.shape
    ts = TS_IN
    const2 = lambda b, i: (0, 0)
    k_shape = jax.ShapeDtypeStruct((bsz, N_GROUPS, seq, LANES), BF16)
    k_spec = pl.BlockSpec((1, N_GROUPS, ts, LANES), lambda b, i: (b, 0, i, 0))
    v_shape = jax.ShapeDtypeStruct((bsz, N_GROUPS, seq, V_LANES), BF16)
    v_spec = pl.BlockSpec((1, N_GROUPS, ts, V_LANES), lambda b, i: (b, 0, i, 0))
    return pl.pallas_call(
        _inproj_kernel,
        grid=(bsz, seq // ts),
        in_specs=[pl.BlockSpec((1, ts, d), lambda b, i: (b, i, 0)),
                  pl.BlockSpec((1, 6, d), lambda b, i: (b, 0, 0)),
                  pl.BlockSpec((1, d), const2),
                  pl.BlockSpec(w_p.shape, const2, pipeline_mode=pl.Buffered(1)),
                  pl.BlockSpec(seg.shape, const2),
                  pl.BlockSpec(gq4.shape, const2),
                  pl.BlockSpec(gkk4.shape, const2),
                  pl.BlockSpec(tails.shape, const2)],
        out_specs=[pl.BlockSpec((1, N_HEADS, ts, LANES), lambda b, i: (b, 0, i, 0)),
                   pl.BlockSpec((1, 2, ts, D_KV), lambda b, i: (b, 0, i, 0)),
                   k_spec, v_spec, k_spec, v_spec,
                   pl.BlockSpec((1, N_GROUPS, ts, LANES), lambda b, i: (b, 0, i, 0)),
                   pl.BlockSpec((1, ts, D_CONV), lambda b, i: (b, i, 0)),
                   pl.BlockSpec((1, ts, D_CONV), lambda b, i: (b, i, 0))],
        out_shape=[jax.ShapeDtypeStruct((bsz, N_HEADS, seq, LANES), BF16),
                   jax.ShapeDtypeStruct((bsz, 2, seq, D_KV), F32),
                   k_shape, v_shape, k_shape, v_shape,
                   jax.ShapeDtypeStruct((bsz, N_GROUPS, seq, LANES), F32),
                   jax.ShapeDtypeStruct((bsz, seq, D_CONV), BF16),
                   jax.ShapeDtypeStruct((bsz, seq, D_CONV), BF16)],
        compiler_params=pltpu.CompilerParams(
            dimension_semantics=("arbitrary", "arbitrary"), vmem_limit_bytes=VMEM_LIMIT),
        name="inproj",
    )(x, mod3, g1, w_p, seg, gq4, gkk4, tails)


def _compress_kernel(kvc_ref, w1k_ref, w1v_ref, w2k_ref, w2v_ref, pk_ref, pv_ref, seg_ref,
                     gk_ref, kct_ref, vc_ref):
    n_rows = kvc_ref.shape[2] // CMP_STRIDE

    def branch(which, w1_ref, w2_ref, pos_ref):
        p1 = jnp.zeros((n_rows, N_GROUPS * CMP_HIDDEN), F32)
        p2 = jnp.zeros((n_rows, N_GROUPS * CMP_HIDDEN), F32)
        for r in range(CMP_STRIDE):
            rows = kvc_ref[0, which, pl.ds(r, n_rows, stride=CMP_STRIDE), :]
            a1 = (rows + pos_ref[r:r + 1, :]).astype(BF16)
            a2 = (rows + pos_ref[CMP_STRIDE + r:CMP_STRIDE + r + 1, :]).astype(BF16)
            p1 = p1 + _dot(a1, w1_ref[r])
            p2 = p2 + _dot(a2, w1_ref[CMP_STRIDE + r])
        hid = p1 + pltpu.roll(p2, n_rows - 1, 0)
        return _dot(_silu(hid).astype(BF16), w2_ref[...])

    kc = branch(0, w1k_ref, w2k_ref, pk_ref)
    kc = kc * lax.rsqrt(_seg_mean_sq(kc, seg_ref[...]) + EPS) * gk_ref[...]
    kct = kc.T
    zpad = jnp.zeros((HEAD_DIM, n_rows), F32)
    vcv = branch(1, w1v_ref, w2v_ref, pv_ref)
    zlane = jnp.zeros((n_rows, HEAD_DIM), F32)
    for g in range(N_GROUPS):
        kct_ref[0, g] = jnp.concatenate(
            [kct[g * HEAD_DIM:(g + 1) * HEAD_DIM, :], zpad], axis=0).astype(BF16)
        vc_ref[0, g] = jnp.concatenate(
            [vcv[:, g * HEAD_DIM:(g + 1) * HEAD_DIM], zlane], axis=-1).astype(BF16)


def _compress(kvc, w1k, w1v, w2k, w2v, pk2, pv2, seg, gk2):
    bsz, _, seq, _ = kvc.shape
    c3 = lambda b: (0, 0, 0)
    c2 = lambda b: (0, 0)
    out_shape = jax.ShapeDtypeStruct((bsz, N_GROUPS, N_CMP_PAD, LANES), BF16)
    out_spec = pl.BlockSpec((1, N_GROUPS, N_CMP_PAD, LANES), lambda b: (b, 0, 0, 0))
    return pl.pallas_call(
        _compress_kernel,
        grid=(bsz,),
        in_specs=[pl.BlockSpec((1, 2, seq, D_KV), lambda b: (b, 0, 0, 0)),
                  pl.BlockSpec(w1k.shape, c3), pl.BlockSpec(w1v.shape, c3),
                  pl.BlockSpec(w2k.shape, c2), pl.BlockSpec(w2v.shape, c2),
                  pl.BlockSpec(pk2.shape, c2), pl.BlockSpec(pv2.shape, c2),
                  pl.BlockSpec(seg.shape, c2), pl.BlockSpec(gk2.shape, c2)],
        out_specs=[out_spec, out_spec],
        out_shape=[out_shape, out_shape],
        compiler_params=pltpu.CompilerParams(
            dimension_semantics=("arbitrary",), vmem_limit_bytes=VMEM_LIMIT),
        name="compress",
    )(kvc, w1k, w1v, w2k, w2v, pk2, pv2, seg, gk2)


def _logits(qa, k_rows, bias):
    s = lax.dot_general(qa, k_rows, _NT, preferred_element_type=F32)
    return s if bias is None else s + bias


def _shifted_pv(qa, k_rows, v_rows, bias, keep=None):
    if keep is not None:
        v_rows = v_rows * keep.astype(BF16)
    return _dot(jnp.exp2(_logits(qa, k_rows, bias)).astype(BF16), v_rows)


def _online_update(qa, k_tile, v_tile, bias, acc_ref, m_ref):
    s = _logits(qa, k_tile, bias)
    m_prev = m_ref[...]
    m_new = jnp.maximum(m_prev, jnp.max(s, axis=-1, keepdims=True))
    alpha = jnp.exp2(m_prev - m_new)
    p = jnp.exp2(s - jnp.concatenate([m_new] * (TQ // LANES), axis=-1))
    acc_ref[...] = (acc_ref[...] * jnp.concatenate([alpha] * (V_LANES // LANES), axis=-1)
                    + _dot(p.astype(BF16), v_tile))
    m_ref[...] = m_new


def _selection_bias(pc, qi, ovlt, n_sel):
    psum = pc[0:TQ] + pc[TQ:2 * TQ] + pc[2 * TQ:3 * TQ] + pc[3 * TQ:4 * TQ]
    hi = psum.astype(BF16)
    lo = (psum - hi.astype(F32)).astype(BF16)
    pslc_t = (lax.dot_general(ovlt, hi, _NT, preferred_element_type=F32)
              + lax.dot_general(ovlt, lo, _NT, preferred_element_type=F32))
    top_k = min(SEL_TOPK, n_sel)
    sub = 8
    j = lax.broadcasted_iota(jnp.int32, (n_sel, TQ), 0)
    t = qi * TQ + lax.broadcasted_iota(jnp.int32, (n_sel, TQ), 1)
    dist = jnp.right_shift(t, int(math.log2(SEL_BLOCK))) - j
    score = jnp.where(dist < N_LOCAL_FORCED, FORCED_SCORE, pslc_t[SEL_LANE0:SEL_LANE0 + n_sel, :])
    score = jnp.where(j == 0, FORCED_SCORE, score)
    score = jnp.where(dist >= 0, score, NEG_INF)
    groups = [score[a * sub:(a + 1) * sub] for a in range(n_sel // sub)]
    cnts = [jnp.zeros((sub, TQ), F32) for _ in groups]
    j_sub = lax.broadcasted_iota(jnp.int32, (sub, TQ), 0)
    for i in range(n_sel):
        row = jnp.broadcast_to(score[i:i + 1, :], (sub, TQ))
        for a, grp in enumerate(groups):
            if a * sub > i:
                beats = jnp.where(row >= grp, 1.0, 0.0)
            elif a * sub + sub - 1 <= i:
                beats = jnp.where(row > grp, 1.0, 0.0)
            else:
                beats = jnp.where(j_sub + a * sub > i, jnp.where(row >= grp, 1.0, 0.0),
                                  jnp.where(row > grp, 1.0, 0.0))
            cnts[a] = cnts[a] + beats
    sel_t = jnp.where(jnp.concatenate(cnts, axis=0) < top_k, 0.0, NEG_INF)
    return jnp.concatenate(
        [jnp.zeros((SEL_LANE0, TQ), F32), sel_t,
         jnp.zeros((LANES - SEL_LANE0 - n_sel, TQ), F32)], axis=0).T


def _attn_step(shifted, q_ref, kct_ref, vc_ref, ks_ref, vs_ref, kw_ref, vw_ref, gate_ref,
               cb_ref, b0_ref, b1_ref, w2_ref, ovlt_ref, o_ref, qa_ref, acc_s, m_s, acc_w, m_w):
    qi = pl.program_id(1)
    groups = range(N_GROUPS)
    q = [q_ref[0, g * HPG:(g + 1) * HPG].reshape(ROWS, LANES) for g in groups]

    def key_rows(ref, g, kt, n_tiles=1):
        return ref[0, g, pl.ds(pl.multiple_of(kt * TQ, TQ), n_tiles * TQ), :]

    def compressed_and_selection(which=groups):
        o_c = []
        for g in which:
            s = _dot(q[g], kct_ref[0, g]) + cb_ref[g, 0]
            m = jnp.max(s, axis=-1, keepdims=True)
            e = jnp.exp2(s - m)
            l = jnp.sum(e, axis=-1, keepdims=True)
            pc = e * jnp.where(m > 0.5 * NEG_INF, 1.0 / l, 0.0)
            o_c.append(_dot(pc.astype(BF16), vc_ref[0, g]))
            selbias = _selection_bias(pc, qi, ovlt_ref[...], ks_ref.shape[2] // SEL_BLOCK)
            for h in range(HPG):
                qa_ref[g, h * TQ:(h + 1) * TQ, :] = (
                    q_ref[0, g * HPG + h].astype(F32) + selbias).astype(BF16)
        return o_c

    kt1 = jnp.maximum(qi - 1, 0)
    kt2 = jnp.maximum(qi - 2, 0)
    if shifted:
        has1 = jnp.where(qi >= 1, 1.0, 0.0)
        has2 = jnp.where(qi >= 2, 1.0, 0.0)
        a_w, o_c = [], []
        for g in groups:
            a_w.append(
                _shifted_pv(q[g], key_rows(kw_ref, g, kt2), key_rows(vw_ref, g, kt2), w2_ref[...], has2)
                + _shifted_pv(q[g], key_rows(kw_ref, g, kt1), key_rows(vw_ref, g, kt1), b1_ref[g], has1)
                + _shifted_pv(q[g], key_rows(kw_ref, g, qi), key_rows(vw_ref, g, qi), b0_ref[g]))
            o_c += compressed_and_selection([g])
        acc_s[...] = jnp.zeros(acc_s.shape, F32)
        n_plain = jnp.maximum(qi - 1, 0)

        def pair_body(it, carry):
            pv = [None] * N_GROUPS
            for half in range(2):
                for g in groups:
                    t = _shifted_pv(qa_ref[g], key_rows(ks_ref, g, 2 * it + half),
                                    key_rows(vs_ref, g, 2 * it + half), None)
                    pv[g] = t if pv[g] is None else pv[g] + t
            for g in groups:
                acc_s[g] += pv[g]
            return carry

        lax.fori_loop(0, n_plain // 2, pair_body, 0)
        odd = jnp.where(n_plain % 2 == 1, 1.0, 0.0)
        a_s = [acc_s[g]
               + _shifted_pv(qa_ref[g], key_rows(ks_ref, g, kt2), key_rows(vs_ref, g, kt2), None, odd)
               + _shifted_pv(qa_ref[g], key_rows(ks_ref, g, kt1), key_rows(vs_ref, g, kt1), b1_ref[g], has1)
               + _shifted_pv(qa_ref[g], key_rows(ks_ref, g, qi), key_rows(vs_ref, g, qi), b0_ref[g])
               for g in groups]
    else:
        o_c = compressed_and_selection()
        a_s, a_w = [], []
        for g in groups:
            qa = qa_ref[g]
            sel = (acc_s.at[g], m_s.at[g])
            win = (acc_w.at[g], m_w.at[g])
            for acc, m_ref in (sel, win):
                acc[...] = jnp.zeros(acc.shape, F32)
                m_ref[...] = jnp.full(m_ref.shape, NEG_INF, F32)

            def sel_body(kt, carry, g=g, qa=qa, sel=sel):
                _online_update(qa, key_rows(ks_ref, g, kt), key_rows(vs_ref, g, kt), None, *sel)
                return carry

            lax.fori_loop(0, qi - 1, sel_body, 0)

            @pl.when(qi >= 2)
            def _(g=g, win=win):
                _online_update(q[g], key_rows(kw_ref, g, kt2), key_rows(vw_ref, g, kt2), w2_ref[...], *win)

            @pl.when(qi >= 1)
            def _(g=g, qa=qa, sel=sel, win=win):
                _online_update(qa, key_rows(ks_ref, g, kt1), key_rows(vs_ref, g, kt1), b1_ref[g], *sel)
                _online_update(q[g], key_rows(kw_ref, g, kt1), key_rows(vw_ref, g, kt1), b1_ref[g], *win)

            _online_update(qa, key_rows(ks_ref, g, qi), key_rows(vs_ref, g, qi), b0_ref[g], *sel)
            _online_update(q[g], key_rows(kw_ref, g, qi), key_rows(vw_ref, g, qi), b0_ref[g], *win)
            a_s.append(acc_s[g])
            a_w.append(acc_w[g])

    outs = []
    for g in groups:
        o_s = a_s[g][:, :LANES] / a_s[g][:, LANES:]
        o_w = a_w[g][:, :LANES] / a_w[g][:, LANES:]
        gates = gate_ref[0, g]
        for h in range(HPG):
            rows = slice(h * TQ, (h + 1) * TQ)
            g_c = gates[:, h * N_BRANCH + 0:h * N_BRANCH + 1]
            g_s = gates[:, h * N_BRANCH + 1:h * N_BRANCH + 2]
            g_w = gates[:, h * N_BRANCH + 2:h * N_BRANCH + 3]
            o_h = g_c * o_c[g][rows] + g_s * o_s[rows] + g_w * o_w[rows]
            outs.append(o_h[:, :HEAD_DIM])
    o_ref[0] = jnp.concatenate(outs, axis=-1)


def _attn_kernel(flag_ref, *refs):
    use_shift = flag_ref[0] == 1

    @pl.when(use_shift)
    def _():
        _attn_step(True, *refs)

    @pl.when(jnp.logical_not(use_shift))
    def _():
        _attn_step(False, *refs)


def _attn(flag, q, kct, vc, ks, vs, kw, vw, gate, cb, b0, b1, w2, ovlt):
    bsz, _, seq, _ = q.shape
    nq = seq // TQ
    per_batch = lambda b, i: (b, 0, 0, 0)
    k_spec = pl.BlockSpec((1, N_GROUPS, seq, LANES), per_batch)
    v_spec = pl.BlockSpec((1, N_GROUPS, seq, V_LANES), per_batch)
    cmp_spec = pl.BlockSpec((1, N_GROUPS, N_CMP_PAD, LANES), per_batch)
    const = lambda shape: pl.BlockSpec(shape, lambda b, i: (0,) * len(shape),
                                       pipeline_mode=pl.Buffered(1))
    return pl.pallas_call(
        _attn_kernel,
        grid=(bsz, nq),
        in_specs=[pl.BlockSpec(memory_space=pltpu.SMEM),
                  pl.BlockSpec((1, N_HEADS, TQ, LANES), lambda b, i: (b, 0, i, 0)),
                  cmp_spec, cmp_spec, k_spec, v_spec, k_spec, v_spec,
                  pl.BlockSpec((1, N_GROUPS, TQ, LANES), lambda b, i: (b, 0, i, 0)),
                  pl.BlockSpec((N_GROUPS, 1, ROWS, N_CMP_PAD), lambda b, i: (0, i, 0, 0)),
                  const(b0.shape), const(b1.shape), const(w2.shape), const(ovlt.shape)],
        out_specs=pl.BlockSpec((1, TQ, D_ATTN), lambda b, i: (b, i, 0)),
        out_shape=jax.ShapeDtypeStruct((bsz, seq, D_ATTN), F32),
        scratch_shapes=[pltpu.VMEM((N_GROUPS, ROWS, LANES), BF16),
                        pltpu.VMEM((N_GROUPS, ROWS, V_LANES), F32),
                        pltpu.VMEM((N_GROUPS, ROWS, LANES), F32),
                        pltpu.VMEM((N_GROUPS, ROWS, V_LANES), F32),
                        pltpu.VMEM((N_GROUPS, ROWS, LANES), F32)],
        compiler_params=pltpu.CompilerParams(
            dimension_semantics=("arbitrary", "arbitrary"),
            vmem_limit_bytes=VMEM_LIMIT),
        name="attn",
    )(flag, q, kct, vc, ks, vs, kw, vw, gate, cb, b0, b1, w2, ovlt)


TS_OUT = 512
HALO = 16


def _outproj_ffn_kernel(ya_ref, bg_ref, u_ref, uh_ref, cw_ref, ga_ref, gc_ref, w_ref, x_ref, mod_ref,
                        g2_ref, w1_ref, w3_ref, w2_ref, o_ref):
    u = u_ref[0].astype(F32)
    halo = uh_ref[0].astype(F32) * jnp.where(pl.program_id(1) > 0, 1.0, 0.0)
    h1 = halo[HALO - 1:HALO, :]
    h2 = halo[HALO - 2:HALO - 1, :]
    row = lax.broadcasted_iota(jnp.int32, u.shape, 0)
    u1 = jnp.where(row == 0, h1, pltpu.roll(u, 1, 0))
    u2 = jnp.where(row == 0, h2, jnp.where(row == 1, h1, pltpu.roll(u, 2, 0)))
    conv = u2 * cw_ref[0:1, :] + u1 * cw_ref[1:2, :] + u * cw_ref[2:3, :]
    yc = bg_ref[0].astype(F32) * conv
    yc = yc * lax.rsqrt(jnp.mean(yc * yc, axis=-1, keepdims=True) + EPS) * gc_ref[...]
    ya = ya_ref[0]
    ya = ya * lax.rsqrt(jnp.mean(ya * ya, axis=-1, keepdims=True) + EPS) * ga_ref[...]
    proj = _dot(ya.astype(BF16), w_ref[0:D_ATTN, :]) + _dot(yc.astype(BF16), w_ref[D_ATTN:, :])
    x = x_ref[0] + mod_ref[0, 2:3, :] * proj

    ms = jnp.mean(x * x, axis=-1, keepdims=True)
    h = x * lax.rsqrt(ms + EPS) * g2_ref[...] * (1.0 + mod_ref[0, 4:5, :]) + mod_ref[0, 3:4, :]
    hb = h.astype(BF16)
    a = _dot(hb, w1_ref[...])
    b = _dot(hb, w3_ref[...])
    act = (_silu(a) * b).astype(BF16)
    o_ref[0] = x + mod_ref[0, 5:6, :] * _dot(act, w2_ref[...])


def _outproj_ffn(ya, bg, u, conv_w, ga, gc, w_out, x, mod3, g2, w1, w3, w2):
    bsz, seq, d = x.shape
    ts = TS_OUT
    c2 = lambda b, i: (0, 0)
    row_spec = lambda width: pl.BlockSpec((1, ts, width), lambda b, i: (b, i, 0))
    wspec = lambda w: pl.BlockSpec(w.shape, c2, pipeline_mode=pl.Buffered(1))
    return pl.pallas_call(
        _outproj_ffn_kernel,
        grid=(bsz, seq // ts),
        in_specs=[row_spec(D_ATTN), row_spec(D_CONV), row_spec(D_CONV),
                  pl.BlockSpec((1, HALO, D_CONV),
                               lambda b, i: (b, jnp.maximum(i * (ts // HALO) - 1, 0), 0)),
                  pl.BlockSpec(conv_w.shape, c2),
                  pl.BlockSpec((1, D_ATTN), c2), pl.BlockSpec((1, D_CONV), c2),
                  wspec(w_out),
                  row_spec(d),
                  pl.BlockSpec((1, 6, d), lambda b, i: (b, 0, 0)),
                  pl.BlockSpec((1, d), c2),
                  wspec(w1), wspec(w3), wspec(w2)],
        out_specs=row_spec(d),
        out_shape=jax.ShapeDtypeStruct((bsz, seq, d), F32),
        compiler_params=pltpu.CompilerParams(
            dimension_semantics=("arbitrary", "arbitrary"), vmem_limit_bytes=VMEM_LIMIT),
        name="outproj_ffn",
    )(ya, bg, u, u, conv_w, ga, gc, w_out, x, mod3, g2, w1, w3, w2)


def _t5_bucket_np(rel):
    n = np.maximum(rel, 0)
    max_exact = N_BUCKETS // 2
    nf = np.maximum(n, 1).astype(np.float32)
    large = max_exact + (np.log(nf / max_exact) / math.log(MAX_DISTANCE / max_exact)
                         * (N_BUCKETS - max_exact)).astype(np.int32)
    return np.where(n < max_exact, n, np.minimum(large, N_BUCKETS - 1)).astype(np.int32)


def _bias_of_rel(tab_rel, rel):
    idx = jnp.asarray(_t5_bucket_np(rel))
    tab_b = tab_rel.T.reshape((N_HEADS, N_BUCKETS) + (1,) * idx.ndim)
    vals = jnp.zeros((N_HEADS,) + idx.shape, F32)
    for k in range(N_BUCKETS):
        vals = jnp.where(idx == k, tab_b[:, k], vals)
    return jnp.where(jnp.asarray(rel >= 0), vals, NEG_INF)


def _toeplitz_offsets(length, n_cols):
    k = np.arange(length)
    return np.where(k < n_cols, k, k - length)


def _toeplitz(vec, n_rows, n_cols):
    length = vec.shape[-1]
    assert length >= n_rows + n_cols - 1
    m = jnp.tile(vec, (1,) * (vec.ndim - 1) + (n_rows,))[..., :n_rows * (length - 1)]
    return m.reshape(vec.shape[:-1] + (n_rows, length - 1))[..., :n_cols]


def _stack_heads(vals):
    return vals.reshape(N_GROUPS, HPG * vals.shape[-2], vals.shape[-1])


def _bias_tables(rel_bias_table, seq):
    tab_rel = (rel_bias_table - rel_bias_table[N_BUCKETS - 1:N_BUCKETS, :]) * LOG2E
    d = _toeplitz_offsets(2 * TQ - 1, TQ)
    b0 = _stack_heads(_toeplitz(_bias_of_rel(tab_rel, -d), TQ, TQ))
    b1 = _stack_heads(_toeplitz(_bias_of_rel(tab_rel, TQ - d), TQ, TQ))
    i = np.arange(TQ)[:, None]
    jj = np.arange(TQ)[None, :]
    w2 = jnp.asarray(np.tile(np.where(jj > i, 0.0, NEG_INF).astype(np.float32), (HPG, 1)))
    nq = seq // TQ
    n_cmp = (seq - CMP_BLOCK) // CMP_STRIDE + 1
    shift = TQ // CMP_STRIDE
    assert nq * shift <= N_CMP_PAD and n_cmp * CMP_STRIDE + CMP_BLOCK - 1 > seq
    off = N_CMP_PAD * CMP_STRIDE - (CMP_BLOCK - 1)
    d = _toeplitz_offsets(shift + 2 * N_CMP_PAD - 1, 2 * N_CMP_PAD)[None, :]
    r = np.arange(CMP_STRIDE)[:, None]
    pat = _toeplitz(_bias_of_rel(tab_rel, r + off - CMP_STRIDE * d),
                    shift, 2 * N_CMP_PAD)
    pat = _stack_heads(jnp.swapaxes(pat, 1, 2).reshape(N_HEADS, TQ, 2 * N_CMP_PAD))
    cb = jnp.stack([pat[:, :, N_CMP_PAD - qi * shift:2 * N_CMP_PAD - qi * shift]
                    for qi in range(nq)], axis=1)
    n_sel = seq // SEL_BLOCK
    cs = np.arange(n_cmp) * CMP_STRIDE
    ce = cs + CMP_BLOCK - 1
    ss = np.arange(n_sel) * SEL_BLOCK
    ov = np.clip(np.minimum(ce[:, None], ss[None, :] + SEL_BLOCK - 1)
                 - np.maximum(cs[:, None], ss[None, :]) + 1, 0, None) / CMP_STRIDE
    ovlt = np.zeros((LANES, N_CMP_PAD), np.float32)
    ovlt[SEL_LANE0:SEL_LANE0 + n_sel, :n_cmp] = ov.T
    return tab_rel, cb, b0, b1, w2, jnp.asarray(ovlt, BF16)


def _blockdiag2(w):
    z = jnp.zeros_like(w)
    return jnp.concatenate([jnp.concatenate([w, z], axis=-1),
                            jnp.concatenate([z, w], axis=-1)], axis=-2)


def kernel(x, c, w_ada, b_ada, norm1_gain, w_in, q_gain, k_cmp_gain, k_sel_gain, k_win_gain,
           cmp_pos_k, cmp_pos_v, w_ck1, w_ck2, w_cv1, w_cv2, rel_bias_table, conv_w,
           attn_out_gain, conv_out_gain, w_out, norm2_gain, w_ff1, w_ff3, w_ff2):
    bsz, seq, d = x.shape
    assert d == D_MODEL and seq % TQ == 0 and SEL_LANE0 + seq // SEL_BLOCK <= SHIFT_LANE
    assert (seq - CMP_BLOCK) // CMP_STRIDE + 1 <= N_CMP_PAD and seq // CMP_STRIDE == N_CMP_PAD

    seg_of = lambda n: jnp.asarray(np.kron(np.eye(n), np.ones((HEAD_DIM, HEAD_DIM))) / HEAD_DIM, BF16)
    seg, seg4 = seg_of(LANES // HEAD_DIM), seg_of(MXU_N // HEAD_DIM)
    tab_rel, cb, b0, b1, w2m, ovlt = _bias_tables(rel_bias_table, seq)
    bias_max = jnp.max(jnp.abs(tab_rel))
    two = lambda g: jnp.tile(g.reshape(1, HEAD_DIM), (1, 2))
    n_gate = N_BRANCH * HPG
    gate_pad = jnp.zeros((d, LANES - n_gate), F32)

    for layer in range(w_in.shape[0]):
        wi = w_in[layer]
        o_g = D_ATTN + 6 * D_KV
        kv = lambda n: wi[:, D_ATTN + n * D_KV:D_ATTN + (n + 1) * D_KV]
        w_p = jnp.concatenate(
            [wi[:, :D_ATTN], kv(0), kv(1), kv(2), kv(4), kv(3), kv(5),
             wi[:, o_g:o_g + n_gate], gate_pad,
             wi[:, o_g + n_gate:o_g + 2 * n_gate], gate_pad, wi[:, o_g + 2 * n_gate:]],
            axis=-1).astype(BF16)
        w1k = _blockdiag2(w_ck1[layer].reshape(CMP_BLOCK, HEAD_DIM, CMP_HIDDEN)).astype(BF16)
        w1v = _blockdiag2(w_cv1[layer].reshape(CMP_BLOCK, HEAD_DIM, CMP_HIDDEN)).astype(BF16)
        w2k = _blockdiag2(w_ck2[layer]).astype(BF16)
        w2v = _blockdiag2(w_cv2[layer]).astype(BF16)
        pk2 = jnp.tile(cmp_pos_k[layer], (1, 2))
        pv2 = jnp.tile(cmp_pos_v[layer], (1, 2))

        gq_max = jnp.max(jnp.abs(q_gain[layer]))
        bound = lambda gk: (HEAD_DIM ** 0.5 * LOG2E * 1.01) * gq_max * jnp.max(jnp.abs(gk)) + bias_max
        c_sel, c_win = bound(k_sel_gain[layer]), bound(k_win_gain[layer])
        flag = (2.0 * jnp.maximum(c_sel, c_win) <= MAX_SHIFTED_RANGE).astype(jnp.int32).reshape(1)
        tails = jnp.zeros((2, HEAD_DIM), F32).at[:, SHIFT_LANE - HEAD_DIM].set(
            -jnp.stack([c_sel, c_win]))

        mod3 = _adaln(c, w_ada[layer], b_ada[layer]).reshape(bsz, 6, d)
        q, kvc, ks, vs, kw, vw, gate, bg, u = _inproj(
            x, mod3, norm1_gain[layer].reshape(1, d), w_p, seg4,
            jnp.tile(q_gain[layer].reshape(1, HEAD_DIM), (1, MXU_N // HEAD_DIM)),
            jnp.concatenate([two(k_sel_gain[layer]), two(k_win_gain[layer])], axis=-1), tails)
        kct, vc = _compress(kvc, w1k, w1v, w2k, w2v, pk2, pv2, seg, two(k_cmp_gain[layer]))
        y_attn = _attn(flag, q, kct, vc, ks, vs, kw, vw, gate, cb, b0, b1, w2m, ovlt)
        x = _outproj_ffn(y_attn, bg, u, conv_w[layer], attn_out_gain[layer].reshape(1, D_ATTN),
                         conv_out_gain[layer].reshape(1, D_CONV), w_out[layer].astype(BF16), x, mod3,
                         norm2_gain[layer].reshape(1, d), w_ff1[layer].astype(BF16),
                         w_ff3[layer].astype(BF16), w_ff2[layer].astype(BF16))
    return x
```

```python
import math

import numpy as np
import jax
import jax.numpy as jnp
from jax import lax
from jax.experimental import pallas as pl
from jax.experimental.pallas import tpu as pltpu

F32 = jnp.float32
BF16 = jnp.bfloat16

D_MODEL = 1024
HEAD_DIM = 64
N_HEADS = 8
N_GROUPS = 2
HPG = N_HEADS // N_GROUPS
D_ATTN = N_HEADS * HEAD_DIM
D_KV = N_GROUPS * HEAD_DIM
N_BRANCH = 3
D_CONV = D_MODEL - D_ATTN
CONV_WIDTH = 3
CMP_BLOCK = 32
CMP_STRIDE = 16
CMP_HIDDEN = 256
SEL_BLOCK = 64
SEL_TOPK = 16
N_LOCAL_FORCED = 2
WINDOW = 512
N_BUCKETS = 32
MAX_DISTANCE = 128
EPS = 1e-6
NEG_INF = -1e30
FORCED_SCORE = 1e6

LANES = 128
TQ = 256
ROWS = HPG * TQ
N_CMP_PAD = 128
SEL_LANE0 = 64
SHIFT_LANE = 96
V_LANES = 256
LOG2E = math.log2(math.e)
MAX_SHIFTED_RANGE = 100.0
VMEM_LIMIT = 56 * 1024 * 1024

_NT = (((1,), (1,)), ((), ()))


def _dot(a, b):
    return jnp.dot(a, b, preferred_element_type=F32)


def _silu(v):
    return v * (1.0 / (1.0 + jnp.exp(-v)))


def _sigmoid(v):
    return 1.0 / (1.0 + jnp.exp(-v))


def _seg_mean_sq(v, seg):
    sq = v * v
    hi = sq.astype(BF16)
    lo = (sq - hi.astype(F32)).astype(BF16)
    return _dot(hi, seg) + _dot(lo, seg)


def _adaln_kernel(c_ref, w_ref, b_ref, o_ref):
    sc = _silu(c_ref[...]).astype(BF16)
    o_ref[...] = _dot(sc, w_ref[...].astype(BF16)) + b_ref[...]


def _adaln(c, w_ada, b_ada):
    bsz, d = c.shape
    n = w_ada.shape[1]
    tn = 1536
    return pl.pallas_call(
        _adaln_kernel,
        grid=(n // tn,),
        in_specs=[pl.BlockSpec((bsz, d), lambda j: (0, 0)),
                  pl.BlockSpec((d, tn), lambda j: (0, j)),
                  pl.BlockSpec((1, tn), lambda j: (0, j))],
        out_specs=pl.BlockSpec((bsz, tn), lambda j: (0, j)),
        out_shape=jax.ShapeDtypeStruct((bsz, n), F32),
        compiler_params=pltpu.CompilerParams(
            dimension_semantics=("arbitrary",), vmem_limit_bytes=VMEM_LIMIT),
        name="adaln",
    )(c, w_ada, b_ada.reshape(1, n))


C_Q, C_KVC, C_KK, C_VV, C_GATE, C_BG, C_CG, C_XT, C_END = (
    0, 512, 768, 1024, 1280, 1536, 2048, 2560, 3072)
MXU_N = 256
TS_IN = 1024


def _inproj_kernel(x_ref, mod_ref, g1_ref, w_ref, seg_ref, gq_ref, gkk_ref, tail_ref,
                   q_ref, kvc_ref, ks_ref, vs_ref, kw_ref, vw_ref, gate_ref, bg_ref, u_ref):
    ts = x_ref.shape[1]
    x = x_ref[0]
    ms = jnp.mean(x * x, axis=-1, keepdims=True)
    y = x * lax.rsqrt(ms + EPS) * g1_ref[...]
    h = y * (1.0 + mod_ref[0, 1:2, :]) + mod_ref[0, 0:1, :]
    hb = h.astype(BF16)
    seg = seg_ref[...]

    def proj(c0, c1):
        return _dot(hb, w_ref[:, c0:c1])

    lane = lax.broadcasted_iota(jnp.int32, (ts, HEAD_DIM), 1)
    ones192 = jnp.ones((ts, V_LANES - HEAD_DIM), F32)
    q_tail = jnp.where(lane == SHIFT_LANE - HEAD_DIM, 1.0, 0.0).astype(F32)

    heads_per_tile = MXU_N // HEAD_DIM
    row = pl.program_id(1) * ts + lax.broadcasted_iota(jnp.int32, (ts, HEAD_DIM), 0)
    onehot = jnp.where(lane == row // SEL_BLOCK, 1.0, 0.0).astype(F32) + tail_ref[0:1, :]
    win_tail = jnp.broadcast_to(tail_ref[1:2, :], (ts, HEAD_DIM))

    def do_q(c):
        v = proj(C_Q + c * MXU_N, C_Q + (c + 1) * MXU_N)
        vn = v * lax.rsqrt(_seg_mean_sq(v, seg) + EPS) * gq_ref[...] * (HEAD_DIM ** -0.5 * LOG2E)
        for hh in range(heads_per_tile):
            qh = jnp.concatenate([vn[:, hh * HEAD_DIM:(hh + 1) * HEAD_DIM], q_tail], axis=-1)
            q_ref[0, c * heads_per_tile + hh] = qh.astype(BF16)

    def do_kvc():
        v = proj(C_KVC, C_KK)
        kvc_ref[0, 0] = v[:, :D_KV]
        kvc_ref[0, 1] = v[:, D_KV:]

    def do_kk():
        v = proj(C_KK, C_VV)
        vn = v * lax.rsqrt(_seg_mean_sq(v, seg) + EPS) * gkk_ref[...]
        for g in range(N_GROUPS):
            ks_ref[0, g] = jnp.concatenate(
                [vn[:, g * HEAD_DIM:(g + 1) * HEAD_DIM], onehot], axis=-1).astype(BF16)
            kw_ref[0, g] = jnp.concatenate(
                [vn[:, D_KV + g * HEAD_DIM:D_KV + (g + 1) * HEAD_DIM], win_tail], axis=-1).astype(BF16)

    def do_vv():
        v = proj(C_VV, C_GATE)
        for g in range(N_GROUPS):
            vs_ref[0, g] = jnp.concatenate(
                [v[:, g * HEAD_DIM:(g + 1) * HEAD_DIM], ones192], axis=-1).astype(BF16)
            vw_ref[0, g] = jnp.concatenate(
                [v[:, D_KV + g * HEAD_DIM:D_KV + (g + 1) * HEAD_DIM], ones192], axis=-1).astype(BF16)

    def do_gate():
        v = _sigmoid(proj(C_GATE, C_BG))
        for g in range(N_GROUPS):
            gate_ref[0, g] = v[:, g * LANES:(g + 1) * LANES]

    def do_bg(c):
        cols = slice(c * MXU_N, (c + 1) * MXU_N)
        bg_ref[0, :, cols] = proj(C_BG + c * MXU_N, C_BG + (c + 1) * MXU_N).astype(BF16)

    def do_u(c):
        cols = slice(c * MXU_N, (c + 1) * MXU_N)
        u_ref[0, :, cols] = (proj(C_CG + c * MXU_N, C_CG + (c + 1) * MXU_N)
                             * proj(C_XT + c * MXU_N, C_XT + (c + 1) * MXU_N)).astype(BF16)

    do_q(0)
    do_bg(0)
    do_q(1)
    do_bg(1)
    do_kvc()
    do_u(0)
    do_kk()
    do_u(1)
    do_gate()
    do_vv()


def _inproj(x, mod3, g1, w_p, seg, gq4, gkk4, tails):
    bsz, seq, d = x.shape
    ts = TS_IN
    const2 = lambda b, i: (0, 0)
    k_shape = jax.ShapeDtypeStruct((bsz, N_GROUPS, seq, LANES), BF16)
    k_spec = pl.BlockSpec((1, N_GROUPS, ts, LANES), lambda b, i: (b, 0, i, 0))
    v_shape = jax.ShapeDtypeStruct((bsz, N_GROUPS, seq, V_LANES), BF16)
    v_spec = pl.BlockSpec((1, N_GROUPS, ts, V_LANES), lambda b, i: (b, 0, i, 0))
    return pl.pallas_call(
        _inproj_kernel,
        grid=(bsz, seq // ts),
        in_specs=[pl.BlockSpec((1, ts, d), lambda b, i: (b, i, 0)),
                  pl.BlockSpec((1, 6, d), lambda b, i: (b, 0, 0)),
                  pl.BlockSpec((1, d), const2),
                  pl.BlockSpec(w_p.shape, const2, pipeline_mode=pl.Buffered(1)),
                  pl.BlockSpec(seg.shape, const2),
                  pl.BlockSpec(gq4.shape, const2),
                  pl.BlockSpec(gkk4.shape, const2),
                  pl.BlockSpec(tails.shape, const2)],
        out_specs=[pl.BlockSpec((1, N_HEADS, ts, LANES), lambda b, i: (b, 0, i, 0)),
                   pl.BlockSpec((1, 2, ts, D_KV), lambda b, i: (b, 0, i, 0)),
                   k_spec, v_spec, k_spec, v_spec,
                   pl.BlockSpec((1, N_GROUPS, ts, LANES), lambda b, i: (b, 0, i, 0)),
                   pl.BlockSpec((1, ts, D_CONV), lambda b, i: (b, i, 0)),
                   pl.BlockSpec((1, ts, D_CONV), lambda b, i: (b, i, 0))],
        out_shape=[jax.ShapeDtypeStruct((bsz, N_HEADS, seq, LANES), BF16),
                   jax.ShapeDtypeStruct((bsz, 2, seq, D_KV), F32),
                   k_shape, v_shape, k_shape, v_shape,
                   jax.ShapeDtypeStruct((bsz, N_GROUPS, seq, LANES), F32),
                   jax.ShapeDtypeStruct((bsz, seq, D_CONV), BF16),
                   jax.ShapeDtypeStruct((bsz, seq, D_CONV), BF16)],
        compiler_params=pltpu.CompilerParams(
            dimension_semantics=("arbitrary", "arbitrary"), vmem_limit_bytes=VMEM_LIMIT),
        name="inproj",
    )(x, mod3, g1, w_p, seg, gq4, gkk4, tails)


def _compress_kernel(kvc_ref, w1k_ref, w1v_ref, w2k_ref, w2v_ref, pk_ref, pv_ref, gk_ref,
                     kct_ref, vc_ref):
    n_rows = kvc_ref.shape[2] // CMP_STRIDE
    half = CMP_STRIDE * HEAD_DIM
    zlane = jnp.zeros((n_rows, HEAD_DIM), F32)

    def branch(which, w1_ref, w2_ref, pos_ref):
        rows = [kvc_ref[0, which, pl.ds(r, n_rows, stride=CMP_STRIDE), :] for r in range(CMP_STRIDE)]
        outs = []
        for g in range(N_GROUPS):
            x = jnp.concatenate([rw[:, g * HEAD_DIM:(g + 1) * HEAD_DIM] for rw in rows], axis=-1)
            a1 = (x + pos_ref[0:1, :]).astype(BF16)
            a2 = (x + pos_ref[1:2, :]).astype(BF16)
            hid = _dot(a1, w1_ref[0:half, :]) + pltpu.roll(_dot(a2, w1_ref[half:, :]), n_rows - 1, 0)
            outs.append(_dot(_silu(hid).astype(BF16), w2_ref[...]))
        return outs

    kc = branch(0, w1k_ref, w2k_ref, pk_ref)
    vc = branch(1, w1v_ref, w2v_ref, pv_ref)
    for g in range(N_GROUPS):
        kn = kc[g] * lax.rsqrt(jnp.mean(kc[g] * kc[g], axis=-1, keepdims=True) + EPS) * gk_ref[...]
        kct_ref[0, g] = jnp.concatenate([kn, zlane], axis=-1).T.astype(BF16)
        vc_ref[0, g] = jnp.concatenate([vc[g], zlane], axis=-1).astype(BF16)


def _compress(kvc, w1k, w1v, w2k, w2v, pk2, pv2, gk):
    bsz, _, seq, _ = kvc.shape
    c2 = lambda b: (0, 0)
    out_shape = jax.ShapeDtypeStruct((bsz, N_GROUPS, N_CMP_PAD, LANES), BF16)
    out_spec = pl.BlockSpec((1, N_GROUPS, N_CMP_PAD, LANES), lambda b: (b, 0, 0, 0))
    return pl.pallas_call(
        _compress_kernel,
        grid=(bsz,),
        in_specs=[pl.BlockSpec((1, 2, seq, D_KV), lambda b: (b, 0, 0, 0)),
                  pl.BlockSpec(w1k.shape, c2), pl.BlockSpec(w1v.shape, c2),
                  pl.BlockSpec(w2k.shape, c2), pl.BlockSpec(w2v.shape, c2),
                  pl.BlockSpec(pk2.shape, c2), pl.BlockSpec(pv2.shape, c2),
                  pl.BlockSpec(gk.shape, c2)],
        out_specs=[out_spec, out_spec],
        out_shape=[out_shape, out_shape],
        compiler_params=pltpu.CompilerParams(
            dimension_semantics=("arbitrary",), vmem_limit_bytes=VMEM_LIMIT),
        name="compress",
    )(kvc, w1k, w1v, w2k, w2v, pk2, pv2, gk)


def _logits(qa, k_rows, bias):
    s = lax.dot_general(qa, k_rows, _NT, preferred_element_type=F32)
    return s if bias is None else s + bias


def _shifted_pv(qa, k_rows, v_rows, bias, keep=None):
    if keep is not None:
        v_rows = v_rows * keep.astype(BF16)
    return _dot(jnp.exp2(_logits(qa, k_rows, bias)).astype(BF16), v_rows)


def _online_update(qa, k_tile, v_tile, bias, acc_ref, m_ref):
    s = _logits(qa, k_tile, bias)
    m_prev = m_ref[...]
    m_new = jnp.maximum(m_prev, jnp.max(s, axis=-1, keepdims=True))
    alpha = jnp.exp2(m_prev - m_new)
    p = jnp.exp2(s - jnp.concatenate([m_new] * (TQ // LANES), axis=-1))
    acc_ref[...] = (acc_ref[...] * jnp.concatenate([alpha] * (V_LANES // LANES), axis=-1)
                    + _dot(p.astype(BF16), v_tile))
    m_ref[...] = m_new


def _selection_bias(pc, qi, ovlt, n_sel):
    psum = pc[0:TQ] + pc[TQ:2 * TQ] + pc[2 * TQ:3 * TQ] + pc[3 * TQ:4 * TQ]
    hi = psum.astype(BF16)
    lo = (psum - hi.astype(F32)).astype(BF16)
    pslc_t = (lax.dot_general(ovlt, hi, _NT, preferred_element_type=F32)
              + lax.dot_general(ovlt, lo, _NT, preferred_element_type=F32))
    top_k = min(SEL_TOPK, n_sel)
    sub = 8
    j = lax.broadcasted_iota(jnp.int32, (n_sel, TQ), 0)
    t = qi * TQ + lax.broadcasted_iota(jnp.int32, (n_sel, TQ), 1)
    dist = jnp.right_shift(t, int(math.log2(SEL_BLOCK))) - j
    score = jnp.where(dist < N_LOCAL_FORCED, FORCED_SCORE, pslc_t[SEL_LANE0:SEL_LANE0 + n_sel, :])
    score = jnp.where(j == 0, FORCED_SCORE, score)
    score = jnp.where(dist >= 0, score, NEG_INF)
    groups = [score[a * sub:(a + 1) * sub] for a in range(n_sel // sub)]
    cnts = [jnp.zeros((sub, TQ), F32) for _ in groups]
    j_sub = lax.broadcasted_iota(jnp.int32, (sub, TQ), 0)
    for i in range(n_sel):
        row = jnp.broadcast_to(score[i:i + 1, :], (sub, TQ))
        for a, grp in enumerate(groups):
            if a * sub > i:
                beats = jnp.where(row >= grp, 1.0, 0.0)
            elif a * sub + sub - 1 <= i:
                beats = jnp.where(row > grp, 1.0, 0.0)
            else:
                beats = jnp.where(j_sub + a * sub > i, jnp.where(row >= grp, 1.0, 0.0),
                                  jnp.where(row > grp, 1.0, 0.0))
            cnts[a] = cnts[a] + beats
    sel_t = jnp.where(jnp.concatenate(cnts, axis=0) < top_k, 0.0, NEG_INF)
    return jnp.concatenate(
        [jnp.zeros((SEL_LANE0, TQ), F32), sel_t,
         jnp.zeros((LANES - SEL_LANE0 - n_sel, TQ), F32)], axis=0).T


def _attn_step(shifted, q_ref, kct_ref, vc_ref, ks_ref, vs_ref, kw_ref, vw_ref, gate_ref,
               cb_ref, b0_ref, b1_ref, w2_ref, ovlt_ref, o_ref, qa_ref, acc_s, m_s, acc_w, m_w):
    qi = pl.program_id(1)
    groups = range(N_GROUPS)
    q = [q_ref[0, g * HPG:(g + 1) * HPG].reshape(ROWS, LANES) for g in groups]

    def key_rows(ref, g, kt, n_tiles=1):
        return ref[0, g, pl.ds(pl.multiple_of(kt * TQ, TQ), n_tiles * TQ), :]

    def compressed_and_selection(which=groups):
        o_c = []
        for g in which:
            s = _dot(q[g], kct_ref[0, g]) + cb_ref[g, 0]
            m = jnp.max(s, axis=-1, keepdims=True)
            e = jnp.exp2(s - m)
            l = jnp.sum(e, axis=-1, keepdims=True)
            pc = e * jnp.where(m > 0.5 * NEG_INF, 1.0 / l, 0.0)
            o_c.append(_dot(pc.astype(BF16), vc_ref[0, g]))
            selbias = _selection_bias(pc, qi, ovlt_ref[...], ks_ref.shape[2] // SEL_BLOCK)
            for h in range(HPG):
                qa_ref[g, h * TQ:(h + 1) * TQ, :] = (
                    q_ref[0, g * HPG + h].astype(F32) + selbias).astype(BF16)
        return o_c

    kt1 = jnp.maximum(qi - 1, 0)
    kt2 = jnp.maximum(qi - 2, 0)
    if shifted:
        has1 = jnp.where(qi >= 1, 1.0, 0.0)
        has2 = jnp.where(qi >= 2, 1.0, 0.0)
        a_w, o_c = [], []
        for g in groups:
            a_w.append(
                _shifted_pv(q[g], key_rows(kw_ref, g, kt2), key_rows(vw_ref, g, kt2), w2_ref[...], has2)
                + _shifted_pv(q[g], key_rows(kw_ref, g, kt1), key_rows(vw_ref, g, kt1), b1_ref[g], has1)
                + _shifted_pv(q[g], key_rows(kw_ref, g, qi), key_rows(vw_ref, g, qi), b0_ref[g]))
            o_c += compressed_and_selection([g])
        acc_s[...] = jnp.zeros(acc_s.shape, F32)
        n_plain = jnp.maximum(qi - 1, 0)

        def pair_body(it, carry):
            pv = [None] * N_GROUPS
            for half in range(2):
                for g in groups:
                    t = _shifted_pv(qa_ref[g], key_rows(ks_ref, g, 2 * it + half),
                                    key_rows(vs_ref, g, 2 * it + half), None)
                    pv[g] = t if pv[g] is None else pv[g] + t
            for g in groups:
                acc_s[g] += pv[g]
            return carry

        lax.fori_loop(0, n_plain // 2, pair_body, 0)
        odd = jnp.where(n_plain % 2 == 1, 1.0, 0.0)
        a_s = [acc_s[g]
               + _shifted_pv(qa_ref[g], key_rows(ks_ref, g, kt2), key_rows(vs_ref, g, kt2), None, odd)
               + _shifted_pv(qa_ref[g], key_rows(ks_ref, g, kt1), key_rows(vs_ref, g, kt1), b1_ref[g], has1)
               + _shifted_pv(qa_ref[g], key_rows(ks_ref, g, qi), key_rows(vs_ref, g, qi), b0_ref[g])
               for g in groups]
    else:
        o_c = compressed_and_selection()
        a_s, a_w = [], []
        for g in groups:
            qa = qa_ref[g]
            sel = (acc_s.at[g], m_s.at[g])
            win = (acc_w.at[g], m_w.at[g])
            for acc, m_ref in (sel, win):
                acc[...] = jnp.zeros(acc.shape, F32)
                m_ref[...] = jnp.full(m_ref.shape, NEG_INF, F32)

            def sel_body(kt, carry, g=g, qa=qa, sel=sel):
                _online_update(qa, key_rows(ks_ref, g, kt), key_rows(vs_ref, g, kt), None, *sel)
                return carry

            lax.fori_loop(0, qi - 1, sel_body, 0)

            @pl.when(qi >= 2)
            def _(g=g, win=win):
                _online_update(q[g], key_rows(kw_ref, g, kt2), key_rows(vw_ref, g, kt2), w2_ref[...], *win)

            @pl.when(qi >= 1)
            def _(g=g, qa=qa, sel=sel, win=win):
                _online_update(qa, key_rows(ks_ref, g, kt1), key_rows(vs_ref, g, kt1), b1_ref[g], *sel)
                _online_update(q[g], key_rows(kw_ref, g, kt1), key_rows(vw_ref, g, kt1), b1_ref[g], *win)

            _online_update(qa, key_rows(ks_ref, g, qi), key_rows(vs_ref, g, qi), b0_ref[g], *sel)
            _online_update(q[g], key_rows(kw_ref, g, qi), key_rows(vw_ref, g, qi), b0_ref[g], *win)
            a_s.append(acc_s[g])
            a_w.append(acc_w[g])

    outs = []
    for g in groups:
        o_s = a_s[g][:, :LANES] / a_s[g][:, LANES:]
        o_w = a_w[g][:, :LANES] / a_w[g][:, LANES:]
        gates = gate_ref[0, g]
        for h in range(HPG):
            rows = slice(h * TQ, (h + 1) * TQ)
            g_c = gates[:, h * N_BRANCH + 0:h * N_BRANCH + 1]
            g_s = gates[:, h * N_BRANCH + 1:h * N_BRANCH + 2]
            g_w = gates[:, h * N_BRANCH + 2:h * N_BRANCH + 3]
            o_h = g_c * o_c[g][rows] + g_s * o_s[rows] + g_w * o_w[rows]
            outs.append(o_h[:, :HEAD_DIM])
    o_ref[0] = jnp.concatenate(outs, axis=-1)


def _attn_kernel(flag_ref, *refs):
    use_shift = flag_ref[0] == 1

    @pl.when(use_shift)
    def _():
        _attn_step(True, *refs)

    @pl.when(jnp.logical_not(use_shift))
    def _():
        _attn_step(False, *refs)


def _attn(flag, q, kct, vc, ks, vs, kw, vw, gate, cb, b0, b1, w2, ovlt):
    bsz, _, seq, _ = q.shape
    nq = seq // TQ
    per_batch = lambda b, i: (b, 0, 0, 0)
    k_spec = pl.BlockSpec((1, N_GROUPS, seq, LANES), per_batch)
    v_spec = pl.BlockSpec((1, N_GROUPS, seq, V_LANES), per_batch)
    cmp_spec = pl.BlockSpec((1, N_GROUPS, N_CMP_PAD, LANES), per_batch)
    const = lambda shape: pl.BlockSpec(shape, lambda b, i: (0,) * len(shape),
                                       pipeline_mode=pl.Buffered(1))
    return pl.pallas_call(
        _attn_kernel,
        grid=(bsz, nq),
        in_specs=[pl.BlockSpec(memory_space=pltpu.SMEM),
                  pl.BlockSpec((1, N_HEADS, TQ, LANES), lambda b, i: (b, 0, i, 0)),
                  cmp_spec, cmp_spec, k_spec, v_spec, k_spec, v_spec,
                  pl.BlockSpec((1, N_GROUPS, TQ, LANES), lambda b, i: (b, 0, i, 0)),
                  pl.BlockSpec((N_GROUPS, 1, ROWS, N_CMP_PAD), lambda b, i: (0, i, 0, 0)),
                  const(b0.shape), const(b1.shape), const(w2.shape), const(ovlt.shape)],
        out_specs=pl.BlockSpec((1, TQ, D_ATTN), lambda b, i: (b, i, 0)),
        out_shape=jax.ShapeDtypeStruct((bsz, seq, D_ATTN), F32),
        scratch_shapes=[pltpu.VMEM((N_GROUPS, ROWS, LANES), BF16),
                        pltpu.VMEM((N_GROUPS, ROWS, V_LANES), F32),
                        pltpu.VMEM((N_GROUPS, ROWS, LANES), F32),
                        pltpu.VMEM((N_GROUPS, ROWS, V_LANES), F32),
                        pltpu.VMEM((N_GROUPS, ROWS, LANES), F32)],
        compiler_params=pltpu.CompilerParams(
            dimension_semantics=("arbitrary", "arbitrary"),
            vmem_limit_bytes=VMEM_LIMIT),
        name="attn",
    )(flag, q, kct, vc, ks, vs, kw, vw, gate, cb, b0, b1, w2, ovlt)


TS_OUT = 512
HALO = 16


def _outproj_ffn_kernel(ya_ref, bg_ref, u_ref, uh_ref, cw_ref, ga_ref, gc_ref, w_ref, x_ref, mod_ref,
                        g2_ref, w1_ref, w3_ref, w2_ref, o_ref):
    u = u_ref[0].astype(F32)
    halo = uh_ref[0].astype(F32) * jnp.where(pl.program_id(1) > 0, 1.0, 0.0)
    h1 = halo[HALO - 1:HALO, :]
    h2 = halo[HALO - 2:HALO - 1, :]
    row = lax.broadcasted_iota(jnp.int32, u.shape, 0)
    u1 = jnp.where(row == 0, h1, pltpu.roll(u, 1, 0))
    u2 = jnp.where(row == 0, h2, jnp.where(row == 1, h1, pltpu.roll(u, 2, 0)))
    conv = u2 * cw_ref[0:1, :] + u1 * cw_ref[1:2, :] + u * cw_ref[2:3, :]
    yc = bg_ref[0].astype(F32) * conv
    yc = yc * lax.rsqrt(jnp.mean(yc * yc, axis=-1, keepdims=True) + EPS) * gc_ref[...]
    ya = ya_ref[0]
    ya = ya * lax.rsqrt(jnp.mean(ya * ya, axis=-1, keepdims=True) + EPS) * ga_ref[...]
    proj = _dot(ya.astype(BF16), w_ref[0:D_ATTN, :]) + _dot(yc.astype(BF16), w_ref[D_ATTN:, :])
    x = x_ref[0] + mod_ref[0, 2:3, :] * proj

    ms = jnp.mean(x * x, axis=-1, keepdims=True)
    h = x * lax.rsqrt(ms + EPS) * g2_ref[...] * (1.0 + mod_ref[0, 4:5, :]) + mod_ref[0, 3:4, :]
    hb = h.astype(BF16)
    a = _dot(hb, w1_ref[...])
    b = _dot(hb, w3_ref[...])
    act = (_silu(a) * b).astype(BF16)
    o_ref[0] = x + mod_ref[0, 5:6, :] * _dot(act, w2_ref[...])


def _outproj_ffn(ya, bg, u, conv_w, ga, gc, w_out, x, mod3, g2, w1, w3, w2):
    bsz, seq, d = x.shape
    ts = TS_OUT
    c2 = lambda b, i: (0, 0)
    row_spec = lambda width: pl.BlockSpec((1, ts, width), lambda b, i: (b, i, 0))
    wspec = lambda w: pl.BlockSpec(w.shape, c2, pipeline_mode=pl.Buffered(1))
    return pl.pallas_call(
        _outproj_ffn_kernel,
        grid=(bsz, seq // ts),
        in_specs=[row_spec(D_ATTN), row_spec(D_CONV), row_spec(D_CONV),
                  pl.BlockSpec((1, HALO, D_CONV),
                               lambda b, i: (b, jnp.maximum(i * (ts // HALO) - 1, 0), 0)),
                  pl.BlockSpec(conv_w.shape, c2),
                  pl.BlockSpec((1, D_ATTN), c2), pl.BlockSpec((1, D_CONV), c2),
                  wspec(w_out),
                  row_spec(d),
                  pl.BlockSpec((1, 6, d), lambda b, i: (b, 0, 0)),
                  pl.BlockSpec((1, d), c2),
                  wspec(w1), wspec(w3), wspec(w2)],
        out_specs=row_spec(d),
        out_shape=jax.ShapeDtypeStruct((bsz, seq, d), F32),
        compiler_params=pltpu.CompilerParams(
            dimension_semantics=("arbitrary", "arbitrary"), vmem_limit_bytes=VMEM_LIMIT),
        name="outproj_ffn",
    )(ya, bg, u, u, conv_w, ga, gc, w_out, x, mod3, g2, w1, w3, w2)


def _t5_bucket_np(rel):
    n = np.maximum(rel, 0)
    max_exact = N_BUCKETS // 2
    nf = np.maximum(n, 1).astype(np.float32)
    large = max_exact + (np.log(nf / max_exact) / math.log(MAX_DISTANCE / max_exact)
                         * (N_BUCKETS - max_exact)).astype(np.int32)
    return np.where(n < max_exact, n, np.minimum(large, N_BUCKETS - 1)).astype(np.int32)


def _bias_of_rel(tab_rel, rel):
    idx = jnp.asarray(_t5_bucket_np(rel))
    tab_b = tab_rel.T.reshape((N_HEADS, N_BUCKETS) + (1,) * idx.ndim)
    vals = jnp.zeros((N_HEADS,) + idx.shape, F32)
    for k in range(N_BUCKETS):
        vals = jnp.where(idx == k, tab_b[:, k], vals)
    return jnp.where(jnp.asarray(rel >= 0), vals, NEG_INF)


def _toeplitz_offsets(length, n_cols):
    k = np.arange(length)
    return np.where(k < n_cols, k, k - length)


def _toeplitz(vec, n_rows, n_cols):
    length = vec.shape[-1]
    assert length >= n_rows + n_cols - 1
    m = jnp.tile(vec, (1,) * (vec.ndim - 1) + (n_rows,))[..., :n_rows * (length - 1)]
    return m.reshape(vec.shape[:-1] + (n_rows, length - 1))[..., :n_cols]


def _stack_heads(vals):
    return vals.reshape(N_GROUPS, HPG * vals.shape[-2], vals.shape[-1])


def _bias_tables(rel_bias_table, seq):
    tab_rel = (rel_bias_table - rel_bias_table[N_BUCKETS - 1:N_BUCKETS, :]) * LOG2E
    d = _toeplitz_offsets(2 * TQ - 1, TQ)
    b0 = _stack_heads(_toeplitz(_bias_of_rel(tab_rel, -d), TQ, TQ))
    b1 = _stack_heads(_toeplitz(_bias_of_rel(tab_rel, TQ - d), TQ, TQ))
    i = np.arange(TQ)[:, None]
    jj = np.arange(TQ)[None, :]
    w2 = jnp.asarray(np.tile(np.where(jj > i, 0.0, NEG_INF).astype(np.float32), (HPG, 1)))
    nq = seq // TQ
    n_cmp = (seq - CMP_BLOCK) // CMP_STRIDE + 1
    shift = TQ // CMP_STRIDE
    assert nq * shift <= N_CMP_PAD and n_cmp * CMP_STRIDE + CMP_BLOCK - 1 > seq
    off = N_CMP_PAD * CMP_STRIDE - (CMP_BLOCK - 1)
    d = _toeplitz_offsets(shift + 2 * N_CMP_PAD - 1, 2 * N_CMP_PAD)[None, :]
    r = np.arange(CMP_STRIDE)[:, None]
    pat = _toeplitz(_bias_of_rel(tab_rel, r + off - CMP_STRIDE * d),
                    shift, 2 * N_CMP_PAD)
    pat = _stack_heads(jnp.swapaxes(pat, 1, 2).reshape(N_HEADS, TQ, 2 * N_CMP_PAD))
    cb = jnp.stack([pat[:, :, N_CMP_PAD - qi * shift:2 * N_CMP_PAD - qi * shift]
                    for qi in range(nq)], axis=1)
    n_sel = seq // SEL_BLOCK
    cs = np.arange(n_cmp) * CMP_STRIDE
    ce = cs + CMP_BLOCK - 1
    ss = np.arange(n_sel) * SEL_BLOCK
    ov = np.clip(np.minimum(ce[:, None], ss[None, :] + SEL_BLOCK - 1)
                 - np.maximum(cs[:, None], ss[None, :]) + 1, 0, None) / CMP_STRIDE
    ovlt = np.zeros((LANES, N_CMP_PAD), np.float32)
    ovlt[SEL_LANE0:SEL_LANE0 + n_sel, :n_cmp] = ov.T
    return tab_rel, cb, b0, b1, w2, jnp.asarray(ovlt, BF16)


def kernel(x, c, w_ada, b_ada, norm1_gain, w_in, q_gain, k_cmp_gain, k_sel_gain, k_win_gain,
           cmp_pos_k, cmp_pos_v, w_ck1, w_ck2, w_cv1, w_cv2, rel_bias_table, conv_w,
           attn_out_gain, conv_out_gain, w_out, norm2_gain, w_ff1, w_ff3, w_ff2):
    bsz, seq, d = x.shape
    assert d == D_MODEL and seq % TQ == 0 and SEL_LANE0 + seq // SEL_BLOCK <= SHIFT_LANE
    assert (seq - CMP_BLOCK) // CMP_STRIDE + 1 <= N_CMP_PAD and seq // CMP_STRIDE == N_CMP_PAD

    seg_of = lambda n: jnp.asarray(np.kron(np.eye(n), np.ones((HEAD_DIM, HEAD_DIM))) / HEAD_DIM, BF16)
    seg4 = seg_of(MXU_N // HEAD_DIM)
    tab_rel, cb, b0, b1, w2m, ovlt = _bias_tables(rel_bias_table, seq)
    bias_max = jnp.max(jnp.abs(tab_rel))
    two = lambda g: jnp.tile(g.reshape(1, HEAD_DIM), (1, 2))
    n_gate = N_BRANCH * HPG
    gate_pad = jnp.zeros((d, LANES - n_gate), F32)

    for layer in range(w_in.shape[0]):
        wi = w_in[layer]
        o_g = D_ATTN + 6 * D_KV
        kv = lambda n: wi[:, D_ATTN + n * D_KV:D_ATTN + (n + 1) * D_KV]
        w_p = jnp.concatenate(
            [wi[:, :D_ATTN], kv(0), kv(1), kv(2), kv(4), kv(3), kv(5),
             wi[:, o_g:o_g + n_gate], gate_pad,
             wi[:, o_g + n_gate:o_g + 2 * n_gate], gate_pad, wi[:, o_g + 2 * n_gate:]],
            axis=-1).astype(BF16)
        w1k, w1v = w_ck1[layer].astype(BF16), w_cv1[layer].astype(BF16)
        w2k, w2v = w_ck2[layer].astype(BF16), w_cv2[layer].astype(BF16)
        pk2 = cmp_pos_k[layer].reshape(2, CMP_STRIDE * HEAD_DIM)
        pv2 = cmp_pos_v[layer].reshape(2, CMP_STRIDE * HEAD_DIM)

        gq_max = jnp.max(jnp.abs(q_gain[layer]))
        bound = lambda gk: (HEAD_DIM ** 0.5 * LOG2E * 1.01) * gq_max * jnp.max(jnp.abs(gk)) + bias_max
        c_sel, c_win = bound(k_sel_gain[layer]), bound(k_win_gain[layer])
        flag = (2.0 * jnp.maximum(c_sel, c_win) <= MAX_SHIFTED_RANGE).astype(jnp.int32).reshape(1)
        tails = jnp.zeros((2, HEAD_DIM), F32).at[:, SHIFT_LANE - HEAD_DIM].set(
            -jnp.stack([c_sel, c_win]))

        mod3 = _adaln(c, w_ada[layer], b_ada[layer]).reshape(bsz, 6, d)
        q, kvc, ks, vs, kw, vw, gate, bg, u = _inproj(
            x, mod3, norm1_gain[layer].reshape(1, d), w_p, seg4,
            jnp.tile(q_gain[layer].reshape(1, HEAD_DIM), (1, MXU_N // HEAD_DIM)),
            jnp.concatenate([two(k_sel_gain[layer]), two(k_win_gain[layer])], axis=-1), tails)
        kct, vc = _compress(kvc, w1k, w1v, w2k, w2v, pk2, pv2,
                            k_cmp_gain[layer].reshape(1, HEAD_DIM))
        y_attn = _attn(flag, q, kct, vc, ks, vs, kw, vw, gate, cb, b0, b1, w2m, ovlt)
        x = _outproj_ffn(y_attn, bg, u, conv_w[layer], attn_out_gain[layer].reshape(1, D_ATTN),
                         conv_out_gain[layer].reshape(1, D_CONV), w_out[layer].astype(BF16), x, mod3,
                         norm2_gain[layer].reshape(1, d), w_ff1[layer].astype(BF16),
                         w_ff3[layer].astype(BF16), w_ff2[layer].astype(BF16))
    return x
```

```python
import math

import numpy as np
import jax
import jax.numpy as jnp
from jax import lax
from jax.experimental import pallas as pl
from jax.experimental.pallas import tpu as pltpu

F32 = jnp.float32
BF16 = jnp.bfloat16

D_MODEL = 1024
HEAD_DIM = 64
N_HEADS = 8
N_GROUPS = 2
HPG = N_HEADS // N_GROUPS
D_ATTN = N_HEADS * HEAD_DIM
D_KV = N_GROUPS * HEAD_DIM
N_BRANCH = 3
D_CONV = D_MODEL - D_ATTN
CONV_WIDTH = 3
CMP_BLOCK = 32
CMP_STRIDE = 16
CMP_HIDDEN = 256
SEL_BLOCK = 64
SEL_TOPK = 16
N_LOCAL_FORCED = 2
WINDOW = 512
N_BUCKETS = 32
MAX_DISTANCE = 128
EPS = 1e-6
NEG_INF = -1e30
FORCED_SCORE = 1e6

LANES = 128
TQ = 256
ROWS = HPG * TQ
N_CMP_PAD = 128
SEL_LANE0 = 64
SHIFT_LANE = 96
V_LANES = 256
LOG2E = math.log2(math.e)
MAX_SHIFTED_RANGE = 100.0
VMEM_LIMIT = 56 * 1024 * 1024

_NT = (((1,), (1,)), ((), ()))


def _dot(a, b):
    return jnp.dot(a, b, preferred_element_type=F32)


def _silu(v):
    return v * (1.0 / (1.0 + jnp.exp(-v)))


def _sigmoid(v):
    return 1.0 / (1.0 + jnp.exp(-v))


def _seg_mean_sq(v, seg):
    sq = v * v
    hi = sq.astype(BF16)
    lo = (sq - hi.astype(F32)).astype(BF16)
    return _dot(hi, seg) + _dot(lo, seg)


def _adaln_kernel(c_ref, w_ref, b_ref, o_ref):
    sc = _silu(c_ref[...]).astype(BF16)
    o_ref[...] = _dot(sc, w_ref[...].astype(BF16)) + b_ref[...]


def _adaln(c, w_ada, b_ada):
    bsz, d = c.shape
    n = w_ada.shape[1]
    tn = 1536
    return pl.pallas_call(
        _adaln_kernel,
        grid=(n // tn,),
        in_specs=[pl.BlockSpec((bsz, d), lambda j: (0, 0)),
                  pl.BlockSpec((d, tn), lambda j: (0, j)),
                  pl.BlockSpec((1, tn), lambda j: (0, j))],
        out_specs=pl.BlockSpec((bsz, tn), lambda j: (0, j)),
        out_shape=jax.ShapeDtypeStruct((bsz, n), F32),
        compiler_params=pltpu.CompilerParams(
            dimension_semantics=("arbitrary",), vmem_limit_bytes=VMEM_LIMIT),
        name="adaln",
    )(c, w_ada, b_ada.reshape(1, n))


C_Q, C_KVC, C_KK, C_VV, C_GATE, C_BG, C_CG, C_XT, C_END = (
    0, 512, 768, 1024, 1280, 1536, 2048, 2560, 3072)
MXU_N = 256
TS_IN = 1024


def _inproj_kernel(x_ref, mod_ref, g1_ref, w_ref, seg_ref, gq_ref, gkk_ref, tail_ref,
                   q_ref, kvc_ref, ks_ref, vs_ref, kw_ref, vw_ref, gate_ref, bg_ref, u_ref):
    ts = x_ref.shape[1]
    x = x_ref[0]
    ms = jnp.mean(x * x, axis=-1, keepdims=True)
    y = x * lax.rsqrt(ms + EPS) * g1_ref[...]
    h = y * (1.0 + mod_ref[0, 1:2, :]) + mod_ref[0, 0:1, :]
    hb = h.astype(BF16)
    seg = seg_ref[...]

    def proj(c0, c1):
        return _dot(hb, w_ref[:, c0:c1])

    lane = lax.broadcasted_iota(jnp.int32, (ts, HEAD_DIM), 1)
    ones192 = jnp.ones((ts, V_LANES - HEAD_DIM), F32)
    q_tail = jnp.where(lane == SHIFT_LANE - HEAD_DIM, 1.0, 0.0).astype(F32)

    heads_per_tile = MXU_N // HEAD_DIM
    row = pl.program_id(1) * ts + lax.broadcasted_iota(jnp.int32, (ts, HEAD_DIM), 0)
    onehot = jnp.where(lane == row // SEL_BLOCK, 1.0, 0.0).astype(F32) + tail_ref[0:1, :]
    win_tail = jnp.broadcast_to(tail_ref[1:2, :], (ts, HEAD_DIM))

    def do_q(c):
        v = proj(C_Q + c * MXU_N, C_Q + (c + 1) * MXU_N)
        vn = v * lax.rsqrt(_seg_mean_sq(v, seg) + EPS) * gq_ref[...] * (HEAD_DIM ** -0.5 * LOG2E)
        for hh in range(heads_per_tile):
            qh = jnp.concatenate([vn[:, hh * HEAD_DIM:(hh + 1) * HEAD_DIM], q_tail], axis=-1)
            q_ref[0, c * heads_per_tile + hh] = qh.astype(BF16)

    def do_kvc():
        v = proj(C_KVC, C_KK)
        kvc_ref[0, 0] = v[:, :D_KV]
        kvc_ref[0, 1] = v[:, D_KV:]

    def do_kk():
        v = proj(C_KK, C_VV)
        vn = v * lax.rsqrt(_seg_mean_sq(v, seg) + EPS) * gkk_ref[...]
        for g in range(N_GROUPS):
            ks_ref[0, g] = jnp.concatenate(
                [vn[:, g * HEAD_DIM:(g + 1) * HEAD_DIM], onehot], axis=-1).astype(BF16)
            kw_ref[0, g] = jnp.concatenate(
                [vn[:, D_KV + g * HEAD_DIM:D_KV + (g + 1) * HEAD_DIM], win_tail], axis=-1).astype(BF16)

    def do_vv():
        v = proj(C_VV, C_GATE)
        for g in range(N_GROUPS):
            vs_ref[0, g] = jnp.concatenate(
                [v[:, g * HEAD_DIM:(g + 1) * HEAD_DIM], ones192], axis=-1).astype(BF16)
            vw_ref[0, g] = jnp.concatenate(
                [v[:, D_KV + g * HEAD_DIM:D_KV + (g + 1) * HEAD_DIM], ones192], axis=-1).astype(BF16)

    def do_gate():
        v = _sigmoid(proj(C_GATE, C_BG))
        for g in range(N_GROUPS):
            gate_ref[0, g] = v[:, g * LANES:(g + 1) * LANES]

    def do_bg(c):
        cols = slice(c * MXU_N, (c + 1) * MXU_N)
        bg_ref[0, :, cols] = proj(C_BG + c * MXU_N, C_BG + (c + 1) * MXU_N).astype(BF16)

    def do_u(c):
        cols = slice(c * MXU_N, (c + 1) * MXU_N)
        u_ref[0, :, cols] = (proj(C_CG + c * MXU_N, C_CG + (c + 1) * MXU_N)
                             * proj(C_XT + c * MXU_N, C_XT + (c + 1) * MXU_N)).astype(BF16)

    do_q(0)
    do_bg(0)
    do_q(1)
    do_bg(1)
    do_kvc()
    do_u(0)
    do_kk()
    do_u(1)
    do_gate()
    do_vv()


def _inproj(x, mod3, g1, w_p, seg, gq4, gkk4, tails):
    bsz, seq, d = x.shape
    ts = TS_IN
    const2 = lambda b, i: (0, 0)
    k_shape = jax.ShapeDtypeStruct((bsz, N_GROUPS, seq, LANES), BF16)
    k_spec = pl.BlockSpec((1, N_GROUPS, ts, LANES), lambda b, i: (b, 0, i, 0))
    v_shape = jax.ShapeDtypeStruct((bsz, N_GROUPS, seq, V_LANES), BF16)
    v_spec = pl.BlockSpec((1, N_GROUPS, ts, V_LANES), lambda b, i: (b, 0, i, 0))
    return pl.pallas_call(
        _inproj_kernel,
        grid=(bsz, seq // ts),
        in_specs=[pl.BlockSpec((1, ts, d), lambda b, i: (b, i, 0)),
                  pl.BlockSpec((1, 6, d), lambda b, i: (b, 0, 0)),
                  pl.BlockSpec((1, d), const2),
                  pl.BlockSpec(w_p.shape, const2, pipeline_mode=pl.Buffered(1)),
                  pl.BlockSpec(seg.shape, const2),
                  pl.BlockSpec(gq4.shape, const2),
                  pl.BlockSpec(gkk4.shape, const2),
                  pl.BlockSpec(tails.shape, const2)],
        out_specs=[pl.BlockSpec((1, N_HEADS, ts, LANES), lambda b, i: (b, 0, i, 0)),
                   pl.BlockSpec((1, 2, ts, D_KV), lambda b, i: (b, 0, i, 0)),
                   k_spec, v_spec, k_spec, v_spec,
                   pl.BlockSpec((1, N_GROUPS, ts, LANES), lambda b, i: (b, 0, i, 0)),
                   pl.BlockSpec((1, ts, D_CONV), lambda b, i: (b, i, 0)),
                   pl.BlockSpec((1, ts, D_CONV), lambda b, i: (b, i, 0))],
        out_shape=[jax.ShapeDtypeStruct((bsz, N_HEADS, seq, LANES), BF16),
                   jax.ShapeDtypeStruct((bsz, 2, seq, D_KV), F32),
                   k_shape, v_shape, k_shape, v_shape,
                   jax.ShapeDtypeStruct((bsz, N_GROUPS, seq, LANES), F32),
                   jax.ShapeDtypeStruct((bsz, seq, D_CONV), BF16),
                   jax.ShapeDtypeStruct((bsz, seq, D_CONV), BF16)],
        compiler_params=pltpu.CompilerParams(
            dimension_semantics=("arbitrary", "arbitrary"), vmem_limit_bytes=VMEM_LIMIT),
        name="inproj",
    )(x, mod3, g1, w_p, seg, gq4, gkk4, tails)


def _compress_kernel(kvc_ref, w1k_ref, w1v_ref, w2k_ref, w2v_ref, pk_ref, pv_ref, gk_ref,
                     kct_ref, vc_ref):
    n_rows = kvc_ref.shape[2] // CMP_STRIDE
    half = CMP_STRIDE * HEAD_DIM
    zlane = jnp.zeros((n_rows, HEAD_DIM), F32)

    def branch(which, w1_ref, w2_ref, pos_ref):
        rows = [kvc_ref[0, which, pl.ds(r, n_rows, stride=CMP_STRIDE), :] for r in range(CMP_STRIDE)]
        outs = []
        for g in range(N_GROUPS):
            x = jnp.concatenate([rw[:, g * HEAD_DIM:(g + 1) * HEAD_DIM] for rw in rows], axis=-1)
            a1 = (x + pos_ref[0:1, :]).astype(BF16)
            a2 = (x + pos_ref[1:2, :]).astype(BF16)
            hid = _dot(a1, w1_ref[0:half, :]) + pltpu.roll(_dot(a2, w1_ref[half:, :]), n_rows - 1, 0)
            outs.append(_dot(_silu(hid).astype(BF16), w2_ref[...]))
        return outs

    kc = branch(0, w1k_ref, w2k_ref, pk_ref)
    vc = branch(1, w1v_ref, w2v_ref, pv_ref)
    for g in range(N_GROUPS):
        kn = kc[g] * lax.rsqrt(jnp.mean(kc[g] * kc[g], axis=-1, keepdims=True) + EPS) * gk_ref[...]
        kct_ref[0, g] = jnp.concatenate([kn, zlane], axis=-1).T.astype(BF16)
        vc_ref[0, g] = jnp.concatenate([vc[g], zlane], axis=-1).astype(BF16)


def _compress(kvc, w1k, w1v, w2k, w2v, pk2, pv2, gk):
    bsz, _, seq, _ = kvc.shape
    c2 = lambda b: (0, 0)
    out_shape = jax.ShapeDtypeStruct((bsz, N_GROUPS, N_CMP_PAD, LANES), BF16)
    out_spec = pl.BlockSpec((1, N_GROUPS, N_CMP_PAD, LANES), lambda b: (b, 0, 0, 0))
    return pl.pallas_call(
        _compress_kernel,
        grid=(bsz,),
        in_specs=[pl.BlockSpec((1, 2, seq, D_KV), lambda b: (b, 0, 0, 0)),
                  pl.BlockSpec(w1k.shape, c2), pl.BlockSpec(w1v.shape, c2),
                  pl.BlockSpec(w2k.shape, c2), pl.BlockSpec(w2v.shape, c2),
                  pl.BlockSpec(pk2.shape, c2), pl.BlockSpec(pv2.shape, c2),
                  pl.BlockSpec(gk.shape, c2)],
        out_specs=[out_spec, out_spec],
        out_shape=[out_shape, out_shape],
        compiler_params=pltpu.CompilerParams(
            dimension_semantics=("arbitrary",), vmem_limit_bytes=VMEM_LIMIT),
        name="compress",
    )(kvc, w1k, w1v, w2k, w2v, pk2, pv2, gk)


def _logits(qa, k_rows, bias):
    s = lax.dot_general(qa, k_rows, _NT, preferred_element_type=F32)
    return s if bias is None else s + bias


def _shifted_pv(qa, k_rows, v_rows, bias, keep=None):
    if keep is not None:
        v_rows = v_rows * keep.astype(BF16)
    return _dot(jnp.exp2(_logits(qa, k_rows, bias)).astype(BF16), v_rows)


def _online_update(qa, k_tile, v_tile, bias, acc_ref, m_ref):
    s = _logits(qa, k_tile, bias)
    m_prev = m_ref[...]
    m_new = jnp.maximum(m_prev, jnp.max(s, axis=-1, keepdims=True))
    alpha = jnp.exp2(m_prev - m_new)
    p = jnp.exp2(s - jnp.concatenate([m_new] * (TQ // LANES), axis=-1))
    acc_ref[...] = (acc_ref[...] * jnp.concatenate([alpha] * (V_LANES // LANES), axis=-1)
                    + _dot(p.astype(BF16), v_tile))
    m_ref[...] = m_new


def _selection_bias(pc, qi, ovlt, n_sel):
    psum = pc[0:TQ] + pc[TQ:2 * TQ] + pc[2 * TQ:3 * TQ] + pc[3 * TQ:4 * TQ]
    hi = psum.astype(BF16)
    lo = (psum - hi.astype(F32)).astype(BF16)
    pslc_t = (lax.dot_general(ovlt, hi, _NT, preferred_element_type=F32)
              + lax.dot_general(ovlt, lo, _NT, preferred_element_type=F32))
    top_k = min(SEL_TOPK, n_sel)
    sub = 8
    j = lax.broadcasted_iota(jnp.int32, (n_sel, TQ), 0)
    t = qi * TQ + lax.broadcasted_iota(jnp.int32, (n_sel, TQ), 1)
    dist = jnp.right_shift(t, int(math.log2(SEL_BLOCK))) - j
    score = jnp.where(dist < N_LOCAL_FORCED, FORCED_SCORE, pslc_t[SEL_LANE0:SEL_LANE0 + n_sel, :])
    score = jnp.where(j == 0, FORCED_SCORE, score)
    score = jnp.where(dist >= 0, score, NEG_INF)
    groups = [score[a * sub:(a + 1) * sub] for a in range(n_sel // sub)]
    cnts = [jnp.zeros((sub, TQ), F32) for _ in groups]
    j_sub = lax.broadcasted_iota(jnp.int32, (sub, TQ), 0)
    for i in range(n_sel):
        row = jnp.broadcast_to(score[i:i + 1, :], (sub, TQ))
        for a, grp in enumerate(groups):
            if a * sub > i:
                beats = jnp.where(row >= grp, 1.0, 0.0)
            elif a * sub + sub - 1 <= i:
                beats = jnp.where(row > grp, 1.0, 0.0)
            else:
                beats = jnp.where(j_sub + a * sub > i, jnp.where(row >= grp, 1.0, 0.0),
                                  jnp.where(row > grp, 1.0, 0.0))
            cnts[a] = cnts[a] + beats
    sel_t = jnp.where(jnp.concatenate(cnts, axis=0) < top_k, 0.0, NEG_INF)
    return jnp.concatenate(
        [jnp.zeros((SEL_LANE0, TQ), F32), sel_t,
         jnp.zeros((LANES - SEL_LANE0 - n_sel, TQ), F32)], axis=0).T


def _attn_step(shifted, q_ref, kct_ref, vc_ref, ks_ref, vs_ref, kw_ref, vw_ref, gate_ref,
               cb_ref, b0_ref, b1_ref, w2_ref, ovlt_ref, o_ref, qa_ref, acc_s, m_s, acc_w, m_w):
    qi = pl.program_id(1)
    groups = range(N_GROUPS)
    q = [q_ref[0, g * HPG:(g + 1) * HPG].reshape(ROWS, LANES) for g in groups]

    def key_rows(ref, g, kt, n_tiles=1):
        return ref[0, g, pl.ds(pl.multiple_of(kt * TQ, TQ), n_tiles * TQ), :]

    def compressed_and_selection(which=groups):
        o_c = []
        for g in which:
            s = _dot(q[g], kct_ref[0, g]) + cb_ref[g, 0]
            m = jnp.max(s, axis=-1, keepdims=True)
            e = jnp.exp2(s - m)
            l = jnp.sum(e, axis=-1, keepdims=True)
            pc = e * jnp.where(m > 0.5 * NEG_INF, 1.0 / l, 0.0)
            o_c.append(_dot(pc.astype(BF16), vc_ref[0, g]))
            selbias = _selection_bias(pc, qi, ovlt_ref[...], ks_ref.shape[2] // SEL_BLOCK)
            for h in range(HPG):
                qa_ref[g, h * TQ:(h + 1) * TQ, :] = (
                    q_ref[0, g * HPG + h].astype(F32) + selbias).astype(BF16)
        return o_c

    kt1 = jnp.maximum(qi - 1, 0)
    kt2 = jnp.maximum(qi - 2, 0)
    if shifted:
        has1 = jnp.where(qi >= 1, 1.0, 0.0)
        has2 = jnp.where(qi >= 2, 1.0, 0.0)
        a_w, o_c = [], []
        for g in groups:
            a_w.append(
                _shifted_pv(q[g], key_rows(kw_ref, g, kt2), key_rows(vw_ref, g, kt2), w2_ref[...], has2)
                + _shifted_pv(q[g], key_rows(kw_ref, g, kt1), key_rows(vw_ref, g, kt1), b1_ref[g], has1)
                + _shifted_pv(q[g], key_rows(kw_ref, g, qi), key_rows(vw_ref, g, qi), b0_ref[g]))
            o_c += compressed_and_selection([g])
        acc_s[...] = jnp.zeros(acc_s.shape, F32)
        n_plain = jnp.maximum(qi - 1, 0)

        def pair_body(it, carry):
            pv = [None] * N_GROUPS
            for half in range(2):
                for g in groups:
                    t = _shifted_pv(qa_ref[g], key_rows(ks_ref, g, 2 * it + half),
                                    key_rows(vs_ref, g, 2 * it + half), None)
                    pv[g] = t if pv[g] is None else pv[g] + t
            for g in groups:
                acc_s[g] += pv[g]
            return carry

        lax.fori_loop(0, n_plain // 2, pair_body, 0)
        odd = jnp.where(n_plain % 2 == 1, 1.0, 0.0)
        a_s = [acc_s[g]
               + _shifted_pv(qa_ref[g], key_rows(ks_ref, g, kt2), key_rows(vs_ref, g, kt2), None, odd)
               + _shifted_pv(qa_ref[g], key_rows(ks_ref, g, kt1), key_rows(vs_ref, g, kt1), b1_ref[g], has1)
               + _shifted_pv(qa_ref[g], key_rows(ks_ref, g, qi), key_rows(vs_ref, g, qi), b0_ref[g])
               for g in groups]
    else:
        o_c = compressed_and_selection()
        a_s, a_w = [], []
        for g in groups:
            qa = qa_ref[g]
            sel = (acc_s.at[g], m_s.at[g])
            win = (acc_w.at[g], m_w.at[g])
            for acc, m_ref in (sel, win):
                acc[...] = jnp.zeros(acc.shape, F32)
                m_ref[...] = jnp.full(m_ref.shape, NEG_INF, F32)

            def sel_body(kt, carry, g=g, qa=qa, sel=sel):
                _online_update(qa, key_rows(ks_ref, g, kt), key_rows(vs_ref, g, kt), None, *sel)
                return carry

            lax.fori_loop(0, qi - 1, sel_body, 0)

            @pl.when(qi >= 2)
            def _(g=g, win=win):
                _online_update(q[g], key_rows(kw_ref, g, kt2), key_rows(vw_ref, g, kt2), w2_ref[...], *win)

            @pl.when(qi >= 1)
            def _(g=g, qa=qa, sel=sel, win=win):
                _online_update(qa, key_rows(ks_ref, g, kt1), key_rows(vs_ref, g, kt1), b1_ref[g], *sel)
                _online_update(q[g], key_rows(kw_ref, g, kt1), key_rows(vw_ref, g, kt1), b1_ref[g], *win)

            _online_update(qa, key_rows(ks_ref, g, qi), key_rows(vs_ref, g, qi), b0_ref[g], *sel)
            _online_update(q[g], key_rows(kw_ref, g, qi), key_rows(vw_ref, g, qi), b0_ref[g], *win)
            a_s.append(acc_s[g])
            a_w.append(acc_w[g])

    outs = []
    for g in groups:
        o_s = a_s[g][:, :LANES] / a_s[g][:, LANES:]
        o_w = a_w[g][:, :LANES] / a_w[g][:, LANES:]
        gates = gate_ref[0, g]
        for h in range(HPG):
            rows = slice(h * TQ, (h + 1) * TQ)
            g_c = gates[:, h * N_BRANCH + 0:h * N_BRANCH + 1]
            g_s = gates[:, h * N_BRANCH + 1:h * N_BRANCH + 2]
            g_w = gates[:, h * N_BRANCH + 2:h * N_BRANCH + 3]
            o_h = g_c * o_c[g][rows] + g_s * o_s[rows] + g_w * o_w[rows]
            outs.append(o_h[:, :HEAD_DIM])
    o_ref[0] = jnp.concatenate(outs, axis=-1)


def _attn_kernel(flag_ref, *refs):
    use_shift = flag_ref[0] == 1

    @pl.when(use_shift)
    def _():
        _attn_step(True, *refs)

    @pl.when(jnp.logical_not(use_shift))
    def _():
        _attn_step(False, *refs)


def _attn(flag, q, kct, vc, ks, vs, kw, vw, gate, cb, b0, b1, w2, ovlt):
    bsz, _, seq, _ = q.shape
    nq = seq // TQ
    per_batch = lambda b, i: (b, 0, 0, 0)
    k_spec = pl.BlockSpec((1, N_GROUPS, seq, LANES), per_batch)
    v_spec = pl.BlockSpec((1, N_GROUPS, seq, V_LANES), per_batch)
    cmp_spec = pl.BlockSpec((1, N_GROUPS, N_CMP_PAD, LANES), per_batch)
    const = lambda shape: pl.BlockSpec(shape, lambda b, i: (0,) * len(shape),
                                       pipeline_mode=pl.Buffered(1))
    return pl.pallas_call(
        _attn_kernel,
        grid=(bsz, nq),
        in_specs=[pl.BlockSpec(memory_space=pltpu.SMEM),
                  pl.BlockSpec((1, N_HEADS, TQ, LANES), lambda b, i: (b, 0, i, 0)),
                  cmp_spec, cmp_spec, k_spec, v_spec, k_spec, v_spec,
                  pl.BlockSpec((1, N_GROUPS, TQ, LANES), lambda b, i: (b, 0, i, 0)),
                  pl.BlockSpec((N_GROUPS, 1, ROWS, N_CMP_PAD), lambda b, i: (0, i, 0, 0)),
                  const(b0.shape), const(b1.shape), const(w2.shape), const(ovlt.shape)],
        out_specs=pl.BlockSpec((1, TQ, D_ATTN), lambda b, i: (b, i, 0)),
        out_shape=jax.ShapeDtypeStruct((bsz, seq, D_ATTN), F32),
        scratch_shapes=[pltpu.VMEM((N_GROUPS, ROWS, LANES), BF16),
                        pltpu.VMEM((N_GROUPS, ROWS, V_LANES), F32),
                        pltpu.VMEM((N_GROUPS, ROWS, LANES), F32),
                        pltpu.VMEM((N_GROUPS, ROWS, V_LANES), F32),
                        pltpu.VMEM((N_GROUPS, ROWS, LANES), F32)],
        compiler_params=pltpu.CompilerParams(
            dimension_semantics=("arbitrary", "arbitrary"),
            vmem_limit_bytes=VMEM_LIMIT),
        name="attn",
    )(flag, q, kct, vc, ks, vs, kw, vw, gate, cb, b0, b1, w2, ovlt)


TS_OUT = 512
HALO = 16


def _outproj_ffn_kernel(ya_ref, bg_ref, u_ref, uh_ref, cw_ref, ga_ref, gc_ref, w_ref, x_ref, mod_ref,
                        g2_ref, w1_ref, w3_ref, w2_ref, o_ref):
    u = u_ref[0].astype(F32)
    halo = uh_ref[0].astype(F32) * jnp.where(pl.program_id(1) > 0, 1.0, 0.0)
    h1 = halo[HALO - 1:HALO, :]
    h2 = halo[HALO - 2:HALO - 1, :]
    row = lax.broadcasted_iota(jnp.int32, u.shape, 0)
    u1 = jnp.where(row == 0, h1, pltpu.roll(u, 1, 0))
    u2 = jnp.where(row == 0, h2, jnp.where(row == 1, h1, pltpu.roll(u, 2, 0)))
    conv = u2 * cw_ref[0:1, :] + u1 * cw_ref[1:2, :] + u * cw_ref[2:3, :]
    yc = bg_ref[0].astype(F32) * conv
    yc = yc * lax.rsqrt(jnp.mean(yc * yc, axis=-1, keepdims=True) + EPS) * gc_ref[...]
    ya = ya_ref[0]
    ya = ya * lax.rsqrt(jnp.mean(ya * ya, axis=-1, keepdims=True) + EPS) * ga_ref[...]
    proj = _dot(ya.astype(BF16), w_ref[0:D_ATTN, :]) + _dot(yc.astype(BF16), w_ref[D_ATTN:, :])
    x = x_ref[0] + mod_ref[0, 2:3, :] * proj

    ms = jnp.mean(x * x, axis=-1, keepdims=True)
    h = x * lax.rsqrt(ms + EPS) * g2_ref[...] * (1.0 + mod_ref[0, 4:5, :]) + mod_ref[0, 3:4, :]
    hb = h.astype(BF16)
    a = _dot(hb, w1_ref[...])
    b = _dot(hb, w3_ref[...])
    act = (_silu(a) * b).astype(BF16)
    o_ref[0] = x + mod_ref[0, 5:6, :] * _dot(act, w2_ref[...])


def _outproj_ffn(ya, bg, u, conv_w, ga, gc, w_out, x, mod3, g2, w1, w3, w2):
    bsz, seq, d = x.shape
    ts = TS_OUT
    c2 = lambda b, i: (0, 0)
    row_spec = lambda width: pl.BlockSpec((1, ts, width), lambda b, i: (b, i, 0))
    wspec = lambda w: pl.BlockSpec(w.shape, c2, pipeline_mode=pl.Buffered(1))
    return pl.pallas_call(
        _outproj_ffn_kernel,
        grid=(bsz, seq // ts),
        in_specs=[row_spec(D_ATTN), row_spec(D_CONV), row_spec(D_CONV),
                  pl.BlockSpec((1, HALO, D_CONV),
                               lambda b, i: (b, jnp.maximum(i * (ts // HALO) - 1, 0), 0)),
                  pl.BlockSpec(conv_w.shape, c2),
                  pl.BlockSpec((1, D_ATTN), c2), pl.BlockSpec((1, D_CONV), c2),
                  wspec(w_out),
                  row_spec(d),
                  pl.BlockSpec((1, 6, d), lambda b, i: (b, 0, 0)),
                  pl.BlockSpec((1, d), c2),
                  wspec(w1), wspec(w3), wspec(w2)],
        out_specs=row_spec(d),
        out_shape=jax.ShapeDtypeStruct((bsz, seq, d), F32),
        compiler_params=pltpu.CompilerParams(
            dimension_semantics=("arbitrary", "arbitrary"), vmem_limit_bytes=VMEM_LIMIT),
        name="outproj_ffn",
    )(ya, bg, u, u, conv_w, ga, gc, w_out, x, mod3, g2, w1, w3, w2)


def _t5_bucket_np(rel):
    n = np.maximum(rel, 0)
    max_exact = N_BUCKETS // 2
    nf = np.maximum(n, 1).astype(np.float32)
    large = max_exact + (np.log(nf / max_exact) / math.log(MAX_DISTANCE / max_exact)
                         * (N_BUCKETS - max_exact)).astype(np.int32)
    return np.where(n < max_exact, n, np.minimum(large, N_BUCKETS - 1)).astype(np.int32)


def _bias_of_rel(tab_rel, rel):
    idx = jnp.asarray(_t5_bucket_np(rel))
    tab_b = tab_rel.T.reshape((N_HEADS, N_BUCKETS) + (1,) * idx.ndim)
    vals = jnp.zeros((N_HEADS,) + idx.shape, F32)
    for k in range(N_BUCKETS):
        vals = jnp.where(idx == k, tab_b[:, k], vals)
    return jnp.where(jnp.asarray(rel >= 0), vals, NEG_INF)


def _toeplitz_offsets(length, n_cols):
    k = np.arange(length)
    return np.where(k < n_cols, k, k - length)


TOEPLITZ_LEN = 2 * TQ
PATTERN_LEN = 3 * LANES


def _tables_kernel(v0_ref, v1_ref, vp_ref, b0_ref, b1_ref, cb_ref):
    def toeplitz_tile(v_ref):
        rows = jnp.broadcast_to(v_ref[0], (TQ, TOEPLITZ_LEN))
        return pltpu.roll(rows, 0, 1, stride=1, stride_axis=0)[:, :TQ]

    b0_ref[0] = toeplitz_tile(v0_ref)
    b1_ref[0] = toeplitz_tile(v1_ref)
    vp = vp_ref[0]
    shift = TQ // CMP_STRIDE
    pat = jnp.concatenate(
        [vp[:, :2 * N_CMP_PAD]] + [pltpu.roll(vp, a, 1)[:, :2 * N_CMP_PAD] for a in range(1, shift)],
        axis=0)
    for qi in range(cb_ref.shape[1]):
        cb_ref[0, qi] = pat[:, N_CMP_PAD - qi * shift:2 * N_CMP_PAD - qi * shift]


def _bias_tables(rel_bias_table, seq):
    tab_rel = (rel_bias_table - rel_bias_table[N_BUCKETS - 1:N_BUCKETS, :]) * LOG2E
    nq = seq // TQ
    n_cmp = (seq - CMP_BLOCK) // CMP_STRIDE + 1
    shift = TQ // CMP_STRIDE
    assert nq * shift <= N_CMP_PAD and n_cmp * CMP_STRIDE + CMP_BLOCK - 1 > seq
    assert shift + 2 * N_CMP_PAD - 1 <= PATTERN_LEN
    d = _toeplitz_offsets(TOEPLITZ_LEN, TQ)
    v0 = _bias_of_rel(tab_rel, -d)[:, None, :]
    v1 = _bias_of_rel(tab_rel, TQ - d)[:, None, :]
    off = N_CMP_PAD * CMP_STRIDE - (CMP_BLOCK - 1)
    dp = _toeplitz_offsets(PATTERN_LEN, 2 * N_CMP_PAD)[None, :]
    vp = _bias_of_rel(tab_rel, np.arange(CMP_STRIDE)[:, None] + off - CMP_STRIDE * dp)
    head = lambda h: (h // HPG, h % HPG, 0)
    b0, b1, cb = pl.pallas_call(
        _tables_kernel,
        grid=(N_HEADS,),
        in_specs=[pl.BlockSpec((1, 1, TOEPLITZ_LEN), lambda h: (h, 0, 0)),
                  pl.BlockSpec((1, 1, TOEPLITZ_LEN), lambda h: (h, 0, 0)),
                  pl.BlockSpec((1, CMP_STRIDE, PATTERN_LEN), lambda h: (h, 0, 0))],
        out_specs=[pl.BlockSpec((1, TQ, TQ), head), pl.BlockSpec((1, TQ, TQ), head),
                   pl.BlockSpec((1, nq, TQ, N_CMP_PAD), lambda h: (h // HPG, 0, h % HPG, 0))],
        out_shape=[jax.ShapeDtypeStruct((N_GROUPS, ROWS, TQ), F32),
                   jax.ShapeDtypeStruct((N_GROUPS, ROWS, TQ), F32),
                   jax.ShapeDtypeStruct((N_GROUPS, nq, ROWS, N_CMP_PAD), F32)],
        compiler_params=pltpu.CompilerParams(
            dimension_semantics=("arbitrary",), vmem_limit_bytes=VMEM_LIMIT),
        name="bias_tables",
    )(v0, v1, vp)
    i = np.arange(TQ)[:, None]
    jj = np.arange(TQ)[None, :]
    w2 = jnp.asarray(np.tile(np.where(jj > i, 0.0, NEG_INF).astype(np.float32), (HPG, 1)))
    n_sel = seq // SEL_BLOCK
    cs = np.arange(n_cmp) * CMP_STRIDE
    ce = cs + CMP_BLOCK - 1
    ss = np.arange(n_sel) * SEL_BLOCK
    ov = np.clip(np.minimum(ce[:, None], ss[None, :] + SEL_BLOCK - 1)
                 - np.maximum(cs[:, None], ss[None, :]) + 1, 0, None) / CMP_STRIDE
    ovlt = np.zeros((LANES, N_CMP_PAD), np.float32)
    ovlt[SEL_LANE0:SEL_LANE0 + n_sel, :n_cmp] = ov.T
    return tab_rel, cb, b0, b1, w2, jnp.asarray(ovlt, BF16)


def kernel(x, c, w_ada, b_ada, norm1_gain, w_in, q_gain, k_cmp_gain, k_sel_gain, k_win_gain,
           cmp_pos_k, cmp_pos_v, w_ck1, w_ck2, w_cv1, w_cv2, rel_bias_table, conv_w,
           attn_out_gain, conv_out_gain, w_out, norm2_gain, w_ff1, w_ff3, w_ff2):
    bsz, seq, d = x.shape
    assert d == D_MODEL and seq % TQ == 0 and SEL_LANE0 + seq // SEL_BLOCK <= SHIFT_LANE
    assert (seq - CMP_BLOCK) // CMP_STRIDE + 1 <= N_CMP_PAD and seq // CMP_STRIDE == N_CMP_PAD

    seg_of = lambda n: jnp.asarray(np.kron(np.eye(n), np.ones((HEAD_DIM, HEAD_DIM))) / HEAD_DIM, BF16)
    seg4 = seg_of(MXU_N // HEAD_DIM)
    tab_rel, cb, b0, b1, w2m, ovlt = _bias_tables(rel_bias_table, seq)
    bias_max = jnp.max(jnp.abs(tab_rel))
    two = lambda g: jnp.tile(g.reshape(1, HEAD_DIM), (1, 2))
    n_gate = N_BRANCH * HPG
    gate_pad = jnp.zeros((d, LANES - n_gate), BF16)

    for layer in range(w_in.shape[0]):
        wi = w_in[layer].astype(BF16)
        o_g = D_ATTN + 6 * D_KV
        kv = lambda n: wi[:, D_ATTN + n * D_KV:D_ATTN + (n + 1) * D_KV]
        w_p = jnp.concatenate(
            [wi[:, :D_ATTN], kv(0), kv(1), kv(2), kv(4), kv(3), kv(5),
             wi[:, o_g:o_g + n_gate], gate_pad,
             wi[:, o_g + n_gate:o_g + 2 * n_gate], gate_pad, wi[:, o_g + 2 * n_gate:]],
            axis=-1)
        w1k, w1v = w_ck1[layer].astype(BF16), w_cv1[layer].astype(BF16)
        w2k, w2v = w_ck2[layer].astype(BF16), w_cv2[layer].astype(BF16)
        pk2 = cmp_pos_k[layer].reshape(2, CMP_STRIDE * HEAD_DIM)
        pv2 = cmp_pos_v[layer].reshape(2, CMP_STRIDE * HEAD_DIM)

        gq_max = jnp.max(jnp.abs(q_gain[layer]))
        bound = lambda gk: (HEAD_DIM ** 0.5 * LOG2E * 1.01) * gq_max * jnp.max(jnp.abs(gk)) + bias_max
        c_sel, c_win = bound(k_sel_gain[layer]), bound(k_win_gain[layer])
        flag = (2.0 * jnp.maximum(c_sel, c_win) <= MAX_SHIFTED_RANGE).astype(jnp.int32).reshape(1)
        tails = jnp.zeros((2, HEAD_DIM), F32).at[:, SHIFT_LANE - HEAD_DIM].set(
            -jnp.stack([c_sel, c_win]))

        mod3 = _adaln(c, w_ada[layer], b_ada[layer]).reshape(bsz, 6, d)
        q, kvc, ks, vs, kw, vw, gate, bg, u = _inproj(
            x, mod3, norm1_gain[layer].reshape(1, d), w_p, seg4,
            jnp.tile(q_gain[layer].reshape(1, HEAD_DIM), (1, MXU_N // HEAD_DIM)),
            jnp.concatenate([two(k_sel_gain[layer]), two(k_win_gain[layer])], axis=-1), tails)
        kct, vc = _compress(kvc, w1k, w1v, w2k, w2v, pk2, pv2,
                            k_cmp_gain[layer].reshape(1, HEAD_DIM))
        y_attn = _attn(flag, q, kct, vc, ks, vs, kw, vw, gate, cb, b0, b1, w2m, ovlt)
        x = _outproj_ffn(y_attn, bg, u, conv_w[layer], attn_out_gain[layer].reshape(1, D_ATTN),
                         conv_out_gain[layer].reshape(1, D_CONV), w_out[layer].astype(BF16), x, mod3,
                         norm2_gain[layer].reshape(1, d), w_ff1[layer].astype(BF16),
                         w_ff3[layer].astype(BF16), w_ff2[layer].astype(BF16))
    return x
```

```python
import math

import numpy as np
import jax
import jax.numpy as jnp
from jax import lax
from jax.experimental import pallas as pl
from jax.experimental.pallas import tpu as pltpu

F32 = jnp.float32
BF16 = jnp.bfloat16

D_MODEL = 1024
HEAD_DIM = 64
N_HEADS = 8
N_GROUPS = 2
HPG = N_HEADS // N_GROUPS
D_ATTN = N_HEADS * HEAD_DIM
D_KV = N_GROUPS * HEAD_DIM
N_BRANCH = 3
D_CONV = D_MODEL - D_ATTN
CONV_WIDTH = 3
CMP_BLOCK = 32
CMP_STRIDE = 16
CMP_HIDDEN = 256
SEL_BLOCK = 64
SEL_TOPK = 16
N_LOCAL_FORCED = 2
WINDOW = 512
N_BUCKETS = 32
MAX_DISTANCE = 128
EPS = 1e-6
NEG_INF = -1e30
FORCED_SCORE = 1e6

LANES = 128
SUBLANES = 8
TQ = 256
ROWS = HPG * TQ
N_CMP_PAD = 128
SEL_LANE0 = 64
SHIFT_LANE = 96
V_LANES = 256
LOG2E = math.log2(math.e)
MAX_SHIFTED_RANGE = 100.0
VMEM_LIMIT = 56 * 1024 * 1024

_NT = (((1,), (1,)), ((), ()))


def _dot(a, b):
    return jnp.dot(a, b, preferred_element_type=F32)


def _silu(v):
    return v * (1.0 / (1.0 + jnp.exp(-v)))


def _sigmoid(v):
    return 1.0 / (1.0 + jnp.exp(-v))


def _seg_mean_sq(v, seg):
    sq = v * v
    hi = sq.astype(BF16)
    lo = (sq - hi.astype(F32)).astype(BF16)
    return _dot(hi, seg) + _dot(lo, seg)


TN_ADALN = 1536


def _adaln_kernel(c_ref, w_ref, b_ref, o_ref):
    sc = _silu(c_ref[...]).astype(BF16)
    o_ref[...] = _dot(sc, w_ref[...].astype(BF16)) + b_ref[...]


def _adaln(c, w_ada, b_ada):
    bsz, d = c.shape
    n = w_ada.shape[1]
    tn = TN_ADALN
    return pl.pallas_call(
        _adaln_kernel,
        grid=(n // tn,),
        in_specs=[pl.BlockSpec((bsz, d), lambda j: (0, 0)),
                  pl.BlockSpec((d, tn), lambda j: (0, j)),
                  pl.BlockSpec((1, tn), lambda j: (0, j))],
        out_specs=pl.BlockSpec((bsz, tn), lambda j: (0, j)),
        out_shape=jax.ShapeDtypeStruct((bsz, n), F32),
        compiler_params=pltpu.CompilerParams(
            dimension_semantics=("arbitrary",), vmem_limit_bytes=VMEM_LIMIT),
        name="adaln",
    )(c, w_ada, b_ada.reshape(1, n))


C_Q, C_KVC, C_KK, C_VV, C_GATE, C_BG, C_CG, C_XT, C_END = (
    0, 512, 768, 1024, 1280, 1536, 2048, 2560, 3072)
MXU_N = 256
TS_IN = 1024


def _inproj_kernel(x_ref, mod_ref, g1_ref, win_ref, seg_ref, gq_ref, gkk_ref, tail_ref,
                   q_ref, kvc_ref, ks_ref, vs_ref, kw_ref, vw_ref, gate_ref, bg_ref, u_ref, w_ref):
    ts = x_ref.shape[1]

    @pl.when((pl.program_id(0) == 0) & (pl.program_id(1) == 0))
    def _():
        n_gate = N_BRANCH * HPG
        o_g = D_ATTN + 6 * D_KV
        kv = lambda n: (D_ATTN + n * D_KV, D_ATTN + (n + 1) * D_KV)
        pieces = [(0, D_ATTN), kv(0), kv(1), kv(2), kv(4), kv(3), kv(5)]
        dst = 0
        for c0, c1 in pieces:
            w_ref[:, dst:dst + c1 - c0] = win_ref[:, c0:c1].astype(BF16)
            dst += c1 - c0
        zpad = jnp.zeros((win_ref.shape[0], LANES - n_gate), BF16)
        for g in range(N_GROUPS):
            w_ref[:, dst:dst + LANES] = jnp.concatenate(
                [win_ref[:, o_g + g * n_gate:o_g + (g + 1) * n_gate].astype(BF16), zpad], axis=-1)
            dst += LANES
        rest = o_g + N_GROUPS * n_gate
        for c0 in range(rest, win_ref.shape[1], MXU_N):
            w_ref[:, dst:dst + MXU_N] = win_ref[:, c0:c0 + MXU_N].astype(BF16)
            dst += MXU_N

    x = x_ref[0]
    ms = jnp.mean(x * x, axis=-1, keepdims=True)
    y = x * lax.rsqrt(ms + EPS) * g1_ref[...]
    h = y * (1.0 + mod_ref[0, 1:2, :]) + mod_ref[0, 0:1, :]
    hb = h.astype(BF16)
    seg = seg_ref[...]

    def proj(c0, c1):
        return _dot(hb, w_ref[:, c0:c1])

    lane = lax.broadcasted_iota(jnp.int32, (ts, HEAD_DIM), 1)
    ones192 = jnp.ones((ts, V_LANES - HEAD_DIM), F32)
    q_tail = jnp.where(lane == SHIFT_LANE - HEAD_DIM, 1.0, 0.0).astype(F32)

    heads_per_tile = MXU_N // HEAD_DIM
    row = pl.program_id(1) * ts + lax.broadcasted_iota(jnp.int32, (ts, HEAD_DIM), 0)
    onehot = jnp.where(lane == row // SEL_BLOCK, 1.0, 0.0).astype(F32) + tail_ref[0:1, :]
    win_tail = jnp.broadcast_to(tail_ref[1:2, :], (ts, HEAD_DIM))

    def do_q(c):
        v = proj(C_Q + c * MXU_N, C_Q + (c + 1) * MXU_N)
        vn = v * lax.rsqrt(_seg_mean_sq(v, seg) + EPS) * gq_ref[...] * (HEAD_DIM ** -0.5 * LOG2E)
        for hh in range(heads_per_tile):
            qh = jnp.concatenate([vn[:, hh * HEAD_DIM:(hh + 1) * HEAD_DIM], q_tail], axis=-1)
            q_ref[0, c * heads_per_tile + hh] = qh.astype(BF16)

    def do_kvc():
        v = proj(C_KVC, C_KK)
        kvc_ref[0, 0] = v[:, :D_KV]
        kvc_ref[0, 1] = v[:, D_KV:]

    def do_kk():
        v = proj(C_KK, C_VV)
        vn = v * lax.rsqrt(_seg_mean_sq(v, seg) + EPS) * gkk_ref[...]
        for g in range(N_GROUPS):
            ks_ref[0, g] = jnp.concatenate(
                [vn[:, g * HEAD_DIM:(g + 1) * HEAD_DIM], onehot], axis=-1).astype(BF16)
            kw_ref[0, g] = jnp.concatenate(
                [vn[:, D_KV + g * HEAD_DIM:D_KV + (g + 1) * HEAD_DIM], win_tail], axis=-1).astype(BF16)

    def do_vv():
        v = proj(C_VV, C_GATE)
        for g in range(N_GROUPS):
            vs_ref[0, g] = jnp.concatenate(
                [v[:, g * HEAD_DIM:(g + 1) * HEAD_DIM], ones192], axis=-1).astype(BF16)
            vw_ref[0, g] = jnp.concatenate(
                [v[:, D_KV + g * HEAD_DIM:D_KV + (g + 1) * HEAD_DIM], ones192], axis=-1).astype(BF16)

    def do_gate():
        v = _sigmoid(proj(C_GATE, C_BG))
        for g in range(N_GROUPS):
            gate_ref[0, g] = v[:, g * LANES:(g + 1) * LANES]

    def do_bg(c):
        cols = slice(c * MXU_N, (c + 1) * MXU_N)
        bg_ref[0, :, cols] = proj(C_BG + c * MXU_N, C_BG + (c + 1) * MXU_N).astype(BF16)

    def do_u(c):
        cols = slice(c * MXU_N, (c + 1) * MXU_N)
        u_ref[0, :, cols] = (proj(C_CG + c * MXU_N, C_CG + (c + 1) * MXU_N)
                             * proj(C_XT + c * MXU_N, C_XT + (c + 1) * MXU_N)).astype(BF16)

    do_q(0)
    do_bg(0)
    do_q(1)
    do_bg(1)
    do_kvc()
    do_u(0)
    do_kk()
    do_u(1)
    do_gate()
    do_vv()


def _inproj(x, mod3, g1, w_in, seg, gq4, gkk4, tails):
    bsz, seq, d = x.shape
    ts = TS_IN
    const2 = lambda b, i: (0, 0)
    k_shape = jax.ShapeDtypeStruct((bsz, N_GROUPS, seq, LANES), BF16)
    k_spec = pl.BlockSpec((1, N_GROUPS, ts, LANES), lambda b, i: (b, 0, i, 0))
    v_shape = jax.ShapeDtypeStruct((bsz, N_GROUPS, seq, V_LANES), BF16)
    v_spec = pl.BlockSpec((1, N_GROUPS, ts, V_LANES), lambda b, i: (b, 0, i, 0))
    return pl.pallas_call(
        _inproj_kernel,
        grid=(bsz, seq // ts),
        in_specs=[pl.BlockSpec((1, ts, d), lambda b, i: (b, i, 0)),
                  pl.BlockSpec((1, 6, d), lambda b, i: (b, 0, 0)),
                  pl.BlockSpec((1, d), const2),
                  pl.BlockSpec(w_in.shape, const2, pipeline_mode=pl.Buffered(1)),
                  pl.BlockSpec(seg.shape, const2),
                  pl.BlockSpec(gq4.shape, const2),
                  pl.BlockSpec(gkk4.shape, const2),
                  pl.BlockSpec(tails.shape, const2)],
        out_specs=[pl.BlockSpec((1, N_HEADS, ts, LANES), lambda b, i: (b, 0, i, 0)),
                   pl.BlockSpec((1, 2, ts, D_KV), lambda b, i: (b, 0, i, 0)),
                   k_spec, v_spec, k_spec, v_spec,
                   pl.BlockSpec((1, N_GROUPS, ts, LANES), lambda b, i: (b, 0, i, 0)),
                   pl.BlockSpec((1, ts, D_CONV), lambda b, i: (b, i, 0)),
                   pl.BlockSpec((1, ts, D_CONV), lambda b, i: (b, i, 0))],
        out_shape=[jax.ShapeDtypeStruct((bsz, N_HEADS, seq, LANES), BF16),
                   jax.ShapeDtypeStruct((bsz, 2, seq, D_KV), F32),
                   k_shape, v_shape, k_shape, v_shape,
                   jax.ShapeDtypeStruct((bsz, N_GROUPS, seq, LANES), F32),
                   jax.ShapeDtypeStruct((bsz, seq, D_CONV), BF16),
                   jax.ShapeDtypeStruct((bsz, seq, D_CONV), BF16)],
        compiler_params=pltpu.CompilerParams(
            dimension_semantics=("arbitrary", "arbitrary"), vmem_limit_bytes=VMEM_LIMIT),
        scratch_shapes=[pltpu.VMEM((d, C_END), BF16)],
        name="inproj",
    )(x, mod3, g1, w_in, seg, gq4, gkk4, tails)


def _compress_kernel(kvc_ref, w1k_ref, w1v_ref, w2k_ref, w2v_ref, pk_ref, pv_ref, gk_ref,
                     kct_ref, vc_ref):
    n_rows = kvc_ref.shape[2] // CMP_STRIDE
    half = CMP_STRIDE * HEAD_DIM
    zlane = jnp.zeros((n_rows, HEAD_DIM), F32)

    def branch(which, w1_ref, w2_ref, pos_ref):
        rows = [kvc_ref[0, which, pl.ds(r, n_rows, stride=CMP_STRIDE), :] for r in range(CMP_STRIDE)]
        outs = []
        for g in range(N_GROUPS):
            x = jnp.concatenate([rw[:, g * HEAD_DIM:(g + 1) * HEAD_DIM] for rw in rows], axis=-1)
            a1 = (x + pos_ref[0:1, :]).astype(BF16)
            a2 = (x + pos_ref[1:2, :]).astype(BF16)
            hid = _dot(a1, w1_ref[0:half, :]) + pltpu.roll(_dot(a2, w1_ref[half:, :]), n_rows - 1, 0)
            outs.append(_dot(_silu(hid).astype(BF16), w2_ref[...]))
        return outs

    kc = branch(0, w1k_ref, w2k_ref, pk_ref)
    vc = branch(1, w1v_ref, w2v_ref, pv_ref)
    for g in range(N_GROUPS):
        kn = kc[g] * lax.rsqrt(jnp.mean(kc[g] * kc[g], axis=-1, keepdims=True) + EPS) * gk_ref[...]
        kct_ref[0, g] = jnp.concatenate([kn, zlane], axis=-1).T.astype(BF16)
        vc_ref[0, g] = jnp.concatenate([vc[g], zlane], axis=-1).astype(BF16)


def _compress(kvc, w1k, w1v, w2k, w2v, pk2, pv2, gk):
    bsz, _, seq, _ = kvc.shape
    c2 = lambda b: (0, 0)
    out_shape = jax.ShapeDtypeStruct((bsz, N_GROUPS, N_CMP_PAD, LANES), BF16)
    out_spec = pl.BlockSpec((1, N_GROUPS, N_CMP_PAD, LANES), lambda b: (b, 0, 0, 0))
    return pl.pallas_call(
        _compress_kernel,
        grid=(bsz,),
        in_specs=[pl.BlockSpec((1, 2, seq, D_KV), lambda b: (b, 0, 0, 0)),
                  pl.BlockSpec(w1k.shape, c2), pl.BlockSpec(w1v.shape, c2),
                  pl.BlockSpec(w2k.shape, c2), pl.BlockSpec(w2v.shape, c2),
                  pl.BlockSpec(pk2.shape, c2), pl.BlockSpec(pv2.shape, c2),
                  pl.BlockSpec(gk.shape, c2)],
        out_specs=[out_spec, out_spec],
        out_shape=[out_shape, out_shape],
        compiler_params=pltpu.CompilerParams(
            dimension_semantics=("arbitrary",), vmem_limit_bytes=VMEM_LIMIT),
        name="compress",
    )(kvc, w1k, w1v, w2k, w2v, pk2, pv2, gk)


def _logits(qa, k_rows, bias):
    s = lax.dot_general(qa, k_rows, _NT, preferred_element_type=F32)
    return s if bias is None else s + bias


def _shifted_pv(qa, k_rows, v_rows, bias, keep=None):
    if keep is not None:
        v_rows = v_rows * keep.astype(BF16)
    return _dot(jnp.exp2(_logits(qa, k_rows, bias)).astype(BF16), v_rows)


def _online_update(qa, k_tile, v_tile, bias, acc_ref, m_ref):
    s = _logits(qa, k_tile, bias)
    m_prev = m_ref[...]
    m_new = jnp.maximum(m_prev, jnp.max(s, axis=-1, keepdims=True))
    alpha = jnp.exp2(m_prev - m_new)
    p = jnp.exp2(s - jnp.concatenate([m_new] * (TQ // LANES), axis=-1))
    acc_ref[...] = (acc_ref[...] * jnp.concatenate([alpha] * (V_LANES // LANES), axis=-1)
                    + _dot(p.astype(BF16), v_tile))
    m_ref[...] = m_new


def _selection_bias(pc, qi, ovlt, n_sel):
    psum = pc[0:TQ] + pc[TQ:2 * TQ] + pc[2 * TQ:3 * TQ] + pc[3 * TQ:4 * TQ]
    hi = psum.astype(BF16)
    lo = (psum - hi.astype(F32)).astype(BF16)
    pslc_t = (lax.dot_general(ovlt, hi, _NT, preferred_element_type=F32)
              + lax.dot_general(ovlt, lo, _NT, preferred_element_type=F32))
    top_k = min(SEL_TOPK, n_sel)
    sub = SUBLANES
    j = lax.broadcasted_iota(jnp.int32, (n_sel, TQ), 0)
    t = qi * TQ + lax.broadcasted_iota(jnp.int32, (n_sel, TQ), 1)
    dist = jnp.right_shift(t, int(math.log2(SEL_BLOCK))) - j
    score = jnp.where(dist < N_LOCAL_FORCED, FORCED_SCORE, pslc_t[SEL_LANE0:SEL_LANE0 + n_sel, :])
    score = jnp.where(j == 0, FORCED_SCORE, score)
    score = jnp.where(dist >= 0, score, NEG_INF)
    groups = [score[a * sub:(a + 1) * sub] for a in range(n_sel // sub)]
    cnts = [jnp.zeros((sub, TQ), F32) for _ in groups]
    j_sub = lax.broadcasted_iota(jnp.int32, (sub, TQ), 0)
    for i in range(n_sel):
        row = jnp.broadcast_to(score[i:i + 1, :], (sub, TQ))
        for a, grp in enumerate(groups):
            if a * sub > i:
                beats = jnp.where(row >= grp, 1.0, 0.0)
            elif a * sub + sub - 1 <= i:
                beats = jnp.where(row > grp, 1.0, 0.0)
            else:
                beats = jnp.where(j_sub + a * sub > i, jnp.where(row >= grp, 1.0, 0.0),
                                  jnp.where(row > grp, 1.0, 0.0))
            cnts[a] = cnts[a] + beats
    sel_t = jnp.where(jnp.concatenate(cnts, axis=0) < top_k, 0.0, NEG_INF)
    return jnp.concatenate(
        [jnp.zeros((SEL_LANE0, TQ), F32), sel_t,
         jnp.zeros((LANES - SEL_LANE0 - n_sel, TQ), F32)], axis=0).T


def _attn_step(shifted, q_ref, kct_ref, vc_ref, ks_ref, vs_ref, kw_ref, vw_ref, gate_ref,
               cb_ref, b0_ref, b1_ref, w2_ref, ovlt_ref, o_ref, qa_ref, acc_s, m_s, acc_w, m_w):
    qi = pl.program_id(1)
    groups = range(N_GROUPS)
    q = [q_ref[0, g * HPG:(g + 1) * HPG].reshape(ROWS, LANES) for g in groups]

    def key_rows(ref, g, kt, n_tiles=1):
        return ref[0, g, pl.ds(pl.multiple_of(kt * TQ, TQ), n_tiles * TQ), :]

    def compressed_and_selection(which=groups):
        o_c = []
        for g in which:
            s = _dot(q[g], kct_ref[0, g]) + cb_ref[g, 0]
            m = jnp.max(s, axis=-1, keepdims=True)
            e = jnp.exp2(s - m)
            l = jnp.sum(e, axis=-1, keepdims=True)
            pc = e * jnp.where(m > 0.5 * NEG_INF, 1.0 / l, 0.0)
            o_c.append(_dot(pc.astype(BF16), vc_ref[0, g]))
            selbias = _selection_bias(pc, qi, ovlt_ref[...], ks_ref.shape[2] // SEL_BLOCK)
            for h in range(HPG):
                qa_ref[g, h * TQ:(h + 1) * TQ, :] = (
                    q_ref[0, g * HPG + h].astype(F32) + selbias).astype(BF16)
        return o_c

    kt1 = jnp.maximum(qi - 1, 0)
    kt2 = jnp.maximum(qi - 2, 0)
    if shifted:
        has1 = jnp.where(qi >= 1, 1.0, 0.0)
        has2 = jnp.where(qi >= 2, 1.0, 0.0)
        a_w, o_c = [], []
        for g in groups:
            a_w.append(
                _shifted_pv(q[g], key_rows(kw_ref, g, kt2), key_rows(vw_ref, g, kt2), w2_ref[...], has2)
                + _shifted_pv(q[g], key_rows(kw_ref, g, kt1), key_rows(vw_ref, g, kt1), b1_ref[g], has1)
                + _shifted_pv(q[g], key_rows(kw_ref, g, qi), key_rows(vw_ref, g, qi), b0_ref[g]))
            o_c += compressed_and_selection([g])
        acc_s[...] = jnp.zeros(acc_s.shape, F32)
        n_plain = jnp.maximum(qi - 1, 0)

        def pair_body(it, carry):
            pv = [None] * N_GROUPS
            for half in range(2):
                for g in groups:
                    t = _shifted_pv(qa_ref[g], key_rows(ks_ref, g, 2 * it + half),
                                    key_rows(vs_ref, g, 2 * it + half), None)
                    pv[g] = t if pv[g] is None else pv[g] + t
            for g in groups:
                acc_s[g] += pv[g]
            return carry

        lax.fori_loop(0, n_plain // 2, pair_body, 0)
        odd = jnp.where(n_plain % 2 == 1, 1.0, 0.0)
        a_s = [acc_s[g]
               + _shifted_pv(qa_ref[g], key_rows(ks_ref, g, kt2), key_rows(vs_ref, g, kt2), None, odd)
               + _shifted_pv(qa_ref[g], key_rows(ks_ref, g, kt1), key_rows(vs_ref, g, kt1), b1_ref[g], has1)
               + _shifted_pv(qa_ref[g], key_rows(ks_ref, g, qi), key_rows(vs_ref, g, qi), b0_ref[g])
               for g in groups]
    else:
        o_c = compressed_and_selection()
        a_s, a_w = [], []
        for g in groups:
            qa = qa_ref[g]
            sel = (acc_s.at[g], m_s.at[g])
            win = (acc_w.at[g], m_w.at[g])
            for acc, m_ref in (sel, win):
                acc[...] = jnp.zeros(acc.shape, F32)
                m_ref[...] = jnp.full(m_ref.shape, NEG_INF, F32)

            def sel_body(kt, carry, g=g, qa=qa, sel=sel):
                _online_update(qa, key_rows(ks_ref, g, kt), key_rows(vs_ref, g, kt), None, *sel)
                return carry

            lax.fori_loop(0, qi - 1, sel_body, 0)

            @pl.when(qi >= 2)
            def _(g=g, win=win):
                _online_update(q[g], key_rows(kw_ref, g, kt2), key_rows(vw_ref, g, kt2), w2_ref[...], *win)

            @pl.when(qi >= 1)
            def _(g=g, qa=qa, sel=sel, win=win):
                _online_update(qa, key_rows(ks_ref, g, kt1), key_rows(vs_ref, g, kt1), b1_ref[g], *sel)
                _online_update(q[g], key_rows(kw_ref, g, kt1), key_rows(vw_ref, g, kt1), b1_ref[g], *win)

            _online_update(qa, key_rows(ks_ref, g, qi), key_rows(vs_ref, g, qi), b0_ref[g], *sel)
            _online_update(q[g], key_rows(kw_ref, g, qi), key_rows(vw_ref, g, qi), b0_ref[g], *win)
            a_s.append(acc_s[g])
            a_w.append(acc_w[g])

    outs = []
    for g in groups:
        o_s = a_s[g][:, :LANES] / a_s[g][:, LANES:]
        o_w = a_w[g][:, :LANES] / a_w[g][:, LANES:]
        gates = gate_ref[0, g]
        for h in range(HPG):
            rows = slice(h * TQ, (h + 1) * TQ)
            g_c = gates[:, h * N_BRANCH + 0:h * N_BRANCH + 1]
            g_s = gates[:, h * N_BRANCH + 1:h * N_BRANCH + 2]
            g_w = gates[:, h * N_BRANCH + 2:h * N_BRANCH + 3]
            o_h = g_c * o_c[g][rows] + g_s * o_s[rows] + g_w * o_w[rows]
            outs.append(o_h[:, :HEAD_DIM])
    o_ref[0] = jnp.concatenate(outs, axis=-1)


def _attn_kernel(flag_ref, *refs):
    use_shift = flag_ref[0] == 1

    @pl.when(use_shift)
    def _():
        _attn_step(True, *refs)

    @pl.when(jnp.logical_not(use_shift))
    def _():
        _attn_step(False, *refs)


def _attn(flag, q, kct, vc, ks, vs, kw, vw, gate, cb, b0, b1, w2, ovlt):
    bsz, _, seq, _ = q.shape
    nq = seq // TQ
    per_batch = lambda b, i: (b, 0, 0, 0)
    k_spec = pl.BlockSpec((1, N_GROUPS, seq, LANES), per_batch)
    v_spec = pl.BlockSpec((1, N_GROUPS, seq, V_LANES), per_batch)
    cmp_spec = pl.BlockSpec((1, N_GROUPS, N_CMP_PAD, LANES), per_batch)
    const = lambda shape: pl.BlockSpec(shape, lambda b, i: (0,) * len(shape),
                                       pipeline_mode=pl.Buffered(1))
    return pl.pallas_call(
        _attn_kernel,
        grid=(bsz, nq),
        in_specs=[pl.BlockSpec(memory_space=pltpu.SMEM),
                  pl.BlockSpec((1, N_HEADS, TQ, LANES), lambda b, i: (b, 0, i, 0)),
                  cmp_spec, cmp_spec, k_spec, v_spec, k_spec, v_spec,
                  pl.BlockSpec((1, N_GROUPS, TQ, LANES), lambda b, i: (b, 0, i, 0)),
                  pl.BlockSpec((N_GROUPS, 1, ROWS, N_CMP_PAD), lambda b, i: (0, i, 0, 0)),
                  const(b0.shape), const(b1.shape), const(w2.shape), const(ovlt.shape)],
        out_specs=pl.BlockSpec((1, TQ, D_ATTN), lambda b, i: (b, i, 0)),
        out_shape=jax.ShapeDtypeStruct((bsz, seq, D_ATTN), F32),
        scratch_shapes=[pltpu.VMEM((N_GROUPS, ROWS, LANES), BF16),
                        pltpu.VMEM((N_GROUPS, ROWS, V_LANES), F32),
                        pltpu.VMEM((N_GROUPS, ROWS, LANES), F32),
                        pltpu.VMEM((N_GROUPS, ROWS, V_LANES), F32),
                        pltpu.VMEM((N_GROUPS, ROWS, LANES), F32)],
        compiler_params=pltpu.CompilerParams(
            dimension_semantics=("arbitrary", "arbitrary"),
            vmem_limit_bytes=VMEM_LIMIT),
        name="attn",
    )(flag, q, kct, vc, ks, vs, kw, vw, gate, cb, b0, b1, w2, ovlt)


TS_OUT = 512
HALO = 16


def _outproj_ffn_kernel(ya_ref, bg_ref, u_ref, uh_ref, cw_ref, ga_ref, gc_ref, w_ref, x_ref, mod_ref,
                        g2_ref, w1_ref, w3_ref, w2_ref, o_ref):
    u = u_ref[0].astype(F32)
    halo = uh_ref[0].astype(F32) * jnp.where(pl.program_id(1) > 0, 1.0, 0.0)
    h1 = halo[HALO - 1:HALO, :]
    h2 = halo[HALO - 2:HALO - 1, :]
    row = lax.broadcasted_iota(jnp.int32, u.shape, 0)
    u1 = jnp.where(row == 0, h1, pltpu.roll(u, 1, 0))
    u2 = jnp.where(row == 0, h2, jnp.where(row == 1, h1, pltpu.roll(u, 2, 0)))
    conv = u2 * cw_ref[0:1, :] + u1 * cw_ref[1:2, :] + u * cw_ref[2:3, :]
    yc = bg_ref[0].astype(F32) * conv
    yc = yc * lax.rsqrt(jnp.mean(yc * yc, axis=-1, keepdims=True) + EPS) * gc_ref[...]
    ya = ya_ref[0]
    ya = ya * lax.rsqrt(jnp.mean(ya * ya, axis=-1, keepdims=True) + EPS) * ga_ref[...]
    proj = _dot(ya.astype(BF16), w_ref[0:D_ATTN, :]) + _dot(yc.astype(BF16), w_ref[D_ATTN:, :])
    x = x_ref[0] + mod_ref[0, 2:3, :] * proj

    ms = jnp.mean(x * x, axis=-1, keepdims=True)
    h = x * lax.rsqrt(ms + EPS) * g2_ref[...] * (1.0 + mod_ref[0, 4:5, :]) + mod_ref[0, 3:4, :]
    hb = h.astype(BF16)
    a = _dot(hb, w1_ref[...])
    b = _dot(hb, w3_ref[...])
    act = (_silu(a) * b).astype(BF16)
    o_ref[0] = x + mod_ref[0, 5:6, :] * _dot(act, w2_ref[...])


def _outproj_ffn(ya, bg, u, conv_w, ga, gc, w_out, x, mod3, g2, w1, w3, w2):
    bsz, seq, d = x.shape
    ts = TS_OUT
    c2 = lambda b, i: (0, 0)
    row_spec = lambda width: pl.BlockSpec((1, ts, width), lambda b, i: (b, i, 0))
    wspec = lambda w: pl.BlockSpec(w.shape, c2, pipeline_mode=pl.Buffered(1))
    return pl.pallas_call(
        _outproj_ffn_kernel,
        grid=(bsz, seq // ts),
        in_specs=[row_spec(D_ATTN), row_spec(D_CONV), row_spec(D_CONV),
                  pl.BlockSpec((1, HALO, D_CONV),
                               lambda b, i: (b, jnp.maximum(i * (ts // HALO) - 1, 0), 0)),
                  pl.BlockSpec(conv_w.shape, c2),
                  pl.BlockSpec((1, D_ATTN), c2), pl.BlockSpec((1, D_CONV), c2),
                  wspec(w_out),
                  row_spec(d),
                  pl.BlockSpec((1, 6, d), lambda b, i: (b, 0, 0)),
                  pl.BlockSpec((1, d), c2),
                  wspec(w1), wspec(w3), wspec(w2)],
        out_specs=row_spec(d),
        out_shape=jax.ShapeDtypeStruct((bsz, seq, d), F32),
        compiler_params=pltpu.CompilerParams(
            dimension_semantics=("arbitrary", "arbitrary"), vmem_limit_bytes=VMEM_LIMIT),
        name="outproj_ffn",
    )(ya, bg, u, u, conv_w, ga, gc, w_out, x, mod3, g2, w1, w3, w2)


def _t5_bucket_np(rel):
    n = np.maximum(rel, 0)
    max_exact = N_BUCKETS // 2
    nf = np.maximum(n, 1).astype(np.float32)
    large = max_exact + (np.log(nf / max_exact) / math.log(MAX_DISTANCE / max_exact)
                         * (N_BUCKETS - max_exact)).astype(np.int32)
    return np.where(n < max_exact, n, np.minimum(large, N_BUCKETS - 1)).astype(np.int32)


def _bias_of_rel(tab_rel, rel):
    idx = jnp.asarray(_t5_bucket_np(rel))
    tab_b = tab_rel.T.reshape((N_HEADS, N_BUCKETS) + (1,) * idx.ndim)
    vals = jnp.zeros((N_HEADS,) + idx.shape, F32)
    for k in range(N_BUCKETS):
        vals = jnp.where(idx == k, tab_b[:, k], vals)
    return jnp.where(jnp.asarray(rel >= 0), vals, NEG_INF)


def _toeplitz_offsets(length, n_cols):
    k = np.arange(length)
    return np.where(k < n_cols, k, k - length)


TOEPLITZ_LEN = 2 * TQ
PATTERN_LEN = 3 * LANES


def _tables_kernel(v0_ref, v1_ref, vp_ref, b0_ref, b1_ref, cb_ref):
    def toeplitz_tile(v_ref):
        rows = jnp.broadcast_to(v_ref[0], (TQ, TOEPLITZ_LEN))
        return pltpu.roll(rows, 0, 1, stride=1, stride_axis=0)[:, :TQ]

    b0_ref[0] = toeplitz_tile(v0_ref)
    b1_ref[0] = toeplitz_tile(v1_ref)
    vp = vp_ref[0]
    shift = TQ // CMP_STRIDE
    pat = jnp.concatenate(
        [vp[:, :2 * N_CMP_PAD]] + [pltpu.roll(vp, a, 1)[:, :2 * N_CMP_PAD] for a in range(1, shift)],
        axis=0)
    for qi in range(cb_ref.shape[1]):
        cb_ref[0, qi] = pat[:, N_CMP_PAD - qi * shift:2 * N_CMP_PAD - qi * shift]


def _bias_tables(rel_bias_table, seq):
    tab_rel = (rel_bias_table - rel_bias_table[N_BUCKETS - 1:N_BUCKETS, :]) * LOG2E
    nq = seq // TQ
    n_cmp = (seq - CMP_BLOCK) // CMP_STRIDE + 1
    shift = TQ // CMP_STRIDE
    assert nq * shift <= N_CMP_PAD and n_cmp * CMP_STRIDE + CMP_BLOCK - 1 > seq
    assert shift + 2 * N_CMP_PAD - 1 <= PATTERN_LEN
    d = _toeplitz_offsets(TOEPLITZ_LEN, TQ)
    v0 = _bias_of_rel(tab_rel, -d)[:, None, :]
    v1 = _bias_of_rel(tab_rel, TQ - d)[:, None, :]
    off = N_CMP_PAD * CMP_STRIDE - (CMP_BLOCK - 1)
    dp = _toeplitz_offsets(PATTERN_LEN, 2 * N_CMP_PAD)[None, :]
    vp = _bias_of_rel(tab_rel, np.arange(CMP_STRIDE)[:, None] + off - CMP_STRIDE * dp)
    head = lambda h: (h // HPG, h % HPG, 0)
    b0, b1, cb = pl.pallas_call(
        _tables_kernel,
        grid=(N_HEADS,),
        in_specs=[pl.BlockSpec((1, 1, TOEPLITZ_LEN), lambda h: (h, 0, 0)),
                  pl.BlockSpec((1, 1, TOEPLITZ_LEN), lambda h: (h, 0, 0)),
                  pl.BlockSpec((1, CMP_STRIDE, PATTERN_LEN), lambda h: (h, 0, 0))],
        out_specs=[pl.BlockSpec((1, TQ, TQ), head), pl.BlockSpec((1, TQ, TQ), head),
                   pl.BlockSpec((1, nq, TQ, N_CMP_PAD), lambda h: (h // HPG, 0, h % HPG, 0))],
        out_shape=[jax.ShapeDtypeStruct((N_GROUPS, ROWS, TQ), F32),
                   jax.ShapeDtypeStruct((N_GROUPS, ROWS, TQ), F32),
                   jax.ShapeDtypeStruct((N_GROUPS, nq, ROWS, N_CMP_PAD), F32)],
        compiler_params=pltpu.CompilerParams(
            dimension_semantics=("arbitrary",), vmem_limit_bytes=VMEM_LIMIT),
        name="bias_tables",
    )(v0, v1, vp)
    i = np.arange(TQ)[:, None]
    jj = np.arange(TQ)[None, :]
    w2 = jnp.asarray(np.tile(np.where(jj > i, 0.0, NEG_INF).astype(np.float32), (HPG, 1)))
    n_sel = seq // SEL_BLOCK
    cs = np.arange(n_cmp) * CMP_STRIDE
    ce = cs + CMP_BLOCK - 1
    ss = np.arange(n_sel) * SEL_BLOCK
    ov = np.clip(np.minimum(ce[:, None], ss[None, :] + SEL_BLOCK - 1)
                 - np.maximum(cs[:, None], ss[None, :]) + 1, 0, None) / CMP_STRIDE
    ovlt = np.zeros((LANES, N_CMP_PAD), np.float32)
    ovlt[SEL_LANE0:SEL_LANE0 + n_sel, :n_cmp] = ov.T
    return tab_rel, cb, b0, b1, w2, jnp.asarray(ovlt, BF16)


def kernel(x, c, w_ada, b_ada, norm1_gain, w_in, q_gain, k_cmp_gain, k_sel_gain, k_win_gain,
           cmp_pos_k, cmp_pos_v, w_ck1, w_ck2, w_cv1, w_cv2, rel_bias_table, conv_w,
           attn_out_gain, conv_out_gain, w_out, norm2_gain, w_ff1, w_ff3, w_ff2):
    bsz, seq, d = x.shape
    assert d == D_MODEL and seq % TQ == 0 and SEL_LANE0 + seq // SEL_BLOCK <= SHIFT_LANE
    assert (seq - CMP_BLOCK) // CMP_STRIDE + 1 <= N_CMP_PAD and seq // CMP_STRIDE == N_CMP_PAD

    seg_of = lambda n: jnp.asarray(np.kron(np.eye(n), np.ones((HEAD_DIM, HEAD_DIM))) / HEAD_DIM, BF16)
    seg4 = seg_of(MXU_N // HEAD_DIM)
    tab_rel, cb, b0, b1, w2m, ovlt = _bias_tables(rel_bias_table, seq)
    bias_max = jnp.max(jnp.abs(tab_rel))
    two = lambda g: jnp.tile(g.reshape(1, HEAD_DIM), (1, 2))

    for layer in range(w_in.shape[0]):
        w1k, w1v = w_ck1[layer].astype(BF16), w_cv1[layer].astype(BF16)
        w2k, w2v = w_ck2[layer].astype(BF16), w_cv2[layer].astype(BF16)
        pk2 = cmp_pos_k[layer].reshape(2, CMP_STRIDE * HEAD_DIM)
        pv2 = cmp_pos_v[layer].reshape(2, CMP_STRIDE * HEAD_DIM)

        gq_max = jnp.max(jnp.abs(q_gain[layer]))
        bound = lambda gk: (HEAD_DIM ** 0.5 * LOG2E * 1.01) * gq_max * jnp.max(jnp.abs(gk)) + bias_max
        c_sel, c_win = bound(k_sel_gain[layer]), bound(k_win_gain[layer])
        flag = (2.0 * jnp.maximum(c_sel, c_win) <= MAX_SHIFTED_RANGE).astype(jnp.int32).reshape(1)
        tails = jnp.zeros((2, HEAD_DIM), F32).at[:, SHIFT_LANE - HEAD_DIM].set(
            -jnp.stack([c_sel, c_win]))

        mod3 = _adaln(c, w_ada[layer], b_ada[layer]).reshape(bsz, 6, d)
        q, kvc, ks, vs, kw, vw, gate, bg, u = _inproj(
            x, mod3, norm1_gain[layer].reshape(1, d), w_in[layer], seg4,
            jnp.tile(q_gain[layer].reshape(1, HEAD_DIM), (1, MXU_N // HEAD_DIM)),
            jnp.concatenate([two(k_sel_gain[layer]), two(k_win_gain[layer])], axis=-1), tails)
        kct, vc = _compress(kvc, w1k, w1v, w2k, w2v, pk2, pv2,
                            k_cmp_gain[layer].reshape(1, HEAD_DIM))
        y_attn = _attn(flag, q, kct, vc, ks, vs, kw, vw, gate, cb, b0, b1, w2m, ovlt)
        x = _outproj_ffn(y_attn, bg, u, conv_w[layer], attn_out_gain[layer].reshape(1, D_ATTN),
                         conv_out_gain[layer].reshape(1, D_CONV), w_out[layer].astype(BF16), x, mod3,
                         norm2_gain[layer].reshape(1, d), w_ff1[layer].astype(BF16),
                         w_ff3[layer].astype(BF16), w_ff2[layer].astype(BF16))
    return x
```

```python
import math

import numpy as np
import jax
import jax.numpy as jnp
from jax import lax
from jax.experimental import pallas as pl
from jax.experimental.pallas import tpu as pltpu

F32 = jnp.float32
BF16 = jnp.bfloat16

D_MODEL = 1024
HEAD_DIM = 64
N_HEADS = 8
N_GROUPS = 2
HPG = N_HEADS // N_GROUPS
D_ATTN = N_HEADS * HEAD_DIM
D_KV = N_GROUPS * HEAD_DIM
N_BRANCH = 3
D_CONV = D_MODEL - D_ATTN
CONV_WIDTH = 3
CMP_BLOCK = 32
CMP_STRIDE = 16
CMP_HIDDEN = 256
SEL_BLOCK = 64
SEL_TOPK = 16
N_LOCAL_FORCED = 2
WINDOW = 512
N_BUCKETS = 32
MAX_DISTANCE = 128
EPS = 1e-6
NEG_INF = -1e30
FORCED_SCORE = 1e6

LANES = 128
SUBLANES = 8
TQ = 256
ROWS = HPG * TQ
N_CMP_PAD = 128
SEL_LANE0 = 64
SHIFT_LANE = 96
V_LANES = 256
LOG2E = math.log2(math.e)
MAX_SHIFTED_RANGE = 100.0
VMEM_LIMIT = 56 * 1024 * 1024

_NT = (((1,), (1,)), ((), ()))


def _dot(a, b):
    return jnp.dot(a, b, preferred_element_type=F32)


def _silu(v):
    return v * (1.0 / (1.0 + jnp.exp(-v)))


def _sigmoid(v):
    return 1.0 / (1.0 + jnp.exp(-v))


def _seg_mean_sq(v, seg):
    sq = v * v
    hi = sq.astype(BF16)
    lo = (sq - hi.astype(F32)).astype(BF16)
    return _dot(hi, seg) + _dot(lo, seg)


TN_ADALN = 1536


def _adaln_kernel(c_ref, w_ref, b_ref, o_ref):
    sc = _silu(c_ref[...]).astype(BF16)
    o_ref[...] = _dot(sc, w_ref[...].astype(BF16)) + b_ref[...]


def _adaln(c, w_ada, b_ada):
    bsz, d = c.shape
    n = w_ada.shape[1]
    tn = TN_ADALN
    return pl.pallas_call(
        _adaln_kernel,
        grid=(n // tn,),
        in_specs=[pl.BlockSpec((bsz, d), lambda j: (0, 0)),
                  pl.BlockSpec((d, tn), lambda j: (0, j)),
                  pl.BlockSpec((1, tn), lambda j: (0, j))],
        out_specs=pl.BlockSpec((bsz, tn), lambda j: (0, j)),
        out_shape=jax.ShapeDtypeStruct((bsz, n), F32),
        compiler_params=pltpu.CompilerParams(
            dimension_semantics=("arbitrary",), vmem_limit_bytes=VMEM_LIMIT),
        name="adaln",
    )(c, w_ada, b_ada.reshape(1, n))


C_Q, C_KVC, C_KK, C_VV, C_GATE, C_BG, C_CG, C_XT, C_END = (
    0, 512, 768, 1024, 1280, 1536, 2048, 2560, 3072)
MXU_N = 256
TS_IN = 1024


def _inproj_kernel(x_ref, mod_ref, g1_ref, w_ref, seg_ref, gq_ref, gkk_ref, tail_ref,
                   q_ref, kvc_ref, ks_ref, vs_ref, kw_ref, vw_ref, gate_ref, bg_ref, u_ref):
    ts = x_ref.shape[1]
    x = x_ref[0]
    ms = jnp.mean(x * x, axis=-1, keepdims=True)
    y = x * lax.rsqrt(ms + EPS) * g1_ref[...]
    h = y * (1.0 + mod_ref[0, 1:2, :]) + mod_ref[0, 0:1, :]
    hb = h.astype(BF16)
    seg = seg_ref[...]

    def proj(c0, c1):
        return _dot(hb, w_ref[:, c0:c1])

    lane = lax.broadcasted_iota(jnp.int32, (ts, HEAD_DIM), 1)
    ones192 = jnp.ones((ts, V_LANES - HEAD_DIM), F32)
    q_tail = jnp.where(lane == SHIFT_LANE - HEAD_DIM, 1.0, 0.0).astype(F32)

    heads_per_tile = MXU_N // HEAD_DIM
    row = pl.program_id(1) * ts + lax.broadcasted_iota(jnp.int32, (ts, HEAD_DIM), 0)
    onehot = jnp.where(lane == row // SEL_BLOCK, 1.0, 0.0).astype(F32) + tail_ref[0:1, :]
    win_tail = jnp.broadcast_to(tail_ref[1:2, :], (ts, HEAD_DIM))

    def do_q(c):
        v = proj(C_Q + c * MXU_N, C_Q + (c + 1) * MXU_N)
        vn = v * lax.rsqrt(_seg_mean_sq(v, seg) + EPS) * gq_ref[...] * (HEAD_DIM ** -0.5 * LOG2E)
        for hh in range(heads_per_tile):
            qh = jnp.concatenate([vn[:, hh * HEAD_DIM:(hh + 1) * HEAD_DIM], q_tail], axis=-1)
            q_ref[0, c * heads_per_tile + hh] = qh.astype(BF16)

    def do_kvc():
        v = proj(C_KVC, C_KK)
        kvc_ref[0, 0] = v[:, :D_KV]
        kvc_ref[0, 1] = v[:, D_KV:]

    def do_kk():
        v = proj(C_KK, C_VV)
        vn = v * lax.rsqrt(_seg_mean_sq(v, seg) + EPS) * gkk_ref[...]
        for g in range(N_GROUPS):
            ks_ref[0, g] = jnp.concatenate(
                [vn[:, g * HEAD_DIM:(g + 1) * HEAD_DIM], onehot], axis=-1).astype(BF16)
            kw_ref[0, g] = jnp.concatenate(
                [vn[:, D_KV + g * HEAD_DIM:D_KV + (g + 1) * HEAD_DIM], win_tail], axis=-1).astype(BF16)

    def do_vv():
        v = proj(C_VV, C_GATE)
        for g in range(N_GROUPS):
            vs_ref[0, g] = jnp.concatenate(
                [v[:, g * HEAD_DIM:(g + 1) * HEAD_DIM], ones192], axis=-1).astype(BF16)
            vw_ref[0, g] = jnp.concatenate(
                [v[:, D_KV + g * HEAD_DIM:D_KV + (g + 1) * HEAD_DIM], ones192], axis=-1).astype(BF16)

    def do_gate():
        v = _sigmoid(proj(C_GATE, C_BG))
        for g in range(N_GROUPS):
            gate_ref[0, g] = v[:, g * LANES:(g + 1) * LANES]

    def do_bg(c):
        cols = slice(c * MXU_N, (c + 1) * MXU_N)
        bg_ref[0, :, cols] = proj(C_BG + c * MXU_N, C_BG + (c + 1) * MXU_N).astype(BF16)

    def do_u(c):
        cols = slice(c * MXU_N, (c + 1) * MXU_N)
        u_ref[0, :, cols] = (proj(C_CG + c * MXU_N, C_CG + (c + 1) * MXU_N)
                             * proj(C_XT + c * MXU_N, C_XT + (c + 1) * MXU_N)).astype(BF16)

    do_q(0)
    do_bg(0)
    do_q(1)
    do_bg(1)
    do_kvc()
    do_u(0)
    do_kk()
    do_u(1)
    do_gate()
    do_vv()


def _inproj(x, mod3, g1, w_p, seg, gq4, gkk4, tails):
    bsz, seq, d = x.shape
    ts = TS_IN
    const2 = lambda b, i: (0, 0)
    k_shape = jax.ShapeDtypeStruct((bsz, N_GROUPS, seq, LANES), BF16)
    k_spec = pl.BlockSpec((1, N_GROUPS, ts, LANES), lambda b, i: (b, 0, i, 0))
    v_shape = jax.ShapeDtypeStruct((bsz, N_GROUPS, seq, V_LANES), BF16)
    v_spec = pl.BlockSpec((1, N_GROUPS, ts, V_LANES), lambda b, i: (b, 0, i, 0))
    return pl.pallas_call(
        _inproj_kernel,
        grid=(bsz, seq // ts),
        in_specs=[pl.BlockSpec((1, ts, d), lambda b, i: (b, i, 0)),
                  pl.BlockSpec((1, 6, d), lambda b, i: (b, 0, 0)),
                  pl.BlockSpec((1, d), const2),
                  pl.BlockSpec(w_p.shape, const2, pipeline_mode=pl.Buffered(1)),
                  pl.BlockSpec(seg.shape, const2),
                  pl.BlockSpec(gq4.shape, const2),
                  pl.BlockSpec(gkk4.shape, const2),
                  pl.BlockSpec(tails.shape, const2)],
        out_specs=[pl.BlockSpec((1, N_HEADS, ts, LANES), lambda b, i: (b, 0, i, 0)),
                   pl.BlockSpec((1, 2, ts, D_KV), lambda b, i: (b, 0, i, 0)),
                   k_spec, v_spec, k_spec, v_spec,
                   pl.BlockSpec((1, N_GROUPS, ts, LANES), lambda b, i: (b, 0, i, 0)),
                   pl.BlockSpec((1, ts, D_CONV), lambda b, i: (b, i, 0)),
                   pl.BlockSpec((1, ts, D_CONV), lambda b, i: (b, i, 0))],
        out_shape=[jax.ShapeDtypeStruct((bsz, N_HEADS, seq, LANES), BF16),
                   jax.ShapeDtypeStruct((bsz, 2, seq, D_KV), F32),
                   k_shape, v_shape, k_shape, v_shape,
                   jax.ShapeDtypeStruct((bsz, N_GROUPS, seq, LANES), F32),
                   jax.ShapeDtypeStruct((bsz, seq, D_CONV), BF16),
                   jax.ShapeDtypeStruct((bsz, seq, D_CONV), BF16)],
        compiler_params=pltpu.CompilerParams(
            dimension_semantics=("arbitrary", "arbitrary"), vmem_limit_bytes=VMEM_LIMIT),
        name="inproj",
    )(x, mod3, g1, w_p, seg, gq4, gkk4, tails)


def _compress_kernel(kvc_ref, w1k_ref, w1v_ref, w2k_ref, w2v_ref, pk_ref, pv_ref, gk_ref,
                     kct_ref, vc_ref):
    n_rows = kvc_ref.shape[2] // CMP_STRIDE
    half = CMP_STRIDE * HEAD_DIM
    zlane = jnp.zeros((n_rows, HEAD_DIM), F32)

    def branch(which, w1_ref, w2_ref, pos_ref):
        rows = [kvc_ref[0, which, pl.ds(r, n_rows, stride=CMP_STRIDE), :] for r in range(CMP_STRIDE)]
        outs = []
        for g in range(N_GROUPS):
            x = jnp.concatenate([rw[:, g * HEAD_DIM:(g + 1) * HEAD_DIM] for rw in rows], axis=-1)
            a1 = (x + pos_ref[0:1, :]).astype(BF16)
            a2 = (x + pos_ref[1:2, :]).astype(BF16)
            hid = _dot(a1, w1_ref[0:half, :]) + pltpu.roll(_dot(a2, w1_ref[half:, :]), n_rows - 1, 0)
            outs.append(_dot(_silu(hid).astype(BF16), w2_ref[...]))
        return outs

    kc = branch(0, w1k_ref, w2k_ref, pk_ref)
    vc = branch(1, w1v_ref, w2v_ref, pv_ref)
    for g in range(N_GROUPS):
        kn = kc[g] * lax.rsqrt(jnp.mean(kc[g] * kc[g], axis=-1, keepdims=True) + EPS) * gk_ref[...]
        kct_ref[0, g] = jnp.concatenate([kn, zlane], axis=-1).T.astype(BF16)
        vc_ref[0, g] = jnp.concatenate([vc[g], zlane], axis=-1).astype(BF16)


def _compress(kvc, w1k, w1v, w2k, w2v, pk2, pv2, gk):
    bsz, _, seq, _ = kvc.shape
    c2 = lambda b: (0, 0)
    out_shape = jax.ShapeDtypeStruct((bsz, N_GROUPS, N_CMP_PAD, LANES), BF16)
    out_spec = pl.BlockSpec((1, N_GROUPS, N_CMP_PAD, LANES), lambda b: (b, 0, 0, 0))
    return pl.pallas_call(
        _compress_kernel,
        grid=(bsz,),
        in_specs=[pl.BlockSpec((1, 2, seq, D_KV), lambda b: (b, 0, 0, 0)),
                  pl.BlockSpec(w1k.shape, c2), pl.BlockSpec(w1v.shape, c2),
                  pl.BlockSpec(w2k.shape, c2), pl.BlockSpec(w2v.shape, c2),
                  pl.BlockSpec(pk2.shape, c2), pl.BlockSpec(pv2.shape, c2),
                  pl.BlockSpec(gk.shape, c2)],
        out_specs=[out_spec, out_spec],
        out_shape=[out_shape, out_shape],
        compiler_params=pltpu.CompilerParams(
            dimension_semantics=("arbitrary",), vmem_limit_bytes=VMEM_LIMIT),
        name="compress",
    )(kvc, w1k, w1v, w2k, w2v, pk2, pv2, gk)


def _logits(qa, k_rows, bias):
    s = lax.dot_general(qa, k_rows, _NT, preferred_element_type=F32)
    return s if bias is None else s + bias


def _shifted_pv(qa, k_rows, v_rows, bias, keep=None):
    if keep is not None:
        v_rows = v_rows * keep.astype(BF16)
    return _dot(jnp.exp2(_logits(qa, k_rows, bias)).astype(BF16), v_rows)


def _online_update(qa, k_tile, v_tile, bias, acc_ref, m_ref):
    s = _logits(qa, k_tile, bias)
    m_prev = m_ref[...]
    m_new = jnp.maximum(m_prev, jnp.max(s, axis=-1, keepdims=True))
    alpha = jnp.exp2(m_prev - m_new)
    p = jnp.exp2(s - jnp.concatenate([m_new] * (TQ // LANES), axis=-1))
    acc_ref[...] = (acc_ref[...] * jnp.concatenate([alpha] * (V_LANES // LANES), axis=-1)
                    + _dot(p.astype(BF16), v_tile))
    m_ref[...] = m_new


def _selection_bias(pc, qi, ovlt, n_sel):
    psum = pc[0:TQ] + pc[TQ:2 * TQ] + pc[2 * TQ:3 * TQ] + pc[3 * TQ:4 * TQ]
    hi = psum.astype(BF16)
    lo = (psum - hi.astype(F32)).astype(BF16)
    pslc_t = (lax.dot_general(ovlt, hi, _NT, preferred_element_type=F32)
              + lax.dot_general(ovlt, lo, _NT, preferred_element_type=F32))
    top_k = min(SEL_TOPK, n_sel)
    sub = SUBLANES
    j = lax.broadcasted_iota(jnp.int32, (n_sel, TQ), 0)
    t = qi * TQ + lax.broadcasted_iota(jnp.int32, (n_sel, TQ), 1)
    dist = jnp.right_shift(t, int(math.log2(SEL_BLOCK))) - j
    score = jnp.where(dist < N_LOCAL_FORCED, FORCED_SCORE, pslc_t[SEL_LANE0:SEL_LANE0 + n_sel, :])
    score = jnp.where(j == 0, FORCED_SCORE, score)
    score = jnp.where(dist >= 0, score, NEG_INF)
    groups = [score[a * sub:(a + 1) * sub] for a in range(n_sel // sub)]
    cnts = [jnp.zeros((sub, TQ), F32) for _ in groups]
    j_sub = lax.broadcasted_iota(jnp.int32, (sub, TQ), 0)
    for i in range(n_sel):
        row = jnp.broadcast_to(score[i:i + 1, :], (sub, TQ))
        for a, grp in enumerate(groups):
            if a * sub > i:
                beats = jnp.where(row >= grp, 1.0, 0.0)
            elif a * sub + sub - 1 <= i:
                beats = jnp.where(row > grp, 1.0, 0.0)
            else:
                beats = jnp.where(j_sub + a * sub > i, jnp.where(row >= grp, 1.0, 0.0),
                                  jnp.where(row > grp, 1.0, 0.0))
            cnts[a] = cnts[a] + beats
    sel_t = jnp.where(jnp.concatenate(cnts, axis=0) < top_k, 0.0, NEG_INF)
    return jnp.concatenate(
        [jnp.zeros((SEL_LANE0, TQ), F32), sel_t,
         jnp.zeros((LANES - SEL_LANE0 - n_sel, TQ), F32)], axis=0).T


def _attn_step(shifted, q_ref, kct_ref, vc_ref, ks_ref, vs_ref, kw_ref, vw_ref, gate_ref,
               cb_ref, b0_ref, b1_ref, w2_ref, ovlt_ref, o_ref, qa_ref, acc_s, m_s, acc_w, m_w):
    qi = pl.program_id(1)
    groups = range(N_GROUPS)
    q = [q_ref[0, g * HPG:(g + 1) * HPG].reshape(ROWS, LANES) for g in groups]

    def key_rows(ref, g, kt, n_tiles=1):
        return ref[0, g, pl.ds(pl.multiple_of(kt * TQ, TQ), n_tiles * TQ), :]

    def compressed_and_selection(which=groups):
        o_c = []
        for g in which:
            s = _dot(q[g], kct_ref[0, g]) + cb_ref[g, 0]
            m = jnp.max(s, axis=-1, keepdims=True)
            e = jnp.exp2(s - m)
            l = jnp.sum(e, axis=-1, keepdims=True)
            pc = e * jnp.where(m > 0.5 * NEG_INF, 1.0 / l, 0.0)
            o_c.append(_dot(pc.astype(BF16), vc_ref[0, g]))
            selbias = _selection_bias(pc, qi, ovlt_ref[...], ks_ref.shape[2] // SEL_BLOCK)
            for h in range(HPG):
                qa_ref[g, h * TQ:(h + 1) * TQ, :] = (
                    q_ref[0, g * HPG + h].astype(F32) + selbias).astype(BF16)
        return o_c

    kt1 = jnp.maximum(qi - 1, 0)
    kt2 = jnp.maximum(qi - 2, 0)
    if shifted:
        has1 = jnp.where(qi >= 1, 1.0, 0.0)
        has2 = jnp.where(qi >= 2, 1.0, 0.0)
        a_w, o_c = [], []
        for g in groups:
            a_w.append(
                _shifted_pv(q[g], key_rows(kw_ref, g, kt2), key_rows(vw_ref, g, kt2), w2_ref[...], has2)
                + _shifted_pv(q[g], key_rows(kw_ref, g, kt1), key_rows(vw_ref, g, kt1), b1_ref[g], has1)
                + _shifted_pv(q[g], key_rows(kw_ref, g, qi), key_rows(vw_ref, g, qi), b0_ref[g]))
            o_c += compressed_and_selection([g])
        acc_s[...] = jnp.zeros(acc_s.shape, F32)
        n_plain = jnp.maximum(qi - 1, 0)

        def pair_body(it, carry):
            pv = [None] * N_GROUPS
            for half in range(2):
                for g in groups:
                    t = _shifted_pv(qa_ref[g], key_rows(ks_ref, g, 2 * it + half),
                                    key_rows(vs_ref, g, 2 * it + half), None)
                    pv[g] = t if pv[g] is None else pv[g] + t
            for g in groups:
                acc_s[g] += pv[g]
            return carry

        lax.fori_loop(0, n_plain // 2, pair_body, 0)
        odd = jnp.where(n_plain % 2 == 1, 1.0, 0.0)
        a_s = [acc_s[g]
               + _shifted_pv(qa_ref[g], key_rows(ks_ref, g, kt2), key_rows(vs_ref, g, kt2), None, odd)
               + _shifted_pv(qa_ref[g], key_rows(ks_ref, g, kt1), key_rows(vs_ref, g, kt1), b1_ref[g], has1)
               + _shifted_pv(qa_ref[g], key_rows(ks_ref, g, qi), key_rows(vs_ref, g, qi), b0_ref[g])
               for g in groups]
    else:
        o_c = compressed_and_selection()
        a_s, a_w = [], []
        for g in groups:
            qa = qa_ref[g]
            sel = (acc_s.at[g], m_s.at[g])
            win = (acc_w.at[g], m_w.at[g])
            for acc, m_ref in (sel, win):
                acc[...] = jnp.zeros(acc.shape, F32)
                m_ref[...] = jnp.full(m_ref.shape, NEG_INF, F32)

            def sel_body(kt, carry, g=g, qa=qa, sel=sel):
                _online_update(qa, key_rows(ks_ref, g, kt), key_rows(vs_ref, g, kt), None, *sel)
                return carry

            lax.fori_loop(0, qi - 1, sel_body, 0)

            @pl.when(qi >= 2)
            def _(g=g, win=win):
                _online_update(q[g], key_rows(kw_ref, g, kt2), key_rows(vw_ref, g, kt2), w2_ref[...], *win)

            @pl.when(qi >= 1)
            def _(g=g, qa=qa, sel=sel, win=win):
                _online_update(qa, key_rows(ks_ref, g, kt1), key_rows(vs_ref, g, kt1), b1_ref[g], *sel)
                _online_update(q[g], key_rows(kw_ref, g, kt1), key_rows(vw_ref, g, kt1), b1_ref[g], *win)

            _online_update(qa, key_rows(ks_ref, g, qi), key_rows(vs_ref, g, qi), b0_ref[g], *sel)
            _online_update(q[g], key_rows(kw_ref, g, qi), key_rows(vw_ref, g, qi), b0_ref[g], *win)
            a_s.append(acc_s[g])
            a_w.append(acc_w[g])

    outs = []
    for g in groups:
        o_s = a_s[g][:, :LANES] / a_s[g][:, LANES:]
        o_w = a_w[g][:, :LANES] / a_w[g][:, LANES:]
        gates = gate_ref[0, g]
        for h in range(HPG):
            rows = slice(h * TQ, (h + 1) * TQ)
            g_c = gates[:, h * N_BRANCH + 0:h * N_BRANCH + 1]
            g_s = gates[:, h * N_BRANCH + 1:h * N_BRANCH + 2]
            g_w = gates[:, h * N_BRANCH + 2:h * N_BRANCH + 3]
            o_h = g_c * o_c[g][rows] + g_s * o_s[rows] + g_w * o_w[rows]
            outs.append(o_h[:, :HEAD_DIM])
    o_ref[0] = jnp.concatenate(outs, axis=-1)


def _attn_kernel(flag_ref, *refs):
    use_shift = flag_ref[0] == 1

    @pl.when(use_shift)
    def _():
        _attn_step(True, *refs)

    @pl.when(jnp.logical_not(use_shift))
    def _():
        _attn_step(False, *refs)


def _attn(flag, q, kct, vc, ks, vs, kw, vw, gate, cb, b0, b1, w2, ovlt):
    bsz, _, seq, _ = q.shape
    nq = seq // TQ
    per_batch = lambda b, i: (b, 0, 0, 0)
    k_spec = pl.BlockSpec((1, N_GROUPS, seq, LANES), per_batch)
    v_spec = pl.BlockSpec((1, N_GROUPS, seq, V_LANES), per_batch)
    cmp_spec = pl.BlockSpec((1, N_GROUPS, N_CMP_PAD, LANES), per_batch)
    const = lambda shape: pl.BlockSpec(shape, lambda b, i: (0,) * len(shape),
                                       pipeline_mode=pl.Buffered(1))
    return pl.pallas_call(
        _attn_kernel,
        grid=(bsz, nq),
        in_specs=[pl.BlockSpec(memory_space=pltpu.SMEM),
                  pl.BlockSpec((1, N_HEADS, TQ, LANES), lambda b, i: (b, 0, i, 0)),
                  cmp_spec, cmp_spec, k_spec, v_spec, k_spec, v_spec,
                  pl.BlockSpec((1, N_GROUPS, TQ, LANES), lambda b, i: (b, 0, i, 0)),
                  pl.BlockSpec((N_GROUPS, 1, ROWS, N_CMP_PAD), lambda b, i: (0, i, 0, 0)),
                  const(b0.shape), const(b1.shape), const(w2.shape), const(ovlt.shape)],
        out_specs=pl.BlockSpec((1, TQ, D_ATTN), lambda b, i: (b, i, 0)),
        out_shape=jax.ShapeDtypeStruct((bsz, seq, D_ATTN), F32),
        scratch_shapes=[pltpu.VMEM((N_GROUPS, ROWS, LANES), BF16),
                        pltpu.VMEM((N_GROUPS, ROWS, V_LANES), F32),
                        pltpu.VMEM((N_GROUPS, ROWS, LANES), F32),
                        pltpu.VMEM((N_GROUPS, ROWS, V_LANES), F32),
                        pltpu.VMEM((N_GROUPS, ROWS, LANES), F32)],
        compiler_params=pltpu.CompilerParams(
            dimension_semantics=("arbitrary", "arbitrary"),
            vmem_limit_bytes=VMEM_LIMIT),
        name="attn",
    )(flag, q, kct, vc, ks, vs, kw, vw, gate, cb, b0, b1, w2, ovlt)


TS_OUT = 512
HALO = 16


def _outproj_ffn_kernel(ya_ref, bg_ref, u_ref, uh_ref, cw_ref, ga_ref, gc_ref, w_ref, x_ref, mod_ref,
                        g2_ref, w1_ref, w3_ref, w2_ref, o_ref):
    u = u_ref[0].astype(F32)
    halo = uh_ref[0].astype(F32) * jnp.where(pl.program_id(1) > 0, 1.0, 0.0)
    h1 = halo[HALO - 1:HALO, :]
    h2 = halo[HALO - 2:HALO - 1, :]
    row = lax.broadcasted_iota(jnp.int32, u.shape, 0)
    u1 = jnp.where(row == 0, h1, pltpu.roll(u, 1, 0))
    u2 = jnp.where(row == 0, h2, jnp.where(row == 1, h1, pltpu.roll(u, 2, 0)))
    conv = u2 * cw_ref[0:1, :] + u1 * cw_ref[1:2, :] + u * cw_ref[2:3, :]
    yc = bg_ref[0].astype(F32) * conv
    yc = yc * lax.rsqrt(jnp.mean(yc * yc, axis=-1, keepdims=True) + EPS) * gc_ref[...]
    ya = ya_ref[0]
    ya = ya * lax.rsqrt(jnp.mean(ya * ya, axis=-1, keepdims=True) + EPS) * ga_ref[...]
    proj = _dot(ya.astype(BF16), w_ref[0:D_ATTN, :]) + _dot(yc.astype(BF16), w_ref[D_ATTN:, :])
    x = x_ref[0] + mod_ref[0, 2:3, :] * proj

    ms = jnp.mean(x * x, axis=-1, keepdims=True)
    h = x * lax.rsqrt(ms + EPS) * g2_ref[...] * (1.0 + mod_ref[0, 4:5, :]) + mod_ref[0, 3:4, :]
    hb = h.astype(BF16)
    a = _dot(hb, w1_ref[...])
    b = _dot(hb, w3_ref[...])
    act = (_silu(a) * b).astype(BF16)
    o_ref[0] = x + mod_ref[0, 5:6, :] * _dot(act, w2_ref[...])


def _outproj_ffn(ya, bg, u, conv_w, ga, gc, w_out, x, mod3, g2, w1, w3, w2):
    bsz, seq, d = x.shape
    ts = TS_OUT
    c2 = lambda b, i: (0, 0)
    row_spec = lambda width: pl.BlockSpec((1, ts, width), lambda b, i: (b, i, 0))
    wspec = lambda w: pl.BlockSpec(w.shape, c2, pipeline_mode=pl.Buffered(1))
    return pl.pallas_call(
        _outproj_ffn_kernel,
        grid=(bsz, seq // ts),
        in_specs=[row_spec(D_ATTN), row_spec(D_CONV), row_spec(D_CONV),
                  pl.BlockSpec((1, HALO, D_CONV),
                               lambda b, i: (b, jnp.maximum(i * (ts // HALO) - 1, 0), 0)),
                  pl.BlockSpec(conv_w.shape, c2),
                  pl.BlockSpec((1, D_ATTN), c2), pl.BlockSpec((1, D_CONV), c2),
                  wspec(w_out),
                  row_spec(d),
                  pl.BlockSpec((1, 6, d), lambda b, i: (b, 0, 0)),
                  pl.BlockSpec((1, d), c2),
                  wspec(w1), wspec(w3), wspec(w2)],
        out_specs=row_spec(d),
        out_shape=jax.ShapeDtypeStruct((bsz, seq, d), F32),
        compiler_params=pltpu.CompilerParams(
            dimension_semantics=("arbitrary", "arbitrary"), vmem_limit_bytes=VMEM_LIMIT),
        name="outproj_ffn",
    )(ya, bg, u, u, conv_w, ga, gc, w_out, x, mod3, g2, w1, w3, w2)


def _t5_bucket_np(rel):
    n = np.maximum(rel, 0)
    max_exact = N_BUCKETS // 2
    nf = np.maximum(n, 1).astype(np.float32)
    large = max_exact + (np.log(nf / max_exact) / math.log(MAX_DISTANCE / max_exact)
                         * (N_BUCKETS - max_exact)).astype(np.int32)
    return np.where(n < max_exact, n, np.minimum(large, N_BUCKETS - 1)).astype(np.int32)


def _bias_of_rel(tab_rel, rel):
    idx = jnp.asarray(_t5_bucket_np(rel))
    tab_b = tab_rel.T.reshape((N_HEADS, N_BUCKETS) + (1,) * idx.ndim)
    vals = jnp.zeros((N_HEADS,) + idx.shape, F32)
    for k in range(N_BUCKETS):
        vals = jnp.where(idx == k, tab_b[:, k], vals)
    return jnp.where(jnp.asarray(rel >= 0), vals, NEG_INF)


def _toeplitz_offsets(length, n_cols):
    k = np.arange(length)
    return np.where(k < n_cols, k, k - length)


TOEPLITZ_LEN = 2 * TQ
PATTERN_LEN = 3 * LANES


def _tables_kernel(v0_ref, v1_ref, vp_ref, b0_ref, b1_ref, cb_ref):
    def toeplitz_tile(v_ref):
        rows = jnp.broadcast_to(v_ref[0], (TQ, TOEPLITZ_LEN))
        return pltpu.roll(rows, 0, 1, stride=1, stride_axis=0)[:, :TQ]

    b0_ref[0] = toeplitz_tile(v0_ref)
    b1_ref[0] = toeplitz_tile(v1_ref)
    vp = vp_ref[0]
    shift = TQ // CMP_STRIDE
    pat = jnp.concatenate(
        [vp[:, :2 * N_CMP_PAD]] + [pltpu.roll(vp, a, 1)[:, :2 * N_CMP_PAD] for a in range(1, shift)],
        axis=0)
    for qi in range(cb_ref.shape[1]):
        cb_ref[0, qi] = pat[:, N_CMP_PAD - qi * shift:2 * N_CMP_PAD - qi * shift]


def _bias_tables(rel_bias_table, seq):
    tab_rel = (rel_bias_table - rel_bias_table[N_BUCKETS - 1:N_BUCKETS, :]) * LOG2E
    nq = seq // TQ
    n_cmp = (seq - CMP_BLOCK) // CMP_STRIDE + 1
    shift = TQ // CMP_STRIDE
    assert nq * shift <= N_CMP_PAD and n_cmp * CMP_STRIDE + CMP_BLOCK - 1 > seq
    assert shift + 2 * N_CMP_PAD - 1 <= PATTERN_LEN
    d = _toeplitz_offsets(TOEPLITZ_LEN, TQ)
    v0 = _bias_of_rel(tab_rel, -d)[:, None, :]
    v1 = _bias_of_rel(tab_rel, TQ - d)[:, None, :]
    off = N_CMP_PAD * CMP_STRIDE - (CMP_BLOCK - 1)
    dp = _toeplitz_offsets(PATTERN_LEN, 2 * N_CMP_PAD)[None, :]
    vp = _bias_of_rel(tab_rel, np.arange(CMP_STRIDE)[:, None] + off - CMP_STRIDE * dp)
    head = lambda h: (h // HPG, h % HPG, 0)
    b0, b1, cb = pl.pallas_call(
        _tables_kernel,
        grid=(N_HEADS,),
        in_specs=[pl.BlockSpec((1, 1, TOEPLITZ_LEN), lambda h: (h, 0, 0)),
                  pl.BlockSpec((1, 1, TOEPLITZ_LEN), lambda h: (h, 0, 0)),
                  pl.BlockSpec((1, CMP_STRIDE, PATTERN_LEN), lambda h: (h, 0, 0))],
        out_specs=[pl.BlockSpec((1, TQ, TQ), head), pl.BlockSpec((1, TQ, TQ), head),
                   pl.BlockSpec((1, nq, TQ, N_CMP_PAD), lambda h: (h // HPG, 0, h % HPG, 0))],
        out_shape=[jax.ShapeDtypeStruct((N_GROUPS, ROWS, TQ), F32),
                   jax.ShapeDtypeStruct((N_GROUPS, ROWS, TQ), F32),
                   jax.ShapeDtypeStruct((N_GROUPS, nq, ROWS, N_CMP_PAD), F32)],
        compiler_params=pltpu.CompilerParams(
            dimension_semantics=("arbitrary",), vmem_limit_bytes=VMEM_LIMIT),
        name="bias_tables",
    )(v0, v1, vp)
    i = np.arange(TQ)[:, None]
    jj = np.arange(TQ)[None, :]
    w2 = jnp.asarray(np.tile(np.where(jj > i, 0.0, NEG_INF).astype(np.float32), (HPG, 1)))
    n_sel = seq // SEL_BLOCK
    cs = np.arange(n_cmp) * CMP_STRIDE
    ce = cs + CMP_BLOCK - 1
    ss = np.arange(n_sel) * SEL_BLOCK
    ov = np.clip(np.minimum(ce[:, None], ss[None, :] + SEL_BLOCK - 1)
                 - np.maximum(cs[:, None], ss[None, :]) + 1, 0, None) / CMP_STRIDE
    ovlt = np.zeros((LANES, N_CMP_PAD), np.float32)
    ovlt[SEL_LANE0:SEL_LANE0 + n_sel, :n_cmp] = ov.T
    return tab_rel, cb, b0, b1, w2, jnp.asarray(ovlt, BF16)


def kernel(x, c, w_ada, b_ada, norm1_gain, w_in, q_gain, k_cmp_gain, k_sel_gain, k_win_gain,
           cmp_pos_k, cmp_pos_v, w_ck1, w_ck2, w_cv1, w_cv2, rel_bias_table, conv_w,
           attn_out_gain, conv_out_gain, w_out, norm2_gain, w_ff1, w_ff3, w_ff2):
    bsz, seq, d = x.shape
    assert d == D_MODEL and seq % TQ == 0 and SEL_LANE0 + seq // SEL_BLOCK <= SHIFT_LANE
    assert (seq - CMP_BLOCK) // CMP_STRIDE + 1 <= N_CMP_PAD and seq // CMP_STRIDE == N_CMP_PAD

    seg_of = lambda n: jnp.asarray(np.kron(np.eye(n), np.ones((HEAD_DIM, HEAD_DIM))) / HEAD_DIM, BF16)
    seg4 = seg_of(MXU_N // HEAD_DIM)
    tab_rel, cb, b0, b1, w2m, ovlt = _bias_tables(rel_bias_table, seq)
    bias_max = jnp.max(jnp.abs(tab_rel))
    two = lambda g: jnp.tile(g.reshape(1, HEAD_DIM), (1, 2))
    n_gate = N_BRANCH * HPG
    gate_pad = jnp.zeros((d, LANES - n_gate), BF16)

    for layer in range(w_in.shape[0]):
        wi = w_in[layer].astype(BF16)
        o_g = D_ATTN + 6 * D_KV
        kv = lambda n: wi[:, D_ATTN + n * D_KV:D_ATTN + (n + 1) * D_KV]
        w_p = jnp.concatenate(
            [wi[:, :D_ATTN], kv(0), kv(1), kv(2), kv(4), kv(3), kv(5),
             wi[:, o_g:o_g + n_gate], gate_pad,
             wi[:, o_g + n_gate:o_g + 2 * n_gate], gate_pad, wi[:, o_g + 2 * n_gate:]],
            axis=-1)
        w1k, w1v = w_ck1[layer].astype(BF16), w_cv1[layer].astype(BF16)
        w2k, w2v = w_ck2[layer].astype(BF16), w_cv2[layer].astype(BF16)
        pk2 = cmp_pos_k[layer].reshape(2, CMP_STRIDE * HEAD_DIM)
        pv2 = cmp_pos_v[layer].reshape(2, CMP_STRIDE * HEAD_DIM)

        gq_max = jnp.max(jnp.abs(q_gain[layer]))
        bound = lambda gk: (HEAD_DIM ** 0.5 * LOG2E * 1.01) * gq_max * jnp.max(jnp.abs(gk)) + bias_max
        c_sel, c_win = bound(k_sel_gain[layer]), bound(k_win_gain[layer])
        flag = (2.0 * jnp.maximum(c_sel, c_win) <= MAX_SHIFTED_RANGE).astype(jnp.int32).reshape(1)
        tails = jnp.zeros((2, HEAD_DIM), F32).at[:, SHIFT_LANE - HEAD_DIM].set(
            -jnp.stack([c_sel, c_win]))

        mod3 = _adaln(c, w_ada[layer], b_ada[layer]).reshape(bsz, 6, d)
        q, kvc, ks, vs, kw, vw, gate, bg, u = _inproj(
            x, mod3, norm1_gain[layer].reshape(1, d), w_p, seg4,
            jnp.tile(q_gain[layer].reshape(1, HEAD_DIM), (1, MXU_N // HEAD_DIM)),
            jnp.concatenate([two(k_sel_gain[layer]), two(k_win_gain[layer])], axis=-1), tails)
        kct, vc = _compress(kvc, w1k, w1v, w2k, w2v, pk2, pv2,
                            k_cmp_gain[layer].reshape(1, HEAD_DIM))
        y_attn = _attn(flag, q, kct, vc, ks, vs, kw, vw, gate, cb, b0, b1, w2m, ovlt)
        x = _outproj_ffn(y_attn, bg, u, conv_w[layer], attn_out_gain[layer].reshape(1, D_ATTN),
                         conv_out_gain[layer].reshape(1, D_CONV), w_out[layer].astype(BF16), x, mod3,
                         norm2_gain[layer].reshape(1, d), w_ff1[layer].astype(BF16),
                         w_ff3[layer].astype(BF16), w_ff2[layer].astype(BF16))
    return x
```

```python
import functools
import math

import numpy as np
import jax
import jax.numpy as jnp
from jax import lax
from jax.experimental import pallas as pl
from jax.experimental.pallas import tpu as pltpu

F32 = jnp.float32
BF16 = jnp.bfloat16

D_MODEL = 1024
HEAD_DIM = 64
N_HEADS = 8
N_GROUPS = 2
HPG = N_HEADS // N_GROUPS
D_ATTN = N_HEADS * HEAD_DIM
D_KV = N_GROUPS * HEAD_DIM
N_BRANCH = 3
D_CONV = D_MODEL - D_ATTN
CONV_WIDTH = 3
CMP_BLOCK = 32
CMP_STRIDE = 16
CMP_HIDDEN = 256
SEL_BLOCK = 64
SEL_TOPK = 16
N_LOCAL_FORCED = 2
WINDOW = 512
N_BUCKETS = 32
MAX_DISTANCE = 128
EPS = 1e-6
NEG_INF = -1e30
FORCED_SCORE = 1e6

LANES = 128
SUBLANES = 8
TQ = 256
ROWS = HPG * TQ
N_CMP_PAD = 128
SEL_LANE0 = 64
SHIFT_LANE = 96
V_LANES = 256
LOG2E = math.log2(math.e)
MAX_SHIFTED_RANGE = 100.0
VMEM_LIMIT = 56 * 1024 * 1024

_NT = (((1,), (1,)), ((), ()))


def _dot(a, b):
    return jnp.dot(a, b, preferred_element_type=F32)


def _silu(v):
    return v * (1.0 / (1.0 + jnp.exp(-v)))


def _sigmoid(v):
    return 1.0 / (1.0 + jnp.exp(-v))


def _seg_mean_sq(v, seg):
    sq = v * v
    hi = sq.astype(BF16)
    lo = (sq - hi.astype(F32)).astype(BF16)
    return _dot(hi, seg) + _dot(lo, seg)


TN_ADALN = 1536


def _adaln_kernel(c_ref, w_ref, b_ref, o_ref):
    sc = _silu(c_ref[...]).astype(BF16)
    o_ref[...] = _dot(sc, w_ref[...].astype(BF16)) + b_ref[...]


def _adaln(c, w_ada, b_ada):
    bsz, d = c.shape
    n = w_ada.shape[1]
    tn = TN_ADALN
    return pl.pallas_call(
        _adaln_kernel,
        grid=(n // tn,),
        in_specs=[pl.BlockSpec((bsz, d), lambda j: (0, 0)),
                  pl.BlockSpec((d, tn), lambda j: (0, j)),
                  pl.BlockSpec((1, tn), lambda j: (0, j))],
        out_specs=pl.BlockSpec((bsz, tn), lambda j: (0, j)),
        out_shape=jax.ShapeDtypeStruct((bsz, n), F32),
        compiler_params=pltpu.CompilerParams(
            dimension_semantics=("arbitrary",), vmem_limit_bytes=VMEM_LIMIT),
        name="adaln",
    )(c, w_ada, b_ada.reshape(1, n))


C_Q, C_KVC, C_KK, C_VV, C_GATE, C_BG, C_CG, C_XT, C_END = (
    0, 512, 768, 1024, 1280, 1536, 2048, 2560, 3072)
MXU_N = 256
TS_IN = 1024


def _inproj_kernel(x_ref, mod_ref, g1_ref, w_ref, seg_ref, gq_ref, gkk_ref, tail_ref,
                   q_ref, kvc_ref, ks_ref, vs_ref, kw_ref, vw_ref, gate_ref, bg_ref, u_ref):
    ts = x_ref.shape[1]
    x = x_ref[0]
    ms = jnp.mean(x * x, axis=-1, keepdims=True)
    y = x * lax.rsqrt(ms + EPS) * g1_ref[...]
    h = y * (1.0 + mod_ref[0, 1:2, :]) + mod_ref[0, 0:1, :]
    hb = h.astype(BF16)
    seg = seg_ref[...]

    def proj(c0, c1):
        return _dot(hb, w_ref[:, c0:c1])

    lane = lax.broadcasted_iota(jnp.int32, (ts, HEAD_DIM), 1)
    ones192 = jnp.ones((ts, V_LANES - HEAD_DIM), F32)
    q_tail = jnp.where(lane == SHIFT_LANE - HEAD_DIM, 1.0, 0.0).astype(F32)

    heads_per_tile = MXU_N // HEAD_DIM
    row = pl.program_id(1) * ts + lax.broadcasted_iota(jnp.int32, (ts, HEAD_DIM), 0)
    onehot = jnp.where(lane == row // SEL_BLOCK, 1.0, 0.0).astype(F32) + tail_ref[0:1, :]
    win_tail = jnp.broadcast_to(tail_ref[1:2, :], (ts, HEAD_DIM))

    def do_q(c):
        v = proj(C_Q + c * MXU_N, C_Q + (c + 1) * MXU_N)
        vn = v * lax.rsqrt(_seg_mean_sq(v, seg) + EPS) * gq_ref[...] * (HEAD_DIM ** -0.5 * LOG2E)
        for hh in range(heads_per_tile):
            qh = jnp.concatenate([vn[:, hh * HEAD_DIM:(hh + 1) * HEAD_DIM], q_tail], axis=-1)
            q_ref[0, c * heads_per_tile + hh] = qh.astype(BF16)

    def do_kvc():
        v = proj(C_KVC, C_KK)
        kvc_ref[0, 0] = v[:, :D_KV]
        kvc_ref[0, 1] = v[:, D_KV:]

    def do_kk():
        v = proj(C_KK, C_VV)
        vn = v * lax.rsqrt(_seg_mean_sq(v, seg) + EPS) * gkk_ref[...]
        for g in range(N_GROUPS):
            ks_ref[0, g] = jnp.concatenate(
                [vn[:, g * HEAD_DIM:(g + 1) * HEAD_DIM], onehot], axis=-1).astype(BF16)
            kw_ref[0, g] = jnp.concatenate(
                [vn[:, D_KV + g * HEAD_DIM:D_KV + (g + 1) * HEAD_DIM], win_tail], axis=-1).astype(BF16)

    def do_vv():
        v = proj(C_VV, C_GATE)
        for g in range(N_GROUPS):
            vs_ref[0, g] = jnp.concatenate(
                [v[:, g * HEAD_DIM:(g + 1) * HEAD_DIM], ones192], axis=-1).astype(BF16)
            vw_ref[0, g] = jnp.concatenate(
                [v[:, D_KV + g * HEAD_DIM:D_KV + (g + 1) * HEAD_DIM], ones192], axis=-1).astype(BF16)

    def do_gate():
        v = _sigmoid(proj(C_GATE, C_BG))
        for g in range(N_GROUPS):
            gate_ref[0, g] = v[:, g * LANES:(g + 1) * LANES]

    def do_bg(c):
        cols = slice(c * MXU_N, (c + 1) * MXU_N)
        bg_ref[0, :, cols] = proj(C_BG + c * MXU_N, C_BG + (c + 1) * MXU_N).astype(BF16)

    def do_u(c):
        cols = slice(c * MXU_N, (c + 1) * MXU_N)
        u_ref[0, :, cols] = (proj(C_CG + c * MXU_N, C_CG + (c + 1) * MXU_N)
                             * proj(C_XT + c * MXU_N, C_XT + (c + 1) * MXU_N)).astype(BF16)

    do_q(0)
    do_bg(0)
    do_q(1)
    do_bg(1)
    do_kvc()
    do_u(0)
    do_kk()
    do_u(1)
    do_gate()
    do_vv()


def _inproj(x, mod3, g1, w_p, seg, gq4, gkk4, tails):
    bsz, seq, d = x.shape
    ts = TS_IN
    const2 = lambda b, i: (0, 0)
    k_shape = jax.ShapeDtypeStruct((bsz, N_GROUPS, seq, LANES), BF16)
    k_spec = pl.BlockSpec((1, N_GROUPS, ts, LANES), lambda b, i: (b, 0, i, 0))
    v_shape = jax.ShapeDtypeStruct((bsz, N_GROUPS, seq, V_LANES), BF16)
    v_spec = pl.BlockSpec((1, N_GROUPS, ts, V_LANES), lambda b, i: (b, 0, i, 0))
    return pl.pallas_call(
        _inproj_kernel,
        grid=(bsz, seq // ts),
        in_specs=[pl.BlockSpec((1, ts, d), lambda b, i: (b, i, 0)),
                  pl.BlockSpec((1, 6, d), lambda b, i: (b, 0, 0)),
                  pl.BlockSpec((1, d), const2),
                  pl.BlockSpec(w_p.shape, const2, pipeline_mode=pl.Buffered(1)),
                  pl.BlockSpec(seg.shape, const2),
                  pl.BlockSpec(gq4.shape, const2),
                  pl.BlockSpec(gkk4.shape, const2),
                  pl.BlockSpec(tails.shape, const2)],
        out_specs=[pl.BlockSpec((1, N_HEADS, ts, LANES), lambda b, i: (b, 0, i, 0)),
                   pl.BlockSpec((1, 2, ts, D_KV), lambda b, i: (b, 0, i, 0)),
                   k_spec, v_spec, k_spec, v_spec,
                   pl.BlockSpec((1, N_GROUPS, ts, LANES), lambda b, i: (b, 0, i, 0)),
                   pl.BlockSpec((1, ts, D_CONV), lambda b, i: (b, i, 0)),
                   pl.BlockSpec((1, ts, D_CONV), lambda b, i: (b, i, 0))],
        out_shape=[jax.ShapeDtypeStruct((bsz, N_HEADS, seq, LANES), BF16),
                   jax.ShapeDtypeStruct((bsz, 2, seq, D_KV), F32),
                   k_shape, v_shape, k_shape, v_shape,
                   jax.ShapeDtypeStruct((bsz, N_GROUPS, seq, LANES), F32),
                   jax.ShapeDtypeStruct((bsz, seq, D_CONV), BF16),
                   jax.ShapeDtypeStruct((bsz, seq, D_CONV), BF16)],
        compiler_params=pltpu.CompilerParams(
            dimension_semantics=("arbitrary", "arbitrary"), vmem_limit_bytes=VMEM_LIMIT),
        name="inproj",
    )(x, mod3, g1, w_p, seg, gq4, gkk4, tails)


def _compress_kernel(kvc_ref, w1k_ref, w1v_ref, w2k_ref, w2v_ref, pk_ref, pv_ref, gk_ref,
                     kct_ref, vc_ref):
    n_rows = kvc_ref.shape[2] // CMP_STRIDE
    half = CMP_STRIDE * HEAD_DIM
    zlane = jnp.zeros((n_rows, HEAD_DIM), F32)

    def branch(which, w1_ref, w2_ref, pos_ref):
        rows = [kvc_ref[0, which, pl.ds(r, n_rows, stride=CMP_STRIDE), :] for r in range(CMP_STRIDE)]
        outs = []
        for g in range(N_GROUPS):
            x = jnp.concatenate([rw[:, g * HEAD_DIM:(g + 1) * HEAD_DIM] for rw in rows], axis=-1)
            a1 = (x + pos_ref[0:1, :]).astype(BF16)
            a2 = (x + pos_ref[1:2, :]).astype(BF16)
            hid = _dot(a1, w1_ref[0:half, :]) + pltpu.roll(_dot(a2, w1_ref[half:, :]), n_rows - 1, 0)
            outs.append(_dot(_silu(hid).astype(BF16), w2_ref[...]))
        return outs

    kc = branch(0, w1k_ref, w2k_ref, pk_ref)
    vc = branch(1, w1v_ref, w2v_ref, pv_ref)
    for g in range(N_GROUPS):
        kn = kc[g] * lax.rsqrt(jnp.mean(kc[g] * kc[g], axis=-1, keepdims=True) + EPS) * gk_ref[...]
        kct_ref[0, g] = jnp.concatenate([kn, zlane], axis=-1).T.astype(BF16)
        vc_ref[0, g] = jnp.concatenate([vc[g], zlane], axis=-1).astype(BF16)


def _compress(kvc, w1k, w1v, w2k, w2v, pk2, pv2, gk):
    bsz, _, seq, _ = kvc.shape
    c2 = lambda b: (0, 0)
    out_shape = jax.ShapeDtypeStruct((bsz, N_GROUPS, N_CMP_PAD, LANES), BF16)
    out_spec = pl.BlockSpec((1, N_GROUPS, N_CMP_PAD, LANES), lambda b: (b, 0, 0, 0))
    return pl.pallas_call(
        _compress_kernel,
        grid=(bsz,),
        in_specs=[pl.BlockSpec((1, 2, seq, D_KV), lambda b: (b, 0, 0, 0)),
                  pl.BlockSpec(w1k.shape, c2), pl.BlockSpec(w1v.shape, c2),
                  pl.BlockSpec(w2k.shape, c2), pl.BlockSpec(w2v.shape, c2),
                  pl.BlockSpec(pk2.shape, c2), pl.BlockSpec(pv2.shape, c2),
                  pl.BlockSpec(gk.shape, c2)],
        out_specs=[out_spec, out_spec],
        out_shape=[out_shape, out_shape],
        compiler_params=pltpu.CompilerParams(
            dimension_semantics=("arbitrary",), vmem_limit_bytes=VMEM_LIMIT),
        name="compress",
    )(kvc, w1k, w1v, w2k, w2v, pk2, pv2, gk)


def _logits(qa, k_rows, bias):
    s = lax.dot_general(qa, k_rows, _NT, preferred_element_type=F32)
    return s if bias is None else s + bias


def _shifted_pv(qa, k_rows, v_rows, bias, keep=None):
    if keep is not None:
        v_rows = v_rows * keep.astype(BF16)
    return _dot(jnp.exp2(_logits(qa, k_rows, bias)).astype(BF16), v_rows)


def _online_update(qa, k_tile, v_tile, bias, acc_ref, m_ref):
    s = _logits(qa, k_tile, bias)
    m_prev = m_ref[...]
    m_new = jnp.maximum(m_prev, jnp.max(s, axis=-1, keepdims=True))
    alpha = jnp.exp2(m_prev - m_new)
    p = jnp.exp2(s - jnp.concatenate([m_new] * (TQ // LANES), axis=-1))
    acc_ref[...] = (acc_ref[...] * jnp.concatenate([alpha] * (V_LANES // LANES), axis=-1)
                    + _dot(p.astype(BF16), v_tile))
    m_ref[...] = m_new


def _selection_bias(pc, qi, ovlt, n_sel):
    psum = pc[0:TQ] + pc[TQ:2 * TQ] + pc[2 * TQ:3 * TQ] + pc[3 * TQ:4 * TQ]
    hi = psum.astype(BF16)
    lo = (psum - hi.astype(F32)).astype(BF16)
    pslc_t = (lax.dot_general(ovlt, hi, _NT, preferred_element_type=F32)
              + lax.dot_general(ovlt, lo, _NT, preferred_element_type=F32))
    top_k = min(SEL_TOPK, n_sel)
    sub = SUBLANES
    j = lax.broadcasted_iota(jnp.int32, (n_sel, TQ), 0)
    t = qi * TQ + lax.broadcasted_iota(jnp.int32, (n_sel, TQ), 1)
    dist = jnp.right_shift(t, int(math.log2(SEL_BLOCK))) - j
    score = jnp.where(dist < N_LOCAL_FORCED, FORCED_SCORE, pslc_t[SEL_LANE0:SEL_LANE0 + n_sel, :])
    score = jnp.where(j == 0, FORCED_SCORE, score)
    score = jnp.where(dist >= 0, score, NEG_INF)
    groups = [score[a * sub:(a + 1) * sub] for a in range(n_sel // sub)]
    cnts = [jnp.zeros((sub, TQ), F32) for _ in groups]
    j_sub = lax.broadcasted_iota(jnp.int32, (sub, TQ), 0)
    for i in range(n_sel):
        row = jnp.broadcast_to(score[i:i + 1, :], (sub, TQ))
        for a, grp in enumerate(groups):
            if a * sub > i:
                beats = jnp.where(row >= grp, 1.0, 0.0)
            elif a * sub + sub - 1 <= i:
                beats = jnp.where(row > grp, 1.0, 0.0)
            else:
                beats = jnp.where(j_sub + a * sub > i, jnp.where(row >= grp, 1.0, 0.0),
                                  jnp.where(row > grp, 1.0, 0.0))
            cnts[a] = cnts[a] + beats
    sel_t = jnp.where(jnp.concatenate(cnts, axis=0) < top_k, 0.0, NEG_INF)
    return jnp.concatenate(
        [jnp.zeros((SEL_LANE0, TQ), F32), sel_t,
         jnp.zeros((LANES - SEL_LANE0 - n_sel, TQ), F32)], axis=0).T


def _attn_step(shifted, q_ref, kct_ref, vc_ref, ks_ref, vs_ref, kw_ref, vw_ref, gate_ref,
               cb_ref, b0_ref, b1_ref, w2_ref, ovlt_ref, o_ref, qa_ref, acc_s, m_s, acc_w, m_w):
    qi = pl.program_id(1)
    groups = range(N_GROUPS)
    q = [q_ref[0, g * HPG:(g + 1) * HPG].reshape(ROWS, LANES) for g in groups]

    def key_rows(ref, g, kt, n_tiles=1):
        return ref[0, g, pl.ds(pl.multiple_of(kt * TQ, TQ), n_tiles * TQ), :]

    def compressed_and_selection(which=groups):
        o_c = []
        for g in which:
            s = _dot(q[g], kct_ref[0, g]) + cb_ref[g, 0]
            m = jnp.max(s, axis=-1, keepdims=True)
            e = jnp.exp2(s - m)
            l = jnp.sum(e, axis=-1, keepdims=True)
            pc = e * jnp.where(m > 0.5 * NEG_INF, 1.0 / l, 0.0)
            o_c.append(_dot(pc.astype(BF16), vc_ref[0, g]))
            selbias = _selection_bias(pc, qi, ovlt_ref[...], ks_ref.shape[2] // SEL_BLOCK)
            for h in range(HPG):
                qa_ref[g, h * TQ:(h + 1) * TQ, :] = (
                    q_ref[0, g * HPG + h].astype(F32) + selbias).astype(BF16)
        return o_c

    kt1 = jnp.maximum(qi - 1, 0)
    kt2 = jnp.maximum(qi - 2, 0)
    if shifted:
        has1 = jnp.where(qi >= 1, 1.0, 0.0)
        has2 = jnp.where(qi >= 2, 1.0, 0.0)
        a_w, o_c = [], []
        for g in groups:
            a_w.append(
                _shifted_pv(q[g], key_rows(kw_ref, g, kt2), key_rows(vw_ref, g, kt2), w2_ref[...], has2)
                + _shifted_pv(q[g], key_rows(kw_ref, g, kt1), key_rows(vw_ref, g, kt1), b1_ref[g], has1)
                + _shifted_pv(q[g], key_rows(kw_ref, g, qi), key_rows(vw_ref, g, qi), b0_ref[g]))
            o_c += compressed_and_selection([g])
        acc_s[...] = jnp.zeros(acc_s.shape, F32)
        n_plain = jnp.maximum(qi - 1, 0)

        def pair_body(it, carry):
            pv = [None] * N_GROUPS
            for half in range(2):
                for g in groups:
                    t = _shifted_pv(qa_ref[g], key_rows(ks_ref, g, 2 * it + half),
                                    key_rows(vs_ref, g, 2 * it + half), None)
                    pv[g] = t if pv[g] is None else pv[g] + t
            for g in groups:
                acc_s[g] += pv[g]
            return carry

        lax.fori_loop(0, n_plain // 2, pair_body, 0)
        odd = jnp.where(n_plain % 2 == 1, 1.0, 0.0)
        a_s = [acc_s[g]
               + _shifted_pv(qa_ref[g], key_rows(ks_ref, g, kt2), key_rows(vs_ref, g, kt2), None, odd)
               + _shifted_pv(qa_ref[g], key_rows(ks_ref, g, kt1), key_rows(vs_ref, g, kt1), b1_ref[g], has1)
               + _shifted_pv(qa_ref[g], key_rows(ks_ref, g, qi), key_rows(vs_ref, g, qi), b0_ref[g])
               for g in groups]
    else:
        o_c = compressed_and_selection()
        a_s, a_w = [], []
        for g in groups:
            qa = qa_ref[g]
            sel = (acc_s.at[g], m_s.at[g])
            win = (acc_w.at[g], m_w.at[g])
            for acc, m_ref in (sel, win):
                acc[...] = jnp.zeros(acc.shape, F32)
                m_ref[...] = jnp.full(m_ref.shape, NEG_INF, F32)

            def sel_body(kt, carry, g=g, qa=qa, sel=sel):
                _online_update(qa, key_rows(ks_ref, g, kt), key_rows(vs_ref, g, kt), None, *sel)
                return carry

            lax.fori_loop(0, qi - 1, sel_body, 0)

            @pl.when(qi >= 2)
            def _(g=g, win=win):
                _online_update(q[g], key_rows(kw_ref, g, kt2), key_rows(vw_ref, g, kt2), w2_ref[...], *win)

            @pl.when(qi >= 1)
            def _(g=g, qa=qa, sel=sel, win=win):
                _online_update(qa, key_rows(ks_ref, g, kt1), key_rows(vs_ref, g, kt1), b1_ref[g], *sel)
                _online_update(q[g], key_rows(kw_ref, g, kt1), key_rows(vw_ref, g, kt1), b1_ref[g], *win)

            _online_update(qa, key_rows(ks_ref, g, qi), key_rows(vs_ref, g, qi), b0_ref[g], *sel)
            _online_update(q[g], key_rows(kw_ref, g, qi), key_rows(vw_ref, g, qi), b0_ref[g], *win)
            a_s.append(acc_s[g])
            a_w.append(acc_w[g])

    outs = []
    for g in groups:
        o_s = a_s[g][:, :LANES] / a_s[g][:, LANES:]
        o_w = a_w[g][:, :LANES] / a_w[g][:, LANES:]
        gates = gate_ref[0, g]
        for h in range(HPG):
            rows = slice(h * TQ, (h + 1) * TQ)
            g_c = gates[:, h * N_BRANCH + 0:h * N_BRANCH + 1]
            g_s = gates[:, h * N_BRANCH + 1:h * N_BRANCH + 2]
            g_w = gates[:, h * N_BRANCH + 2:h * N_BRANCH + 3]
            o_h = g_c * o_c[g][rows] + g_s * o_s[rows] + g_w * o_w[rows]
            outs.append(o_h[:, :HEAD_DIM])
    o_ref[0] = jnp.concatenate(outs, axis=-1)


def _attn_kernel(flag_ref, *refs):
    use_shift = flag_ref[0] == 1

    @pl.when(use_shift)
    def _():
        _attn_step(True, *refs)

    @pl.when(jnp.logical_not(use_shift))
    def _():
        _attn_step(False, *refs)


def _attn(flag, q, kct, vc, ks, vs, kw, vw, gate, cb, b0, b1, w2, ovlt):
    bsz, _, seq, _ = q.shape
    nq = seq // TQ
    per_batch = lambda b, i: (b, 0, 0, 0)
    k_spec = pl.BlockSpec((1, N_GROUPS, seq, LANES), per_batch)
    v_spec = pl.BlockSpec((1, N_GROUPS, seq, V_LANES), per_batch)
    cmp_spec = pl.BlockSpec((1, N_GROUPS, N_CMP_PAD, LANES), per_batch)
    const = lambda shape: pl.BlockSpec(shape, lambda b, i: (0,) * len(shape),
                                       pipeline_mode=pl.Buffered(1))
    return pl.pallas_call(
        _attn_kernel,
        grid=(bsz, nq),
        in_specs=[pl.BlockSpec(memory_space=pltpu.SMEM),
                  pl.BlockSpec((1, N_HEADS, TQ, LANES), lambda b, i: (b, 0, i, 0)),
                  cmp_spec, cmp_spec, k_spec, v_spec, k_spec, v_spec,
                  pl.BlockSpec((1, N_GROUPS, TQ, LANES), lambda b, i: (b, 0, i, 0)),
                  pl.BlockSpec((N_GROUPS, 1, ROWS, N_CMP_PAD), lambda b, i: (0, i, 0, 0)),
                  const(b0.shape), const(b1.shape), const(w2.shape), const(ovlt.shape)],
        out_specs=pl.BlockSpec((1, TQ, D_ATTN), lambda b, i: (b, i, 0)),
        out_shape=jax.ShapeDtypeStruct((bsz, seq, D_ATTN), F32),
        scratch_shapes=[pltpu.VMEM((N_GROUPS, ROWS, LANES), BF16),
                        pltpu.VMEM((N_GROUPS, ROWS, V_LANES), F32),
                        pltpu.VMEM((N_GROUPS, ROWS, LANES), F32),
                        pltpu.VMEM((N_GROUPS, ROWS, V_LANES), F32),
                        pltpu.VMEM((N_GROUPS, ROWS, LANES), F32)],
        compiler_params=pltpu.CompilerParams(
            dimension_semantics=("arbitrary", "arbitrary"),
            vmem_limit_bytes=VMEM_LIMIT),
        name="attn",
    )(flag, q, kct, vc, ks, vs, kw, vw, gate, cb, b0, b1, w2, ovlt)


TS_OUT = 512
HALO = 16


FRONT_AT = 3


def _outproj_ffn_kernel(tiles_per_seq, ya_ref, bg_ref, u_ref, uh_ref, cw_ref, ga_ref, gc_ref, w_ref,
                        x_ref, modf_ref, g2_ref, w1_ref, w3_ref, w2_ref, modb_ref, o_ref,
                        xa_ref, ha_ref, xb_ref, hb_ref):
    t = pl.program_id(0)
    n_tiles = pl.num_programs(0) - 1
    i_seq = jnp.minimum(t, n_tiles - 1) % tiles_per_seq

    def step(x_dst, h_dst, x_src, h_src):
        hb = h_src[...]
        vals = {}

        def ffn_chunk(c0):
            a = _dot(hb, w1_ref[:, c0:c0 + MXU_N])
            b = _dot(hb, w3_ref[:, c0:c0 + MXU_N])
            part = _dot((_silu(a) * b).astype(BF16), w2_ref[c0:c0 + MXU_N, :])
            vals["ffn"] = part if "ffn" not in vals else vals["ffn"] + part

        def front_conv():
            u = u_ref[0].astype(F32)
            halo = uh_ref[0].astype(F32) * jnp.where(i_seq > 0, 1.0, 0.0)
            h1 = halo[HALO - 1:HALO, :]
            h2 = halo[HALO - 2:HALO - 1, :]
            row = lax.broadcasted_iota(jnp.int32, u.shape, 0)
            u1 = jnp.where(row == 0, h1, pltpu.roll(u, 1, 0))
            u2 = jnp.where(row == 0, h2, jnp.where(row == 1, h1, pltpu.roll(u, 2, 0)))
            conv = u2 * cw_ref[0:1, :] + u1 * cw_ref[1:2, :] + u * cw_ref[2:3, :]
            yc = bg_ref[0].astype(F32) * conv
            yc = yc * lax.rsqrt(jnp.mean(yc * yc, axis=-1, keepdims=True) + EPS) * gc_ref[...]
            vals["yc"] = yc.astype(BF16)

        def front_attn():
            ya = ya_ref[0]
            ya = ya * lax.rsqrt(jnp.mean(ya * ya, axis=-1, keepdims=True) + EPS) * ga_ref[...]
            vals["ya"] = ya.astype(BF16)

        def front_proj():
            proj = _dot(vals["ya"], w_ref[0:D_ATTN, :]) + _dot(vals["yc"], w_ref[D_ATTN:, :])
            vals["x"] = x_ref[0] + modf_ref[0, 2:3, :] * proj

        def front_norm():
            x = vals["x"]
            ms = jnp.mean(x * x, axis=-1, keepdims=True)
            h = x * lax.rsqrt(ms + EPS) * g2_ref[...] * (1.0 + modf_ref[0, 4:5, :]) + modf_ref[0, 3:4, :]
            x_dst[...] = x
            h_dst[...] = h.astype(BF16)

        fronts = [front_conv, front_attn, front_proj, front_norm]
        n_chunks = w1_ref.shape[1] // MXU_N
        assert FRONT_AT + len(fronts) <= n_chunks
        for n in range(n_chunks):
            ffn_chunk(n * MXU_N)
            if FRONT_AT <= n < FRONT_AT + len(fronts):
                fronts[n - FRONT_AT]()
        o_ref[0] = x_src[...] + modb_ref[0, 5:6, :] * vals["ffn"]

    @pl.when(t == 0)
    def _():
        xb_ref[...] = jnp.zeros(xb_ref.shape, F32)
        hb_ref[...] = jnp.zeros(hb_ref.shape, BF16)

    @pl.when(t % 2 == 0)
    def _():
        step(xa_ref, ha_ref, xb_ref, hb_ref)

    @pl.when(t % 2 == 1)
    def _():
        step(xb_ref, hb_ref, xa_ref, ha_ref)


def _outproj_ffn(ya, bg, u, conv_w, ga, gc, w_out, x, mod3, g2, w1, w3, w2):
    bsz, seq, d = x.shape
    ts = TS_OUT
    nt = seq // ts
    n_tiles = bsz * nt
    c2 = lambda t: (0, 0)
    ft = lambda t: jnp.minimum(t, n_tiles - 1)
    bt = lambda t: jnp.maximum(t - 1, 0)
    row_spec = lambda width: pl.BlockSpec((1, ts, width), lambda t: (ft(t) // nt, ft(t) % nt, 0))
    wspec = lambda w: pl.BlockSpec(w.shape, c2, pipeline_mode=pl.Buffered(1))
    return pl.pallas_call(
        functools.partial(_outproj_ffn_kernel, nt),
        grid=(n_tiles + 1,),
        in_specs=[row_spec(D_ATTN), row_spec(D_CONV), row_spec(D_CONV),
                  pl.BlockSpec((1, HALO, D_CONV),
                               lambda t: (ft(t) // nt, jnp.maximum((ft(t) % nt) * (ts // HALO) - 1, 0), 0)),
                  pl.BlockSpec(conv_w.shape, c2),
                  pl.BlockSpec((1, D_ATTN), c2), pl.BlockSpec((1, D_CONV), c2),
                  wspec(w_out),
                  row_spec(d),
                  pl.BlockSpec((1, 6, d), lambda t: (ft(t) // nt, 0, 0)),
                  pl.BlockSpec((1, d), c2),
                  wspec(w1), wspec(w3), wspec(w2),
                  pl.BlockSpec((1, 6, d), lambda t: (bt(t) // nt, 0, 0))],
        out_specs=pl.BlockSpec((1, ts, d), lambda t: (bt(t) // nt, bt(t) % nt, 0)),
        out_shape=jax.ShapeDtypeStruct((bsz, seq, d), F32),
        scratch_shapes=[pltpu.VMEM((ts, d), F32), pltpu.VMEM((ts, d), BF16),
                        pltpu.VMEM((ts, d), F32), pltpu.VMEM((ts, d), BF16)],
        compiler_params=pltpu.CompilerParams(
            dimension_semantics=("arbitrary",), vmem_limit_bytes=VMEM_LIMIT),
        name="outproj_ffn",
    )(ya, bg, u, u, conv_w, ga, gc, w_out, x, mod3, g2, w1, w3, w2, mod3)


def _t5_bucket_np(rel):
    n = np.maximum(rel, 0)
    max_exact = N_BUCKETS // 2
    nf = np.maximum(n, 1).astype(np.float32)
    large = max_exact + (np.log(nf / max_exact) / math.log(MAX_DISTANCE / max_exact)
                         * (N_BUCKETS - max_exact)).astype(np.int32)
    return np.where(n < max_exact, n, np.minimum(large, N_BUCKETS - 1)).astype(np.int32)


def _bias_of_rel(tab_rel, rel):
    idx = jnp.asarray(_t5_bucket_np(rel))
    tab_b = tab_rel.T.reshape((N_HEADS, N_BUCKETS) + (1,) * idx.ndim)
    vals = jnp.zeros((N_HEADS,) + idx.shape, F32)
    for k in range(N_BUCKETS):
        vals = jnp.where(idx == k, tab_b[:, k], vals)
    return jnp.where(jnp.asarray(rel >= 0), vals, NEG_INF)


def _toeplitz_offsets(length, n_cols):
    k = np.arange(length)
    return np.where(k < n_cols, k, k - length)


TOEPLITZ_LEN = 2 * TQ
PATTERN_LEN = 3 * LANES


def _tables_kernel(v0_ref, v1_ref, vp_ref, b0_ref, b1_ref, cb_ref):
    def toeplitz_tile(v_ref):
        rows = jnp.broadcast_to(v_ref[0], (TQ, TOEPLITZ_LEN))
        return pltpu.roll(rows, 0, 1, stride=1, stride_axis=0)[:, :TQ]

    b0_ref[0] = toeplitz_tile(v0_ref)
    b1_ref[0] = toeplitz_tile(v1_ref)
    vp = vp_ref[0]
    shift = TQ // CMP_STRIDE
    pat = jnp.concatenate(
        [vp[:, :2 * N_CMP_PAD]] + [pltpu.roll(vp, a, 1)[:, :2 * N_CMP_PAD] for a in range(1, shift)],
        axis=0)
    for qi in range(cb_ref.shape[1]):
        cb_ref[0, qi] = pat[:, N_CMP_PAD - qi * shift:2 * N_CMP_PAD - qi * shift]


def _bias_tables(rel_bias_table, seq):
    tab_rel = (rel_bias_table - rel_bias_table[N_BUCKETS - 1:N_BUCKETS, :]) * LOG2E
    nq = seq // TQ
    n_cmp = (seq - CMP_BLOCK) // CMP_STRIDE + 1
    shift = TQ // CMP_STRIDE
    assert nq * shift <= N_CMP_PAD and n_cmp * CMP_STRIDE + CMP_BLOCK - 1 > seq
    assert shift + 2 * N_CMP_PAD - 1 <= PATTERN_LEN
    d = _toeplitz_offsets(TOEPLITZ_LEN, TQ)
    v0 = _bias_of_rel(tab_rel, -d)[:, None, :]
    v1 = _bias_of_rel(tab_rel, TQ - d)[:, None, :]
    off = N_CMP_PAD * CMP_STRIDE - (CMP_BLOCK - 1)
    dp = _toeplitz_offsets(PATTERN_LEN, 2 * N_CMP_PAD)[None, :]
    vp = _bias_of_rel(tab_rel, np.arange(CMP_STRIDE)[:, None] + off - CMP_STRIDE * dp)
    head = lambda h: (h // HPG, h % HPG, 0)
    b0, b1, cb = pl.pallas_call(
        _tables_kernel,
        grid=(N_HEADS,),
        in_specs=[pl.BlockSpec((1, 1, TOEPLITZ_LEN), lambda h: (h, 0, 0)),
                  pl.BlockSpec((1, 1, TOEPLITZ_LEN), lambda h: (h, 0, 0)),
                  pl.BlockSpec((1, CMP_STRIDE, PATTERN_LEN), lambda h: (h, 0, 0))],
        out_specs=[pl.BlockSpec((1, TQ, TQ), head), pl.BlockSpec((1, TQ, TQ), head),
                   pl.BlockSpec((1, nq, TQ, N_CMP_PAD), lambda h: (h // HPG, 0, h % HPG, 0))],
        out_shape=[jax.ShapeDtypeStruct((N_GROUPS, ROWS, TQ), F32),
                   jax.ShapeDtypeStruct((N_GROUPS, ROWS, TQ), F32),
                   jax.ShapeDtypeStruct((N_GROUPS, nq, ROWS, N_CMP_PAD), F32)],
        compiler_params=pltpu.CompilerParams(
            dimension_semantics=("arbitrary",), vmem_limit_bytes=VMEM_LIMIT),
        name="bias_tables",
    )(v0, v1, vp)
    i = np.arange(TQ)[:, None]
    jj = np.arange(TQ)[None, :]
    w2 = jnp.asarray(np.tile(np.where(jj > i, 0.0, NEG_INF).astype(np.float32), (HPG, 1)))
    n_sel = seq // SEL_BLOCK
    cs = np.arange(n_cmp) * CMP_STRIDE
    ce = cs + CMP_BLOCK - 1
    ss = np.arange(n_sel) * SEL_BLOCK
    ov = np.clip(np.minimum(ce[:, None], ss[None, :] + SEL_BLOCK - 1)
                 - np.maximum(cs[:, None], ss[None, :]) + 1, 0, None) / CMP_STRIDE
    ovlt = np.zeros((LANES, N_CMP_PAD), np.float32)
    ovlt[SEL_LANE0:SEL_LANE0 + n_sel, :n_cmp] = ov.T
    return tab_rel, cb, b0, b1, w2, jnp.asarray(ovlt, BF16)


def kernel(x, c, w_ada, b_ada, norm1_gain, w_in, q_gain, k_cmp_gain, k_sel_gain, k_win_gain,
           cmp_pos_k, cmp_pos_v, w_ck1, w_ck2, w_cv1, w_cv2, rel_bias_table, conv_w,
           attn_out_gain, conv_out_gain, w_out, norm2_gain, w_ff1, w_ff3, w_ff2):
    bsz, seq, d = x.shape
    assert d == D_MODEL and seq % TQ == 0 and SEL_LANE0 + seq // SEL_BLOCK <= SHIFT_LANE
    assert (seq - CMP_BLOCK) // CMP_STRIDE + 1 <= N_CMP_PAD and seq // CMP_STRIDE == N_CMP_PAD

    seg_of = lambda n: jnp.asarray(np.kron(np.eye(n), np.ones((HEAD_DIM, HEAD_DIM))) / HEAD_DIM, BF16)
    seg4 = seg_of(MXU_N // HEAD_DIM)
    tab_rel, cb, b0, b1, w2m, ovlt = _bias_tables(rel_bias_table, seq)
    bias_max = jnp.max(jnp.abs(tab_rel))
    two = lambda g: jnp.tile(g.reshape(1, HEAD_DIM), (1, 2))
    n_gate = N_BRANCH * HPG
    gate_pad = jnp.zeros((d, LANES - n_gate), BF16)

    for layer in range(w_in.shape[0]):
        wi = w_in[layer].astype(BF16)
        o_g = D_ATTN + 6 * D_KV
        kv = lambda n: wi[:, D_ATTN + n * D_KV:D_ATTN + (n + 1) * D_KV]
        w_p = jnp.concatenate(
            [wi[:, :D_ATTN], kv(0), kv(1), kv(2), kv(4), kv(3), kv(5),
             wi[:, o_g:o_g + n_gate], gate_pad,
             wi[:, o_g + n_gate:o_g + 2 * n_gate], gate_pad, wi[:, o_g + 2 * n_gate:]],
            axis=-1)
        w1k, w1v = w_ck1[layer].astype(BF16), w_cv1[layer].astype(BF16)
        w2k, w2v = w_ck2[layer].astype(BF16), w_cv2[layer].astype(BF16)
        pk2 = cmp_pos_k[layer].reshape(2, CMP_STRIDE * HEAD_DIM)
        pv2 = cmp_pos_v[layer].reshape(2, CMP_STRIDE * HEAD_DIM)

        gq_max = jnp.max(jnp.abs(q_gain[layer]))
        bound = lambda gk: (HEAD_DIM ** 0.5 * LOG2E * 1.01) * gq_max * jnp.max(jnp.abs(gk)) + bias_max
        c_sel, c_win = bound(k_sel_gain[layer]), bound(k_win_gain[layer])
        flag = (2.0 * jnp.maximum(c_sel, c_win) <= MAX_SHIFTED_RANGE).astype(jnp.int32).reshape(1)
        tails = jnp.zeros((2, HEAD_DIM), F32).at[:, SHIFT_LANE - HEAD_DIM].set(
            -jnp.stack([c_sel, c_win]))

        mod3 = _adaln(c, w_ada[layer], b_ada[layer]).reshape(bsz, 6, d)
        q, kvc, ks, vs, kw, vw, gate, bg, u = _inproj(
            x, mod3, norm1_gain[layer].reshape(1, d), w_p, seg4,
            jnp.tile(q_gain[layer].reshape(1, HEAD_DIM), (1, MXU_N // HEAD_DIM)),
            jnp.concatenate([two(k_sel_gain[layer]), two(k_win_gain[layer])], axis=-1), tails)
        kct, vc = _compress(kvc, w1k, w1v, w2k, w2v, pk2, pv2,
                            k_cmp_gain[layer].reshape(1, HEAD_DIM))
        y_attn = _attn(flag, q, kct, vc, ks, vs, kw, vw, gate, cb, b0, b1, w2m, ovlt)
        x = _outproj_ffn(y_attn, bg, u, conv_w[layer], attn_out_gain[layer].reshape(1, D_ATTN),
                         conv_out_gain[layer].reshape(1, D_CONV), w_out[layer].astype(BF16), x, mod3,
                         norm2_gain[layer].reshape(1, d), w_ff1[layer].astype(BF16),
                         w_ff3[layer].astype(BF16), w_ff2[layer].astype(BF16))
    return x
```

```python
import functools
import math

import numpy as np
import jax
import jax.numpy as jnp
from jax import lax
from jax.experimental import pallas as pl
from jax.experimental.pallas import tpu as pltpu

F32 = jnp.float32
BF16 = jnp.bfloat16

D_MODEL = 1024
HEAD_DIM = 64
N_HEADS = 8
N_GROUPS = 2
HPG = N_HEADS // N_GROUPS
D_ATTN = N_HEADS * HEAD_DIM
D_KV = N_GROUPS * HEAD_DIM
N_BRANCH = 3
D_CONV = D_MODEL - D_ATTN
CONV_WIDTH = 3
CMP_BLOCK = 32
CMP_STRIDE = 16
CMP_HIDDEN = 256
SEL_BLOCK = 64
SEL_TOPK = 16
N_LOCAL_FORCED = 2
WINDOW = 512
N_BUCKETS = 32
MAX_DISTANCE = 128
EPS = 1e-6
NEG_INF = -1e30
FORCED_SCORE = 1e6

LANES = 128
SUBLANES = 8
TQ = 256
ROWS = HPG * TQ
N_CMP_PAD = 128
SEL_LANE0 = 64
SHIFT_LANE = 96
V_LANES = 256
LOG2E = math.log2(math.e)
MAX_SHIFTED_RANGE = 100.0
VMEM_LIMIT = 56 * 1024 * 1024

_NT = (((1,), (1,)), ((), ()))


def _dot(a, b):
    return jnp.dot(a, b, preferred_element_type=F32)


def _silu(v):
    return v * (1.0 / (1.0 + jnp.exp(-v)))


def _sigmoid(v):
    return 1.0 / (1.0 + jnp.exp(-v))


def _seg_mean_sq(v, seg):
    sq = v * v
    hi = sq.astype(BF16)
    lo = (sq - hi.astype(F32)).astype(BF16)
    return _dot(hi, seg) + _dot(lo, seg)


TN_ADALN = 1536


def _adaln_kernel(c_ref, w_ref, b_ref, o_ref):
    sc = _silu(c_ref[...]).astype(BF16)
    o_ref[...] = _dot(sc, w_ref[...].astype(BF16)) + b_ref[...]


def _adaln(c, w_ada, b_ada):
    bsz, d = c.shape
    n = w_ada.shape[1]
    tn = TN_ADALN
    return pl.pallas_call(
        _adaln_kernel,
        grid=(n // tn,),
        in_specs=[pl.BlockSpec((bsz, d), lambda j: (0, 0)),
                  pl.BlockSpec((d, tn), lambda j: (0, j)),
                  pl.BlockSpec((1, tn), lambda j: (0, j))],
        out_specs=pl.BlockSpec((bsz, tn), lambda j: (0, j)),
        out_shape=jax.ShapeDtypeStruct((bsz, n), F32),
        compiler_params=pltpu.CompilerParams(
            dimension_semantics=("arbitrary",), vmem_limit_bytes=VMEM_LIMIT),
        name="adaln",
    )(c, w_ada, b_ada.reshape(1, n))


C_Q, C_KVC, C_KK, C_VV, C_GATE, C_BG, C_CG, C_XT, C_END = (
    0, 512, 768, 1024, 1280, 1536, 2048, 2560, 3072)
MXU_N = 256
TS_IN = 1024


def _inproj_kernel(x_ref, mod_ref, g1_ref, w_ref, seg_ref, gq_ref, gkk_ref, tail_ref,
                   q_ref, kvc_ref, ks_ref, vs_ref, kw_ref, vw_ref, gate_ref, bg_ref, u_ref):
    ts = x_ref.shape[1]
    x = x_ref[0]
    ms = jnp.mean(x * x, axis=-1, keepdims=True)
    y = x * lax.rsqrt(ms + EPS) * g1_ref[...]
    h = y * (1.0 + mod_ref[0, 1:2, :]) + mod_ref[0, 0:1, :]
    hb = h.astype(BF16)
    seg = seg_ref[...]

    def proj(c0, c1):
        return _dot(hb, w_ref[:, c0:c1])

    lane = lax.broadcasted_iota(jnp.int32, (ts, HEAD_DIM), 1)
    ones192 = jnp.ones((ts, V_LANES - HEAD_DIM), F32)
    q_tail = jnp.where(lane == SHIFT_LANE - HEAD_DIM, 1.0, 0.0).astype(F32)

    heads_per_tile = MXU_N // HEAD_DIM
    row = pl.program_id(1) * ts + lax.broadcasted_iota(jnp.int32, (ts, HEAD_DIM), 0)
    onehot = jnp.where(lane == row // SEL_BLOCK, 1.0, 0.0).astype(F32) + tail_ref[0:1, :]
    win_tail = jnp.broadcast_to(tail_ref[1:2, :], (ts, HEAD_DIM))

    def do_q(c):
        v = proj(C_Q + c * MXU_N, C_Q + (c + 1) * MXU_N)
        vn = v * lax.rsqrt(_seg_mean_sq(v, seg) + EPS) * gq_ref[...] * (HEAD_DIM ** -0.5 * LOG2E)
        for hh in range(heads_per_tile):
            qh = jnp.concatenate([vn[:, hh * HEAD_DIM:(hh + 1) * HEAD_DIM], q_tail], axis=-1)
            q_ref[0, c * heads_per_tile + hh] = qh.astype(BF16)

    def do_kvc():
        v = proj(C_KVC, C_KK)
        kvc_ref[0, 0] = v[:, :D_KV]
        kvc_ref[0, 1] = v[:, D_KV:]

    def do_kk():
        v = proj(C_KK, C_VV)
        vn = v * lax.rsqrt(_seg_mean_sq(v, seg) + EPS) * gkk_ref[...]
        for g in range(N_GROUPS):
            ks_ref[0, g] = jnp.concatenate(
                [vn[:, g * HEAD_DIM:(g + 1) * HEAD_DIM], onehot], axis=-1).astype(BF16)
            kw_ref[0, g] = jnp.concatenate(
                [vn[:, D_KV + g * HEAD_DIM:D_KV + (g + 1) * HEAD_DIM], win_tail], axis=-1).astype(BF16)

    def do_vv():
        v = proj(C_VV, C_GATE)
        for g in range(N_GROUPS):
            vs_ref[0, g] = jnp.concatenate(
                [v[:, g * HEAD_DIM:(g + 1) * HEAD_DIM], ones192], axis=-1).astype(BF16)
            vw_ref[0, g] = jnp.concatenate(
                [v[:, D_KV + g * HEAD_DIM:D_KV + (g + 1) * HEAD_DIM], ones192], axis=-1).astype(BF16)

    def do_gate():
        v = _sigmoid(proj(C_GATE, C_BG))
        for g in range(N_GROUPS):
            gate_ref[0, g] = v[:, g * LANES:(g + 1) * LANES]

    def do_bg(c):
        cols = slice(c * MXU_N, (c + 1) * MXU_N)
        bg_ref[0, :, cols] = proj(C_BG + c * MXU_N, C_BG + (c + 1) * MXU_N).astype(BF16)

    def do_u(c):
        cols = slice(c * MXU_N, (c + 1) * MXU_N)
        u_ref[0, :, cols] = (proj(C_CG + c * MXU_N, C_CG + (c + 1) * MXU_N)
                             * proj(C_XT + c * MXU_N, C_XT + (c + 1) * MXU_N)).astype(BF16)

    do_q(0)
    do_bg(0)
    do_q(1)
    do_bg(1)
    do_kvc()
    do_u(0)
    do_kk()
    do_u(1)
    do_gate()
    do_vv()


def _inproj(x, mod3, g1, w_p, seg, gq4, gkk4, tails):
    bsz, seq, d = x.shape
    ts = TS_IN
    const2 = lambda b, i: (0, 0)
    k_shape = jax.ShapeDtypeStruct((bsz, N_GROUPS, seq, LANES), BF16)
    k_spec = pl.BlockSpec((1, N_GROUPS, ts, LANES), lambda b, i: (b, 0, i, 0))
    v_shape = jax.ShapeDtypeStruct((bsz, N_GROUPS, seq, V_LANES), BF16)
    v_spec = pl.BlockSpec((1, N_GROUPS, ts, V_LANES), lambda b, i: (b, 0, i, 0))
    return pl.pallas_call(
        _inproj_kernel,
        grid=(bsz, seq // ts),
        in_specs=[pl.BlockSpec((1, ts, d), lambda b, i: (b, i, 0)),
                  pl.BlockSpec((1, 6, d), lambda b, i: (b, 0, 0)),
                  pl.BlockSpec((1, d), const2),
                  pl.BlockSpec(w_p.shape, const2, pipeline_mode=pl.Buffered(1)),
                  pl.BlockSpec(seg.shape, const2),
                  pl.BlockSpec(gq4.shape, const2),
                  pl.BlockSpec(gkk4.shape, const2),
                  pl.BlockSpec(tails.shape, const2)],
        out_specs=[pl.BlockSpec((1, N_HEADS, ts, LANES), lambda b, i: (b, 0, i, 0)),
                   pl.BlockSpec((1, 2, ts, D_KV), lambda b, i: (b, 0, i, 0)),
                   k_spec, v_spec, k_spec, v_spec,
                   pl.BlockSpec((1, N_GROUPS, ts, LANES), lambda b, i: (b, 0, i, 0)),
                   pl.BlockSpec((1, ts, D_CONV), lambda b, i: (b, i, 0)),
                   pl.BlockSpec((1, ts, D_CONV), lambda b, i: (b, i, 0))],
        out_shape=[jax.ShapeDtypeStruct((bsz, N_HEADS, seq, LANES), BF16),
                   jax.ShapeDtypeStruct((bsz, 2, seq, D_KV), F32),
                   k_shape, v_shape, k_shape, v_shape,
                   jax.ShapeDtypeStruct((bsz, N_GROUPS, seq, LANES), F32),
                   jax.ShapeDtypeStruct((bsz, seq, D_CONV), BF16),
                   jax.ShapeDtypeStruct((bsz, seq, D_CONV), BF16)],
        compiler_params=pltpu.CompilerParams(
            dimension_semantics=("arbitrary", "arbitrary"), vmem_limit_bytes=VMEM_LIMIT),
        name="inproj",
    )(x, mod3, g1, w_p, seg, gq4, gkk4, tails)


def _compress_kernel(kvc_ref, w1k_ref, w1v_ref, w2k_ref, w2v_ref, pk_ref, pv_ref, gk_ref,
                     kct_ref, vc_ref):
    n_rows = kvc_ref.shape[2] // CMP_STRIDE
    half = CMP_STRIDE * HEAD_DIM
    zlane = jnp.zeros((n_rows, HEAD_DIM), F32)

    def branch(which, w1_ref, w2_ref, pos_ref):
        rows = [kvc_ref[0, which, pl.ds(r, n_rows, stride=CMP_STRIDE), :] for r in range(CMP_STRIDE)]
        outs = []
        for g in range(N_GROUPS):
            x = jnp.concatenate([rw[:, g * HEAD_DIM:(g + 1) * HEAD_DIM] for rw in rows], axis=-1)
            a1 = (x + pos_ref[0:1, :]).astype(BF16)
            a2 = (x + pos_ref[1:2, :]).astype(BF16)
            hid = _dot(a1, w1_ref[0:half, :]) + pltpu.roll(_dot(a2, w1_ref[half:, :]), n_rows - 1, 0)
            outs.append(_dot(_silu(hid).astype(BF16), w2_ref[...]))
        return outs

    kc = branch(0, w1k_ref, w2k_ref, pk_ref)
    vc = branch(1, w1v_ref, w2v_ref, pv_ref)
    for g in range(N_GROUPS):
        kn = kc[g] * lax.rsqrt(jnp.mean(kc[g] * kc[g], axis=-1, keepdims=True) + EPS) * gk_ref[...]
        kct_ref[0, g] = jnp.concatenate([kn, zlane], axis=-1).T.astype(BF16)
        vc_ref[0, g] = jnp.concatenate([vc[g], zlane], axis=-1).astype(BF16)


def _compress(kvc, w1k, w1v, w2k, w2v, pk2, pv2, gk):
    bsz, _, seq, _ = kvc.shape
    c2 = lambda b: (0, 0)
    out_shape = jax.ShapeDtypeStruct((bsz, N_GROUPS, N_CMP_PAD, LANES), BF16)
    out_spec = pl.BlockSpec((1, N_GROUPS, N_CMP_PAD, LANES), lambda b: (b, 0, 0, 0))
    return pl.pallas_call(
        _compress_kernel,
        grid=(bsz,),
        in_specs=[pl.BlockSpec((1, 2, seq, D_KV), lambda b: (b, 0, 0, 0)),
                  pl.BlockSpec(w1k.shape, c2), pl.BlockSpec(w1v.shape, c2),
                  pl.BlockSpec(w2k.shape, c2), pl.BlockSpec(w2v.shape, c2),
                  pl.BlockSpec(pk2.shape, c2), pl.BlockSpec(pv2.shape, c2),
                  pl.BlockSpec(gk.shape, c2)],
        out_specs=[out_spec, out_spec],
        out_shape=[out_shape, out_shape],
        compiler_params=pltpu.CompilerParams(
            dimension_semantics=("arbitrary",), vmem_limit_bytes=VMEM_LIMIT),
        name="compress",
    )(kvc, w1k, w1v, w2k, w2v, pk2, pv2, gk)


def _logits(qa, k_rows, bias):
    s = lax.dot_general(qa, k_rows, _NT, preferred_element_type=F32)
    return s if bias is None else s + bias


def _shifted_pv(qa, k_rows, v_rows, bias):
    return _dot(jnp.exp2(_logits(qa, k_rows, bias)).astype(BF16), v_rows)


def _online_update(qa, k_tile, v_tile, bias, acc_ref, m_ref):
    s = _logits(qa, k_tile, bias)
    m_prev = m_ref[...]
    m_new = jnp.maximum(m_prev, jnp.max(s, axis=-1, keepdims=True))
    alpha = jnp.exp2(m_prev - m_new)
    p = jnp.exp2(s - jnp.concatenate([m_new] * (TQ // LANES), axis=-1))
    acc_ref[...] = (acc_ref[...] * jnp.concatenate([alpha] * (V_LANES // LANES), axis=-1)
                    + _dot(p.astype(BF16), v_tile))
    m_ref[...] = m_new


def _selection_bias(pc, qi, ovlt, n_sel):
    psum = pc[0:TQ] + pc[TQ:2 * TQ] + pc[2 * TQ:3 * TQ] + pc[3 * TQ:4 * TQ]
    hi = psum.astype(BF16)
    lo = (psum - hi.astype(F32)).astype(BF16)
    pslc_t = (lax.dot_general(ovlt, hi, _NT, preferred_element_type=F32)
              + lax.dot_general(ovlt, lo, _NT, preferred_element_type=F32))
    top_k = min(SEL_TOPK, n_sel)
    sub = SUBLANES
    j = lax.broadcasted_iota(jnp.int32, (n_sel, TQ), 0)
    t = qi * TQ + lax.broadcasted_iota(jnp.int32, (n_sel, TQ), 1)
    dist = jnp.right_shift(t, int(math.log2(SEL_BLOCK))) - j
    score = jnp.where(dist < N_LOCAL_FORCED, FORCED_SCORE, pslc_t[SEL_LANE0:SEL_LANE0 + n_sel, :])
    score = jnp.where(j == 0, FORCED_SCORE, score)
    score = jnp.where(dist >= 0, score, NEG_INF)
    groups = [score[a * sub:(a + 1) * sub] for a in range(n_sel // sub)]
    cnts = [jnp.zeros((sub, TQ), F32) for _ in groups]
    j_sub = lax.broadcasted_iota(jnp.int32, (sub, TQ), 0)
    for i in range(n_sel):
        row = jnp.broadcast_to(score[i:i + 1, :], (sub, TQ))
        for a, grp in enumerate(groups):
            if a * sub > i:
                beats = jnp.where(row >= grp, 1.0, 0.0)
            elif a * sub + sub - 1 <= i:
                beats = jnp.where(row > grp, 1.0, 0.0)
            else:
                beats = jnp.where(j_sub + a * sub > i, jnp.where(row >= grp, 1.0, 0.0),
                                  jnp.where(row > grp, 1.0, 0.0))
            cnts[a] = cnts[a] + beats
    sel_t = jnp.where(jnp.concatenate(cnts, axis=0) < top_k, 0.0, NEG_INF)
    return jnp.concatenate(
        [jnp.zeros((SEL_LANE0, TQ), F32), sel_t,
         jnp.zeros((LANES - SEL_LANE0 - n_sel, TQ), F32)], axis=0).T


def _attn_step(shifted, q_ref, kct_ref, vc_ref, ks_ref, vs_ref, kw_ref, vw_ref, gate_ref,
               cb_ref, b0_ref, b1_ref, w2_ref, ovlt_ref, o_ref, qa_ref, oc_ref, acc_s, m_s, acc_w, m_w):
    qi = pl.program_id(1)
    groups = range(N_GROUPS)
    q = [q_ref[0, g * HPG:(g + 1) * HPG].reshape(ROWS, LANES) for g in groups]

    def key_rows(ref, g, kt, n_tiles=1):
        return ref[0, g, pl.ds(pl.multiple_of(kt * TQ, TQ), n_tiles * TQ), :]

    def compressed_and_selection(which=groups):
        o_c = []
        for g in which:
            s = _dot(q[g], kct_ref[0, g]) + cb_ref[g, 0]
            m = jnp.max(s, axis=-1, keepdims=True)
            e = jnp.exp2(s - m)
            l = jnp.sum(e, axis=-1, keepdims=True)
            pc = e * jnp.where(m > 0.5 * NEG_INF, 1.0 / l, 0.0)
            o_c.append(_dot(pc.astype(BF16), vc_ref[0, g]))
            selbias = _selection_bias(pc, qi, ovlt_ref[...], ks_ref.shape[2] // SEL_BLOCK)
            for h in range(HPG):
                qa_ref[g, h * TQ:(h + 1) * TQ, :] = (
                    q_ref[0, g * HPG + h].astype(F32) + selbias).astype(BF16)
        return o_c

    kt1 = jnp.maximum(qi - 1, 0)
    kt2 = jnp.maximum(qi - 2, 0)
    def finish(a_s, a_w, o_c):
        outs = []
        for g in groups:
            o_s = a_s[g][:, :LANES] / a_s[g][:, LANES:]
            o_w = a_w[g][:, :LANES] / a_w[g][:, LANES:]
            gates = gate_ref[0, g]
            for h in range(HPG):
                rows = slice(h * TQ, (h + 1) * TQ)
                g_c = gates[:, h * N_BRANCH + 0:h * N_BRANCH + 1]
                g_s = gates[:, h * N_BRANCH + 1:h * N_BRANCH + 2]
                g_w = gates[:, h * N_BRANCH + 2:h * N_BRANCH + 3]
                o_h = g_c * o_c[g][rows] + g_s * o_s[rows] + g_w * o_w[rows]
                outs.append(o_h[:, :HEAD_DIM])
        o_ref[0] = jnp.concatenate(outs, axis=-1)

    if shifted:
        def window_tiles(g, n_win):
            acc = _shifted_pv(q[g], key_rows(kw_ref, g, qi), key_rows(vw_ref, g, qi), b0_ref[g])
            if n_win >= 2:
                acc += _shifted_pv(q[g], key_rows(kw_ref, g, kt1), key_rows(vw_ref, g, kt1), b1_ref[g])
            if n_win >= 3:
                acc += _shifted_pv(q[g], key_rows(kw_ref, g, kt2), key_rows(vw_ref, g, kt2), w2_ref[...])
            return acc

        def first_block(n_win):
            acc_s[...] = jnp.zeros(acc_s.shape, F32)
            for g in groups:
                acc_w[g] = window_tiles(g, n_win)
                oc_ref[g] = compressed_and_selection([g])[0]

        pl.when(qi == 0)(functools.partial(first_block, 1))
        pl.when(qi == 1)(functools.partial(first_block, 2))
        pl.when(qi >= 2)(functools.partial(first_block, 3))

        n_plain = jnp.maximum(qi - 1, 0)

        def pair_body(it, carry):
            pv = [None] * N_GROUPS
            for half in range(2):
                for g in groups:
                    t = _shifted_pv(qa_ref[g], key_rows(ks_ref, g, 2 * it + half),
                                    key_rows(vs_ref, g, 2 * it + half), None)
                    pv[g] = t if pv[g] is None else pv[g] + t
            for g in groups:
                acc_s[g] += pv[g]
            return carry

        lax.fori_loop(0, n_plain // 2, pair_body, 0)

        def last_block(with_prev, with_odd):
            a_s = []
            for g in groups:
                acc = acc_s[g] + _shifted_pv(qa_ref[g], key_rows(ks_ref, g, qi), key_rows(vs_ref, g, qi),
                                             b0_ref[g])
                if with_prev:
                    acc += _shifted_pv(qa_ref[g], key_rows(ks_ref, g, kt1), key_rows(vs_ref, g, kt1),
                                       b1_ref[g])
                if with_odd:
                    acc += _shifted_pv(qa_ref[g], key_rows(ks_ref, g, kt2), key_rows(vs_ref, g, kt2), None)
                a_s.append(acc)
            finish(a_s, [acc_w[g] for g in groups], [oc_ref[g] for g in groups])

        is_odd = n_plain % 2 == 1
        pl.when(qi == 0)(functools.partial(last_block, False, False))
        pl.when((qi >= 1) & jnp.logical_not(is_odd))(functools.partial(last_block, True, False))
        pl.when(is_odd)(functools.partial(last_block, True, True))
    else:
        o_c = compressed_and_selection()
        a_s, a_w = [], []
        for g in groups:
            qa = qa_ref[g]
            sel = (acc_s.at[g], m_s.at[g])
            win = (acc_w.at[g], m_w.at[g])
            for acc, m_ref in (sel, win):
                acc[...] = jnp.zeros(acc.shape, F32)
                m_ref[...] = jnp.full(m_ref.shape, NEG_INF, F32)

            def sel_body(kt, carry, g=g, qa=qa, sel=sel):
                _online_update(qa, key_rows(ks_ref, g, kt), key_rows(vs_ref, g, kt), None, *sel)
                return carry

            lax.fori_loop(0, qi - 1, sel_body, 0)

            @pl.when(qi >= 2)
            def _(g=g, win=win):
                _online_update(q[g], key_rows(kw_ref, g, kt2), key_rows(vw_ref, g, kt2), w2_ref[...], *win)

            @pl.when(qi >= 1)
            def _(g=g, qa=qa, sel=sel, win=win):
                _online_update(qa, key_rows(ks_ref, g, kt1), key_rows(vs_ref, g, kt1), b1_ref[g], *sel)
                _online_update(q[g], key_rows(kw_ref, g, kt1), key_rows(vw_ref, g, kt1), b1_ref[g], *win)

            _online_update(qa, key_rows(ks_ref, g, qi), key_rows(vs_ref, g, qi), b0_ref[g], *sel)
            _online_update(q[g], key_rows(kw_ref, g, qi), key_rows(vw_ref, g, qi), b0_ref[g], *win)
            a_s.append(acc_s[g])
            a_w.append(acc_w[g])

        finish(a_s, a_w, o_c)


def _attn_kernel(flag_ref, *refs):
    use_shift = flag_ref[0] == 1

    @pl.when(use_shift)
    def _():
        _attn_step(True, *refs)

    @pl.when(jnp.logical_not(use_shift))
    def _():
        _attn_step(False, *refs)


def _attn(flag, q, kct, vc, ks, vs, kw, vw, gate, cb, b0, b1, w2, ovlt):
    bsz, _, seq, _ = q.shape
    nq = seq // TQ
    per_batch = lambda b, i: (b, 0, 0, 0)
    k_spec = pl.BlockSpec((1, N_GROUPS, seq, LANES), per_batch)
    v_spec = pl.BlockSpec((1, N_GROUPS, seq, V_LANES), per_batch)
    cmp_spec = pl.BlockSpec((1, N_GROUPS, N_CMP_PAD, LANES), per_batch)
    const = lambda shape: pl.BlockSpec(shape, lambda b, i: (0,) * len(shape),
                                       pipeline_mode=pl.Buffered(1))
    return pl.pallas_call(
        _attn_kernel,
        grid=(bsz, nq),
        in_specs=[pl.BlockSpec(memory_space=pltpu.SMEM),
                  pl.BlockSpec((1, N_HEADS, TQ, LANES), lambda b, i: (b, 0, i, 0)),
                  cmp_spec, cmp_spec, k_spec, v_spec, k_spec, v_spec,
                  pl.BlockSpec((1, N_GROUPS, TQ, LANES), lambda b, i: (b, 0, i, 0)),
                  pl.BlockSpec((N_GROUPS, 1, ROWS, N_CMP_PAD), lambda b, i: (0, i, 0, 0)),
                  const(b0.shape), const(b1.shape), const(w2.shape), const(ovlt.shape)],
        out_specs=pl.BlockSpec((1, TQ, D_ATTN), lambda b, i: (b, i, 0)),
        out_shape=jax.ShapeDtypeStruct((bsz, seq, D_ATTN), F32),
        scratch_shapes=[pltpu.VMEM((N_GROUPS, ROWS, LANES), BF16),
                        pltpu.VMEM((N_GROUPS, ROWS, LANES), F32),
                        pltpu.VMEM((N_GROUPS, ROWS, V_LANES), F32),
                        pltpu.VMEM((N_GROUPS, ROWS, LANES), F32),
                        pltpu.VMEM((N_GROUPS, ROWS, V_LANES), F32),
                        pltpu.VMEM((N_GROUPS, ROWS, LANES), F32)],
        compiler_params=pltpu.CompilerParams(
            dimension_semantics=("arbitrary", "arbitrary"),
            vmem_limit_bytes=VMEM_LIMIT),
        name="attn",
    )(flag, q, kct, vc, ks, vs, kw, vw, gate, cb, b0, b1, w2, ovlt)


TS_OUT = 512
HALO = 16


def _outproj_ffn_kernel(ya_ref, bg_ref, u_ref, uh_ref, cw_ref, ga_ref, gc_ref, w_ref, x_ref, mod_ref,
                        g2_ref, w1_ref, w3_ref, w2_ref, o_ref):
    u = u_ref[0].astype(F32)
    halo = uh_ref[0].astype(F32) * jnp.where(pl.program_id(1) > 0, 1.0, 0.0)
    h1 = halo[HALO - 1:HALO, :]
    h2 = halo[HALO - 2:HALO - 1, :]
    row = lax.broadcasted_iota(jnp.int32, u.shape, 0)
    u1 = jnp.where(row == 0, h1, pltpu.roll(u, 1, 0))
    u2 = jnp.where(row == 0, h2, jnp.where(row == 1, h1, pltpu.roll(u, 2, 0)))
    conv = u2 * cw_ref[0:1, :] + u1 * cw_ref[1:2, :] + u * cw_ref[2:3, :]
    yc = bg_ref[0].astype(F32) * conv
    yc = yc * lax.rsqrt(jnp.mean(yc * yc, axis=-1, keepdims=True) + EPS) * gc_ref[...]
    ya = ya_ref[0]
    ya = ya * lax.rsqrt(jnp.mean(ya * ya, axis=-1, keepdims=True) + EPS) * ga_ref[...]
    proj = _dot(ya.astype(BF16), w_ref[0:D_ATTN, :]) + _dot(yc.astype(BF16), w_ref[D_ATTN:, :])
    x = x_ref[0] + mod_ref[0, 2:3, :] * proj

    ms = jnp.mean(x * x, axis=-1, keepdims=True)
    h = x * lax.rsqrt(ms + EPS) * g2_ref[...] * (1.0 + mod_ref[0, 4:5, :]) + mod_ref[0, 3:4, :]
    hb = h.astype(BF16)
    a = _dot(hb, w1_ref[...])
    b = _dot(hb, w3_ref[...])
    act = (_silu(a) * b).astype(BF16)
    o_ref[0] = x + mod_ref[0, 5:6, :] * _dot(act, w2_ref[...])


def _outproj_ffn(ya, bg, u, conv_w, ga, gc, w_out, x, mod3, g2, w1, w3, w2):
    bsz, seq, d = x.shape
    ts = TS_OUT
    c2 = lambda b, i: (0, 0)
    row_spec = lambda width: pl.BlockSpec((1, ts, width), lambda b, i: (b, i, 0))
    wspec = lambda w: pl.BlockSpec(w.shape, c2, pipeline_mode=pl.Buffered(1))
    return pl.pallas_call(
        _outproj_ffn_kernel,
        grid=(bsz, seq // ts),
        in_specs=[row_spec(D_ATTN), row_spec(D_CONV), row_spec(D_CONV),
                  pl.BlockSpec((1, HALO, D_CONV),
                               lambda b, i: (b, jnp.maximum(i * (ts // HALO) - 1, 0), 0)),
                  pl.BlockSpec(conv_w.shape, c2),
                  pl.BlockSpec((1, D_ATTN), c2), pl.BlockSpec((1, D_CONV), c2),
                  wspec(w_out),
                  row_spec(d),
                  pl.BlockSpec((1, 6, d), lambda b, i: (b, 0, 0)),
                  pl.BlockSpec((1, d), c2),
                  wspec(w1), wspec(w3), wspec(w2)],
        out_specs=row_spec(d),
        out_shape=jax.ShapeDtypeStruct((bsz, seq, d), F32),
        compiler_params=pltpu.CompilerParams(
            dimension_semantics=("arbitrary", "arbitrary"), vmem_limit_bytes=VMEM_LIMIT),
        name="outproj_ffn",
    )(ya, bg, u, u, conv_w, ga, gc, w_out, x, mod3, g2, w1, w3, w2)


def _t5_bucket_np(rel):
    n = np.maximum(rel, 0)
    max_exact = N_BUCKETS // 2
    nf = np.maximum(n, 1).astype(np.float32)
    large = max_exact + (np.log(nf / max_exact) / math.log(MAX_DISTANCE / max_exact)
                         * (N_BUCKETS - max_exact)).astype(np.int32)
    return np.where(n < max_exact, n, np.minimum(large, N_BUCKETS - 1)).astype(np.int32)


def _bias_of_rel(tab_rel, rel):
    idx = jnp.asarray(_t5_bucket_np(rel))
    tab_b = tab_rel.T.reshape((N_HEADS, N_BUCKETS) + (1,) * idx.ndim)
    vals = jnp.zeros((N_HEADS,) + idx.shape, F32)
    for k in range(N_BUCKETS):
        vals = jnp.where(idx == k, tab_b[:, k], vals)
    return jnp.where(jnp.asarray(rel >= 0), vals, NEG_INF)


def _toeplitz_offsets(length, n_cols):
    k = np.arange(length)
    return np.where(k < n_cols, k, k - length)


TOEPLITZ_LEN = 2 * TQ
PATTERN_LEN = 3 * LANES


def _tables_kernel(v0_ref, v1_ref, vp_ref, b0_ref, b1_ref, cb_ref):
    def toeplitz_tile(v_ref):
        rows = jnp.broadcast_to(v_ref[0], (TQ, TOEPLITZ_LEN))
        return pltpu.roll(rows, 0, 1, stride=1, stride_axis=0)[:, :TQ]

    b0_ref[0] = toeplitz_tile(v0_ref)
    b1_ref[0] = toeplitz_tile(v1_ref)
    vp = vp_ref[0]
    shift = TQ // CMP_STRIDE
    pat = jnp.concatenate(
        [vp[:, :2 * N_CMP_PAD]] + [pltpu.roll(vp, a, 1)[:, :2 * N_CMP_PAD] for a in range(1, shift)],
        axis=0)
    for qi in range(cb_ref.shape[1]):
        cb_ref[0, qi] = pat[:, N_CMP_PAD - qi * shift:2 * N_CMP_PAD - qi * shift]


def _bias_tables(rel_bias_table, seq):
    tab_rel = (rel_bias_table - rel_bias_table[N_BUCKETS - 1:N_BUCKETS, :]) * LOG2E
    nq = seq // TQ
    n_cmp = (seq - CMP_BLOCK) // CMP_STRIDE + 1
    shift = TQ // CMP_STRIDE
    assert nq * shift <= N_CMP_PAD and n_cmp * CMP_STRIDE + CMP_BLOCK - 1 > seq
    assert shift + 2 * N_CMP_PAD - 1 <= PATTERN_LEN
    d = _toeplitz_offsets(TOEPLITZ_LEN, TQ)
    v0 = _bias_of_rel(tab_rel, -d)[:, None, :]
    v1 = _bias_of_rel(tab_rel, TQ - d)[:, None, :]
    off = N_CMP_PAD * CMP_STRIDE - (CMP_BLOCK - 1)
    dp = _toeplitz_offsets(PATTERN_LEN, 2 * N_CMP_PAD)[None, :]
    vp = _bias_of_rel(tab_rel, np.arange(CMP_STRIDE)[:, None] + off - CMP_STRIDE * dp)
    head = lambda h: (h // HPG, h % HPG, 0)
    b0, b1, cb = pl.pallas_call(
        _tables_kernel,
        grid=(N_HEADS,),
        in_specs=[pl.BlockSpec((1, 1, TOEPLITZ_LEN), lambda h: (h, 0, 0)),
                  pl.BlockSpec((1, 1, TOEPLITZ_LEN), lambda h: (h, 0, 0)),
                  pl.BlockSpec((1, CMP_STRIDE, PATTERN_LEN), lambda h: (h, 0, 0))],
        out_specs=[pl.BlockSpec((1, TQ, TQ), head), pl.BlockSpec((1, TQ, TQ), head),
                   pl.BlockSpec((1, nq, TQ, N_CMP_PAD), lambda h: (h // HPG, 0, h % HPG, 0))],
        out_shape=[jax.ShapeDtypeStruct((N_GROUPS, ROWS, TQ), F32),
                   jax.ShapeDtypeStruct((N_GROUPS, ROWS, TQ), F32),
                   jax.ShapeDtypeStruct((N_GROUPS, nq, ROWS, N_CMP_PAD), F32)],
        compiler_params=pltpu.CompilerParams(
            dimension_semantics=("arbitrary",), vmem_limit_bytes=VMEM_LIMIT),
        name="bias_tables",
    )(v0, v1, vp)
    i = np.arange(TQ)[:, None]
    jj = np.arange(TQ)[None, :]
    w2 = jnp.asarray(np.tile(np.where(jj > i, 0.0, NEG_INF).astype(np.float32), (HPG, 1)))
    n_sel = seq // SEL_BLOCK
    cs = np.arange(n_cmp) * CMP_STRIDE
    ce = cs + CMP_BLOCK - 1
    ss = np.arange(n_sel) * SEL_BLOCK
    ov = np.clip(np.minimum(ce[:, None], ss[None, :] + SEL_BLOCK - 1)
                 - np.maximum(cs[:, None], ss[None, :]) + 1, 0, None) / CMP_STRIDE
    ovlt = np.zeros((LANES, N_CMP_PAD), np.float32)
    ovlt[SEL_LANE0:SEL_LANE0 + n_sel, :n_cmp] = ov.T
    return tab_rel, cb, b0, b1, w2, jnp.asarray(ovlt, BF16)


def kernel(x, c, w_ada, b_ada, norm1_gain, w_in, q_gain, k_cmp_gain, k_sel_gain, k_win_gain,
           cmp_pos_k, cmp_pos_v, w_ck1, w_ck2, w_cv1, w_cv2, rel_bias_table, conv_w,
           attn_out_gain, conv_out_gain, w_out, norm2_gain, w_ff1, w_ff3, w_ff2):
    bsz, seq, d = x.shape
    assert d == D_MODEL and seq % TQ == 0 and SEL_LANE0 + seq // SEL_BLOCK <= SHIFT_LANE
    assert (seq - CMP_BLOCK) // CMP_STRIDE + 1 <= N_CMP_PAD and seq // CMP_STRIDE == N_CMP_PAD

    seg_of = lambda n: jnp.asarray(np.kron(np.eye(n), np.ones((HEAD_DIM, HEAD_DIM))) / HEAD_DIM, BF16)
    seg4 = seg_of(MXU_N // HEAD_DIM)
    tab_rel, cb, b0, b1, w2m, ovlt = _bias_tables(rel_bias_table, seq)
    bias_max = jnp.max(jnp.abs(tab_rel))
    two = lambda g: jnp.tile(g.reshape(1, HEAD_DIM), (1, 2))
    n_gate = N_BRANCH * HPG
    gate_pad = jnp.zeros((d, LANES - n_gate), BF16)

    for layer in range(w_in.shape[0]):
        wi = w_in[layer].astype(BF16)
        o_g = D_ATTN + 6 * D_KV
        kv = lambda n: wi[:, D_ATTN + n * D_KV:D_ATTN + (n + 1) * D_KV]
        w_p = jnp.concatenate(
            [wi[:, :D_ATTN], kv(0), kv(1), kv(2), kv(4), kv(3), kv(5),
             wi[:, o_g:o_g + n_gate], gate_pad,
             wi[:, o_g + n_gate:o_g + 2 * n_gate], gate_pad, wi[:, o_g + 2 * n_gate:]],
            axis=-1)
        w1k, w1v = w_ck1[layer].astype(BF16), w_cv1[layer].astype(BF16)
        w2k, w2v = w_ck2[layer].astype(BF16), w_cv2[layer].astype(BF16)
        pk2 = cmp_pos_k[layer].reshape(2, CMP_STRIDE * HEAD_DIM)
        pv2 = cmp_pos_v[layer].reshape(2, CMP_STRIDE * HEAD_DIM)

        gq_max = jnp.max(jnp.abs(q_gain[layer]))
        bound = lambda gk: (HEAD_DIM ** 0.5 * LOG2E * 1.01) * gq_max * jnp.max(jnp.abs(gk)) + bias_max
        c_sel, c_win = bound(k_sel_gain[layer]), bound(k_win_gain[layer])
        flag = (2.0 * jnp.maximum(c_sel, c_win) <= MAX_SHIFTED_RANGE).astype(jnp.int32).reshape(1)
        tails = jnp.zeros((2, HEAD_DIM), F32).at[:, SHIFT_LANE - HEAD_DIM].set(
            -jnp.stack([c_sel, c_win]))

        mod3 = _adaln(c, w_ada[layer], b_ada[layer]).reshape(bsz, 6, d)
        q, kvc, ks, vs, kw, vw, gate, bg, u = _inproj(
            x, mod3, norm1_gain[layer].reshape(1, d), w_p, seg4,
            jnp.tile(q_gain[layer].reshape(1, HEAD_DIM), (1, MXU_N // HEAD_DIM)),
            jnp.concatenate([two(k_sel_gain[layer]), two(k_win_gain[layer])], axis=-1), tails)
        kct, vc = _compress(kvc, w1k, w1v, w2k, w2v, pk2, pv2,
                            k_cmp_gain[layer].reshape(1, HEAD_DIM))
        y_attn = _attn(flag, q, kct, vc, ks, vs, kw, vw, gate, cb, b0, b1, w2m, ovlt)
        x = _outproj_ffn(y_attn, bg, u, conv_w[layer], attn_out_gain[layer].reshape(1, D_ATTN),
                         conv_out_gain[layer].reshape(1, D_CONV), w_out[layer].astype(BF16), x, mod3,
                         norm2_gain[layer].reshape(1, d), w_ff1[layer].astype(BF16),
                         w_ff3[layer].astype(BF16), w_ff2[layer].astype(BF16))
    return x
```

```python
import functools
import math

import numpy as np
import jax
import jax.numpy as jnp
from jax import lax
from jax.experimental import pallas as pl
from jax.experimental.pallas import tpu as pltpu

F32 = jnp.float32
BF16 = jnp.bfloat16

D_MODEL = 1024
HEAD_DIM = 64
N_HEADS = 8
N_GROUPS = 2
HPG = N_HEADS // N_GROUPS
D_ATTN = N_HEADS * HEAD_DIM
D_KV = N_GROUPS * HEAD_DIM
N_BRANCH = 3
D_CONV = D_MODEL - D_ATTN
CONV_WIDTH = 3
CMP_BLOCK = 32
CMP_STRIDE = 16
CMP_HIDDEN = 256
SEL_BLOCK = 64
SEL_TOPK = 16
N_LOCAL_FORCED = 2
WINDOW = 512
N_BUCKETS = 32
MAX_DISTANCE = 128
EPS = 1e-6
NEG_INF = -1e30
FORCED_SCORE = 1e6

LANES = 128
SUBLANES = 8
TQ = 256
ROWS = HPG * TQ
N_CMP_PAD = 128
SEL_LANE0 = 64
SHIFT_LANE = 96
V_LANES = 256
LOG2E = math.log2(math.e)
MAX_SHIFTED_RANGE = 100.0
VMEM_LIMIT = 56 * 1024 * 1024

_NT = (((1,), (1,)), ((), ()))


def _dot(a, b):
    return jnp.dot(a, b, preferred_element_type=F32)


def _silu(v):
    return v * (1.0 / (1.0 + jnp.exp(-v)))


def _sigmoid(v):
    return 1.0 / (1.0 + jnp.exp(-v))


def _seg_mean_sq(v, seg):
    sq = v * v
    hi = sq.astype(BF16)
    lo = (sq - hi.astype(F32)).astype(BF16)
    return _dot(hi, seg) + _dot(lo, seg)


TN_ADALN = 1536


def _adaln_kernel(c_ref, w_ref, b_ref, o_ref):
    sc = _silu(c_ref[...]).astype(BF16)
    o_ref[...] = _dot(sc, w_ref[...].astype(BF16)) + b_ref[...]


def _adaln(c, w_ada, b_ada):
    bsz, d = c.shape
    n = w_ada.shape[1]
    tn = TN_ADALN
    return pl.pallas_call(
        _adaln_kernel,
        grid=(n // tn,),
        in_specs=[pl.BlockSpec((bsz, d), lambda j: (0, 0)),
                  pl.BlockSpec((d, tn), lambda j: (0, j)),
                  pl.BlockSpec((1, tn), lambda j: (0, j))],
        out_specs=pl.BlockSpec((bsz, tn), lambda j: (0, j)),
        out_shape=jax.ShapeDtypeStruct((bsz, n), F32),
        compiler_params=pltpu.CompilerParams(
            dimension_semantics=("arbitrary",), vmem_limit_bytes=VMEM_LIMIT),
        name="adaln",
    )(c, w_ada, b_ada.reshape(1, n))


C_Q, C_KVC, C_KK, C_VV, C_GATE, C_BG, C_CG, C_XT, C_END = (
    0, 512, 768, 1024, 1280, 1536, 2048, 2560, 3072)
MXU_N = 256
TS_IN = 1024


def _inproj_kernel(x_ref, mod_ref, g1_ref, w_ref, seg_ref, gq_ref, gkk_ref, tail_ref,
                   q_ref, kvc_ref, ks_ref, vs_ref, kw_ref, vw_ref, gate_ref, bg_ref, u_ref):
    ts = x_ref.shape[1]
    x = x_ref[0]
    ms = jnp.mean(x * x, axis=-1, keepdims=True)
    y = x * lax.rsqrt(ms + EPS) * g1_ref[...]
    h = y * (1.0 + mod_ref[0, 1:2, :]) + mod_ref[0, 0:1, :]
    hb = h.astype(BF16)
    seg = seg_ref[...]

    def proj(c0, c1):
        return _dot(hb, w_ref[:, c0:c1])

    lane = lax.broadcasted_iota(jnp.int32, (ts, HEAD_DIM), 1)
    ones192 = jnp.ones((ts, V_LANES - HEAD_DIM), F32)
    q_tail = jnp.where(lane == SHIFT_LANE - HEAD_DIM, 1.0, 0.0).astype(F32)

    heads_per_tile = MXU_N // HEAD_DIM
    row = pl.program_id(1) * ts + lax.broadcasted_iota(jnp.int32, (ts, HEAD_DIM), 0)
    onehot = jnp.where(lane == row // SEL_BLOCK, 1.0, 0.0).astype(F32) + tail_ref[0:1, :]
    win_tail = jnp.broadcast_to(tail_ref[1:2, :], (ts, HEAD_DIM))

    def do_q(c):
        v = proj(C_Q + c * MXU_N, C_Q + (c + 1) * MXU_N)
        vn = v * lax.rsqrt(_seg_mean_sq(v, seg) + EPS) * gq_ref[...] * (HEAD_DIM ** -0.5 * LOG2E)
        for hh in range(heads_per_tile):
            qh = jnp.concatenate([vn[:, hh * HEAD_DIM:(hh + 1) * HEAD_DIM], q_tail], axis=-1)
            q_ref[0, c * heads_per_tile + hh] = qh.astype(BF16)

    def do_kvc():
        v = proj(C_KVC, C_KK)
        kvc_ref[0, 0] = v[:, :D_KV]
        kvc_ref[0, 1] = v[:, D_KV:]

    def do_kk():
        v = proj(C_KK, C_VV)
        vn = v * lax.rsqrt(_seg_mean_sq(v, seg) + EPS) * gkk_ref[...]
        for g in range(N_GROUPS):
            ks_ref[0, g] = jnp.concatenate(
                [vn[:, g * HEAD_DIM:(g + 1) * HEAD_DIM], onehot], axis=-1).astype(BF16)
            kw_ref[0, g] = jnp.concatenate(
                [vn[:, D_KV + g * HEAD_DIM:D_KV + (g + 1) * HEAD_DIM], win_tail], axis=-1).astype(BF16)

    def do_vv():
        v = proj(C_VV, C_GATE)
        for g in range(N_GROUPS):
            vs_ref[0, g] = jnp.concatenate(
                [v[:, g * HEAD_DIM:(g + 1) * HEAD_DIM], ones192], axis=-1).astype(BF16)
            vw_ref[0, g] = jnp.concatenate(
                [v[:, D_KV + g * HEAD_DIM:D_KV + (g + 1) * HEAD_DIM], ones192], axis=-1).astype(BF16)

    def do_gate():
        v = _sigmoid(proj(C_GATE, C_BG))
        for g in range(N_GROUPS):
            gate_ref[0, g] = v[:, g * LANES:(g + 1) * LANES]

    def do_bg(c):
        cols = slice(c * MXU_N, (c + 1) * MXU_N)
        bg_ref[0, :, cols] = proj(C_BG + c * MXU_N, C_BG + (c + 1) * MXU_N).astype(BF16)

    def do_u(c):
        cols = slice(c * MXU_N, (c + 1) * MXU_N)
        u_ref[0, :, cols] = (proj(C_CG + c * MXU_N, C_CG + (c + 1) * MXU_N)
                             * proj(C_XT + c * MXU_N, C_XT + (c + 1) * MXU_N)).astype(BF16)

    do_q(0)
    do_bg(0)
    do_q(1)
    do_bg(1)
    do_kvc()
    do_u(0)
    do_kk()
    do_u(1)
    do_gate()
    do_vv()


def _inproj(x, mod3, g1, w_p, seg, gq4, gkk4, tails):
    bsz, seq, d = x.shape
    ts = TS_IN
    const2 = lambda b, i: (0, 0)
    k_shape = jax.ShapeDtypeStruct((bsz, N_GROUPS, seq, LANES), BF16)
    k_spec = pl.BlockSpec((1, N_GROUPS, ts, LANES), lambda b, i: (b, 0, i, 0))
    v_shape = jax.ShapeDtypeStruct((bsz, N_GROUPS, seq, V_LANES), BF16)
    v_spec = pl.BlockSpec((1, N_GROUPS, ts, V_LANES), lambda b, i: (b, 0, i, 0))
    return pl.pallas_call(
        _inproj_kernel,
        grid=(bsz, seq // ts),
        in_specs=[pl.BlockSpec((1, ts, d), lambda b, i: (b, i, 0)),
                  pl.BlockSpec((1, 6, d), lambda b, i: (b, 0, 0)),
                  pl.BlockSpec((1, d), const2),
                  pl.BlockSpec(w_p.shape, const2, pipeline_mode=pl.Buffered(1)),
                  pl.BlockSpec(seg.shape, const2),
                  pl.BlockSpec(gq4.shape, const2),
                  pl.BlockSpec(gkk4.shape, const2),
                  pl.BlockSpec(tails.shape, const2)],
        out_specs=[pl.BlockSpec((1, N_HEADS, ts, LANES), lambda b, i: (b, 0, i, 0)),
                   pl.BlockSpec((1, 2, ts, D_KV), lambda b, i: (b, 0, i, 0)),
                   k_spec, v_spec, k_spec, v_spec,
                   pl.BlockSpec((1, N_GROUPS, ts, LANES), lambda b, i: (b, 0, i, 0)),
                   pl.BlockSpec((1, ts, D_CONV), lambda b, i: (b, i, 0)),
                   pl.BlockSpec((1, ts, D_CONV), lambda b, i: (b, i, 0))],
        out_shape=[jax.ShapeDtypeStruct((bsz, N_HEADS, seq, LANES), BF16),
                   jax.ShapeDtypeStruct((bsz, 2, seq, D_KV), F32),
                   k_shape, v_shape, k_shape, v_shape,
                   jax.ShapeDtypeStruct((bsz, N_GROUPS, seq, LANES), F32),
                   jax.ShapeDtypeStruct((bsz, seq, D_CONV), BF16),
                   jax.ShapeDtypeStruct((bsz, seq, D_CONV), BF16)],
        compiler_params=pltpu.CompilerParams(
            dimension_semantics=("arbitrary", "arbitrary"), vmem_limit_bytes=VMEM_LIMIT),
        name="inproj",
    )(x, mod3, g1, w_p, seg, gq4, gkk4, tails)


def _compress_kernel(kvc_ref, w1k_ref, w1v_ref, w2k_ref, w2v_ref, pk_ref, pv_ref, gk_ref,
                     kct_ref, vc_ref):
    n_rows = kvc_ref.shape[2] // CMP_STRIDE
    half = CMP_STRIDE * HEAD_DIM
    zlane = jnp.zeros((n_rows, HEAD_DIM), F32)

    def branch(which, w1_ref, w2_ref, pos_ref):
        rows = [kvc_ref[0, which, pl.ds(r, n_rows, stride=CMP_STRIDE), :] for r in range(CMP_STRIDE)]
        outs = []
        for g in range(N_GROUPS):
            x = jnp.concatenate([rw[:, g * HEAD_DIM:(g + 1) * HEAD_DIM] for rw in rows], axis=-1)
            a1 = (x + pos_ref[0:1, :]).astype(BF16)
            a2 = (x + pos_ref[1:2, :]).astype(BF16)
            hid = _dot(a1, w1_ref[0:half, :]) + pltpu.roll(_dot(a2, w1_ref[half:, :]), n_rows - 1, 0)
            outs.append(_dot(_silu(hid).astype(BF16), w2_ref[...]))
        return outs

    kc = branch(0, w1k_ref, w2k_ref, pk_ref)
    vc = branch(1, w1v_ref, w2v_ref, pv_ref)
    for g in range(N_GROUPS):
        kn = kc[g] * lax.rsqrt(jnp.mean(kc[g] * kc[g], axis=-1, keepdims=True) + EPS) * gk_ref[...]
        kct_ref[0, g] = jnp.concatenate([kn, zlane], axis=-1).T.astype(BF16)
        vc_ref[0, g] = jnp.concatenate([vc[g], zlane], axis=-1).astype(BF16)


def _compress(kvc, w1k, w1v, w2k, w2v, pk2, pv2, gk):
    bsz, _, seq, _ = kvc.shape
    c2 = lambda b: (0, 0)
    out_shape = jax.ShapeDtypeStruct((bsz, N_GROUPS, N_CMP_PAD, LANES), BF16)
    out_spec = pl.BlockSpec((1, N_GROUPS, N_CMP_PAD, LANES), lambda b: (b, 0, 0, 0))
    return pl.pallas_call(
        _compress_kernel,
        grid=(bsz,),
        in_specs=[pl.BlockSpec((1, 2, seq, D_KV), lambda b: (b, 0, 0, 0)),
                  pl.BlockSpec(w1k.shape, c2), pl.BlockSpec(w1v.shape, c2),
                  pl.BlockSpec(w2k.shape, c2), pl.BlockSpec(w2v.shape, c2),
                  pl.BlockSpec(pk2.shape, c2), pl.BlockSpec(pv2.shape, c2),
                  pl.BlockSpec(gk.shape, c2)],
        out_specs=[out_spec, out_spec],
        out_shape=[out_shape, out_shape],
        compiler_params=pltpu.CompilerParams(
            dimension_semantics=("arbitrary",), vmem_limit_bytes=VMEM_LIMIT),
        name="compress",
    )(kvc, w1k, w1v, w2k, w2v, pk2, pv2, gk)


def _logits(qa, k_rows, bias):
    s = lax.dot_general(qa, k_rows, _NT, preferred_element_type=F32)
    return s if bias is None else s + bias


def _shifted_pv(qa, k_rows, v_rows, bias):
    return _dot(jnp.exp2(_logits(qa, k_rows, bias)).astype(BF16), v_rows)


def _online_update(qa, k_tile, v_tile, bias, acc_ref, m_ref):
    s = _logits(qa, k_tile, bias)
    m_prev = m_ref[...]
    m_new = jnp.maximum(m_prev, jnp.max(s, axis=-1, keepdims=True))
    alpha = jnp.exp2(m_prev - m_new)
    p = jnp.exp2(s - jnp.concatenate([m_new] * (TQ // LANES), axis=-1))
    acc_ref[...] = (acc_ref[...] * jnp.concatenate([alpha] * (V_LANES // LANES), axis=-1)
                    + _dot(p.astype(BF16), v_tile))
    m_ref[...] = m_new


def _selection_bias(pc, qi, ovlt, n_sel):
    psum = pc[0:TQ] + pc[TQ:2 * TQ] + pc[2 * TQ:3 * TQ] + pc[3 * TQ:4 * TQ]
    hi = psum.astype(BF16)
    lo = (psum - hi.astype(F32)).astype(BF16)
    pslc_t = (lax.dot_general(ovlt, hi, _NT, preferred_element_type=F32)
              + lax.dot_general(ovlt, lo, _NT, preferred_element_type=F32))
    top_k = min(SEL_TOPK, n_sel)
    sub = SUBLANES
    j = lax.broadcasted_iota(jnp.int32, (n_sel, TQ), 0)
    t = qi * TQ + lax.broadcasted_iota(jnp.int32, (n_sel, TQ), 1)
    dist = jnp.right_shift(t, int(math.log2(SEL_BLOCK))) - j
    score = jnp.where(dist < N_LOCAL_FORCED, FORCED_SCORE, pslc_t[SEL_LANE0:SEL_LANE0 + n_sel, :])
    score = jnp.where(j == 0, FORCED_SCORE, score)
    score = jnp.where(dist >= 0, score, NEG_INF)
    groups = [score[a * sub:(a + 1) * sub] for a in range(n_sel // sub)]
    cnts = [jnp.zeros((sub, TQ), F32) for _ in groups]
    j_sub = lax.broadcasted_iota(jnp.int32, (sub, TQ), 0)
    for i in range(n_sel):
        row = jnp.broadcast_to(score[i:i + 1, :], (sub, TQ))
        for a, grp in enumerate(groups):
            if a * sub > i:
                beats = jnp.where(row >= grp, 1.0, 0.0)
            elif a * sub + sub - 1 <= i:
                beats = jnp.where(row > grp, 1.0, 0.0)
            else:
                beats = jnp.where(j_sub + a * sub > i, jnp.where(row >= grp, 1.0, 0.0),
                                  jnp.where(row > grp, 1.0, 0.0))
            cnts[a] = cnts[a] + beats
    sel_t = jnp.where(jnp.concatenate(cnts, axis=0) < top_k, 0.0, NEG_INF)
    return jnp.concatenate(
        [jnp.zeros((SEL_LANE0, TQ), F32), sel_t,
         jnp.zeros((LANES - SEL_LANE0 - n_sel, TQ), F32)], axis=0).T


def _attn_step(shifted, q_ref, kct_ref, vc_ref, ks_ref, vs_ref, kw_ref, vw_ref, gate_ref,
               cb_ref, b0_ref, b1_ref, w2_ref, ovlt_ref, o_ref, qa_ref, oc_ref, acc_s, m_s, acc_w, m_w):
    qi = pl.program_id(1)
    groups = range(N_GROUPS)
    q = [q_ref[0, g * HPG:(g + 1) * HPG].reshape(ROWS, LANES) for g in groups]

    def key_rows(ref, g, kt, n_tiles=1):
        return ref[0, g, pl.ds(pl.multiple_of(kt * TQ, TQ), n_tiles * TQ), :]

    def compressed_and_selection(which=groups, rank=True):
        o_c = []
        for g in which:
            s = _dot(q[g], kct_ref[0, g]) + cb_ref[g, 0]
            m = jnp.max(s, axis=-1, keepdims=True)
            e = jnp.exp2(s - m)
            l = jnp.sum(e, axis=-1, keepdims=True)
            pc = e * jnp.where(m > 0.5 * NEG_INF, 1.0 / l, 0.0)
            o_c.append(_dot(pc.astype(BF16), vc_ref[0, g]))
            if not rank:
                qa_ref[g] = q[g]
                continue
            selbias = _selection_bias(pc, qi, ovlt_ref[...], ks_ref.shape[2] // SEL_BLOCK)
            for h in range(HPG):
                qa_ref[g, h * TQ:(h + 1) * TQ, :] = (
                    q_ref[0, g * HPG + h].astype(F32) + selbias).astype(BF16)
        return o_c

    kt1 = jnp.maximum(qi - 1, 0)
    kt2 = jnp.maximum(qi - 2, 0)
    def finish(a_s, a_w, o_c):
        outs = []
        for g in groups:
            o_s = a_s[g][:, :LANES] / a_s[g][:, LANES:]
            o_w = a_w[g][:, :LANES] / a_w[g][:, LANES:]
            gates = gate_ref[0, g]
            for h in range(HPG):
                rows = slice(h * TQ, (h + 1) * TQ)
                g_c = gates[:, h * N_BRANCH + 0:h * N_BRANCH + 1]
                g_s = gates[:, h * N_BRANCH + 1:h * N_BRANCH + 2]
                g_w = gates[:, h * N_BRANCH + 2:h * N_BRANCH + 3]
                o_h = g_c * o_c[g][rows] + g_s * o_s[rows] + g_w * o_w[rows]
                outs.append(o_h[:, :HEAD_DIM])
        o_ref[0] = jnp.concatenate(outs, axis=-1)

    if shifted:
        def window_tiles(g, n_win):
            acc = _shifted_pv(q[g], key_rows(kw_ref, g, qi), key_rows(vw_ref, g, qi), b0_ref[g])
            if n_win >= 2:
                acc += _shifted_pv(q[g], key_rows(kw_ref, g, kt1), key_rows(vw_ref, g, kt1), b1_ref[g])
            if n_win >= 3:
                acc += _shifted_pv(q[g], key_rows(kw_ref, g, kt2), key_rows(vw_ref, g, kt2), w2_ref[...])
            return acc

        def first_block(n_win, rank):
            acc_s[...] = jnp.zeros(acc_s.shape, F32)
            for g in groups:
                acc_w[g] = window_tiles(g, n_win)
                oc_ref[g] = compressed_and_selection([g], rank)[0]

        n_all = (min(SEL_TOPK, ks_ref.shape[2] // SEL_BLOCK) * SEL_BLOCK) // TQ
        pl.when(qi == 0)(functools.partial(first_block, 1, n_all <= 0))
        pl.when(qi == 1)(functools.partial(first_block, 2, n_all <= 1))
        if n_all > 2:
            pl.when((qi >= 2) & (qi < n_all))(functools.partial(first_block, 3, False))
        pl.when(qi >= max(n_all, 2))(functools.partial(first_block, 3, True))

        n_plain = jnp.maximum(qi - 1, 0)

        def pair_body(it, carry):
            pv = [None] * N_GROUPS
            for half in range(2):
                for g in groups:
                    t = _shifted_pv(qa_ref[g], key_rows(ks_ref, g, 2 * it + half),
                                    key_rows(vs_ref, g, 2 * it + half), None)
                    pv[g] = t if pv[g] is None else pv[g] + t
            for g in groups:
                acc_s[g] += pv[g]
            return carry

        lax.fori_loop(0, n_plain // 2, pair_body, 0)

        def last_block(with_prev, with_odd):
            a_s = []
            for g in groups:
                acc = acc_s[g] + _shifted_pv(qa_ref[g], key_rows(ks_ref, g, qi), key_rows(vs_ref, g, qi),
                                             b0_ref[g])
                if with_prev:
                    acc += _shifted_pv(qa_ref[g], key_rows(ks_ref, g, kt1), key_rows(vs_ref, g, kt1),
                                       b1_ref[g])
                if with_odd:
                    acc += _shifted_pv(qa_ref[g], key_rows(ks_ref, g, kt2), key_rows(vs_ref, g, kt2), None)
                a_s.append(acc)
            finish(a_s, [acc_w[g] for g in groups], [oc_ref[g] for g in groups])

        is_odd = n_plain % 2 == 1
        pl.when(qi == 0)(functools.partial(last_block, False, False))
        pl.when((qi >= 1) & jnp.logical_not(is_odd))(functools.partial(last_block, True, False))
        pl.when(is_odd)(functools.partial(last_block, True, True))
    else:
        o_c = compressed_and_selection()
        a_s, a_w = [], []
        for g in groups:
            qa = qa_ref[g]
            sel = (acc_s.at[g], m_s.at[g])
            win = (acc_w.at[g], m_w.at[g])
            for acc, m_ref in (sel, win):
                acc[...] = jnp.zeros(acc.shape, F32)
                m_ref[...] = jnp.full(m_ref.shape, NEG_INF, F32)

            def sel_body(kt, carry, g=g, qa=qa, sel=sel):
                _online_update(qa, key_rows(ks_ref, g, kt), key_rows(vs_ref, g, kt), None, *sel)
                return carry

            lax.fori_loop(0, qi - 1, sel_body, 0)

            @pl.when(qi >= 2)
            def _(g=g, win=win):
                _online_update(q[g], key_rows(kw_ref, g, kt2), key_rows(vw_ref, g, kt2), w2_ref[...], *win)

            @pl.when(qi >= 1)
            def _(g=g, qa=qa, sel=sel, win=win):
                _online_update(qa, key_rows(ks_ref, g, kt1), key_rows(vs_ref, g, kt1), b1_ref[g], *sel)
                _online_update(q[g], key_rows(kw_ref, g, kt1), key_rows(vw_ref, g, kt1), b1_ref[g], *win)

            _online_update(qa, key_rows(ks_ref, g, qi), key_rows(vs_ref, g, qi), b0_ref[g], *sel)
            _online_update(q[g], key_rows(kw_ref, g, qi), key_rows(vw_ref, g, qi), b0_ref[g], *win)
            a_s.append(acc_s[g])
            a_w.append(acc_w[g])

        finish(a_s, a_w, o_c)


def _attn_kernel(flag_ref, *refs):
    use_shift = flag_ref[0] == 1

    @pl.when(use_shift)
    def _():
        _attn_step(True, *refs)

    @pl.when(jnp.logical_not(use_shift))
    def _():
        _attn_step(False, *refs)


def _attn(flag, q, kct, vc, ks, vs, kw, vw, gate, cb, b0, b1, w2, ovlt):
    bsz, _, seq, _ = q.shape
    nq = seq // TQ
    per_batch = lambda b, i: (b, 0, 0, 0)
    k_spec = pl.BlockSpec((1, N_GROUPS, seq, LANES), per_batch)
    v_spec = pl.BlockSpec((1, N_GROUPS, seq, V_LANES), per_batch)
    cmp_spec = pl.BlockSpec((1, N_GROUPS, N_CMP_PAD, LANES), per_batch)
    const = lambda shape: pl.BlockSpec(shape, lambda b, i: (0,) * len(shape),
                                       pipeline_mode=pl.Buffered(1))
    return pl.pallas_call(
        _attn_kernel,
        grid=(bsz, nq),
        in_specs=[pl.BlockSpec(memory_space=pltpu.SMEM),
                  pl.BlockSpec((1, N_HEADS, TQ, LANES), lambda b, i: (b, 0, i, 0)),
                  cmp_spec, cmp_spec, k_spec, v_spec, k_spec, v_spec,
                  pl.BlockSpec((1, N_GROUPS, TQ, LANES), lambda b, i: (b, 0, i, 0)),
                  pl.BlockSpec((N_GROUPS, 1, ROWS, N_CMP_PAD), lambda b, i: (0, i, 0, 0)),
                  const(b0.shape), const(b1.shape), const(w2.shape), const(ovlt.shape)],
        out_specs=pl.BlockSpec((1, TQ, D_ATTN), lambda b, i: (b, i, 0)),
        out_shape=jax.ShapeDtypeStruct((bsz, seq, D_ATTN), F32),
        scratch_shapes=[pltpu.VMEM((N_GROUPS, ROWS, LANES), BF16),
                        pltpu.VMEM((N_GROUPS, ROWS, LANES), F32),
                        pltpu.VMEM((N_GROUPS, ROWS, V_LANES), F32),
                        pltpu.VMEM((N_GROUPS, ROWS, LANES), F32),
                        pltpu.VMEM((N_GROUPS, ROWS, V_LANES), F32),
                        pltpu.VMEM((N_GROUPS, ROWS, LANES), F32)],
        compiler_params=pltpu.CompilerParams(
            dimension_semantics=("arbitrary", "arbitrary"),
            vmem_limit_bytes=VMEM_LIMIT),
        name="attn",
    )(flag, q, kct, vc, ks, vs, kw, vw, gate, cb, b0, b1, w2, ovlt)


TS_OUT = 512
HALO = 16


def _outproj_ffn_kernel(ya_ref, bg_ref, u_ref, uh_ref, cw_ref, ga_ref, gc_ref, w_ref, x_ref, mod_ref,
                        g2_ref, w1_ref, w3_ref, w2_ref, o_ref):
    u = u_ref[0].astype(F32)
    halo = uh_ref[0].astype(F32) * jnp.where(pl.program_id(1) > 0, 1.0, 0.0)
    h1 = halo[HALO - 1:HALO, :]
    h2 = halo[HALO - 2:HALO - 1, :]
    row = lax.broadcasted_iota(jnp.int32, u.shape, 0)
    u1 = jnp.where(row == 0, h1, pltpu.roll(u, 1, 0))
    u2 = jnp.where(row == 0, h2, jnp.where(row == 1, h1, pltpu.roll(u, 2, 0)))
    conv = u2 * cw_ref[0:1, :] + u1 * cw_ref[1:2, :] + u * cw_ref[2:3, :]
    yc = bg_ref[0].astype(F32) * conv
    yc = yc * lax.rsqrt(jnp.mean(yc * yc, axis=-1, keepdims=True) + EPS) * gc_ref[...]
    ya = ya_ref[0]
    ya = ya * lax.rsqrt(jnp.mean(ya * ya, axis=-1, keepdims=True) + EPS) * ga_ref[...]
    proj = _dot(ya.astype(BF16), w_ref[0:D_ATTN, :]) + _dot(yc.astype(BF16), w_ref[D_ATTN:, :])
    x = x_ref[0] + mod_ref[0, 2:3, :] * proj

    ms = jnp.mean(x * x, axis=-1, keepdims=True)
    h = x * lax.rsqrt(ms + EPS) * g2_ref[...] * (1.0 + mod_ref[0, 4:5, :]) + mod_ref[0, 3:4, :]
    hb = h.astype(BF16)
    a = _dot(hb, w1_ref[...])
    b = _dot(hb, w3_ref[...])
    act = (_silu(a) * b).astype(BF16)
    o_ref[0] = x + mod_ref[0, 5:6, :] * _dot(act, w2_ref[...])


def _outproj_ffn(ya, bg, u, conv_w, ga, gc, w_out, x, mod3, g2, w1, w3, w2):
    bsz, seq, d = x.shape
    ts = TS_OUT
    c2 = lambda b, i: (0, 0)
    row_spec = lambda width: pl.BlockSpec((1, ts, width), lambda b, i: (b, i, 0))
    wspec = lambda w: pl.BlockSpec(w.shape, c2, pipeline_mode=pl.Buffered(1))
    return pl.pallas_call(
        _outproj_ffn_kernel,
        grid=(bsz, seq // ts),
        in_specs=[row_spec(D_ATTN), row_spec(D_CONV), row_spec(D_CONV),
                  pl.BlockSpec((1, HALO, D_CONV),
                               lambda b, i: (b, jnp.maximum(i * (ts // HALO) - 1, 0), 0)),
                  pl.BlockSpec(conv_w.shape, c2),
                  pl.BlockSpec((1, D_ATTN), c2), pl.BlockSpec((1, D_CONV), c2),
                  wspec(w_out),
                  row_spec(d),
                  pl.BlockSpec((1, 6, d), lambda b, i: (b, 0, 0)),
                  pl.BlockSpec((1, d), c2),
                  wspec(w1), wspec(w3), wspec(w2)],
        out_specs=row_spec(d),
        out_shape=jax.ShapeDtypeStruct((bsz, seq, d), F32),
        compiler_params=pltpu.CompilerParams(
            dimension_semantics=("arbitrary", "arbitrary"), vmem_limit_bytes=VMEM_LIMIT),
        name="outproj_ffn",
    )(ya, bg, u, u, conv_w, ga, gc, w_out, x, mod3, g2, w1, w3, w2)


def _t5_bucket_np(rel):
    n = np.maximum(rel, 0)
    max_exact = N_BUCKETS // 2
    nf = np.maximum(n, 1).astype(np.float32)
    large = max_exact + (np.log(nf / max_exact) / math.log(MAX_DISTANCE / max_exact)
                         * (N_BUCKETS - max_exact)).astype(np.int32)
    return np.where(n < max_exact, n, np.minimum(large, N_BUCKETS - 1)).astype(np.int32)


def _bias_of_rel(tab_rel, rel):
    idx = jnp.asarray(_t5_bucket_np(rel))
    tab_b = tab_rel.T.reshape((N_HEADS, N_BUCKETS) + (1,) * idx.ndim)
    vals = jnp.zeros((N_HEADS,) + idx.shape, F32)
    for k in range(N_BUCKETS):
        vals = jnp.where(idx == k, tab_b[:, k], vals)
    return jnp.where(jnp.asarray(rel >= 0), vals, NEG_INF)


def _toeplitz_offsets(length, n_cols):
    k = np.arange(length)
    return np.where(k < n_cols, k, k - length)


TOEPLITZ_LEN = 2 * TQ
PATTERN_LEN = 3 * LANES


def _tables_kernel(v0_ref, v1_ref, vp_ref, b0_ref, b1_ref, cb_ref):
    def toeplitz_tile(v_ref):
        rows = jnp.broadcast_to(v_ref[0], (TQ, TOEPLITZ_LEN))
        return pltpu.roll(rows, 0, 1, stride=1, stride_axis=0)[:, :TQ]

    b0_ref[0] = toeplitz_tile(v0_ref)
    b1_ref[0] = toeplitz_tile(v1_ref)
    vp = vp_ref[0]
    shift = TQ // CMP_STRIDE
    pat = jnp.concatenate(
        [vp[:, :2 * N_CMP_PAD]] + [pltpu.roll(vp, a, 1)[:, :2 * N_CMP_PAD] for a in range(1, shift)],
        axis=0)
    for qi in range(cb_ref.shape[1]):
        cb_ref[0, qi] = pat[:, N_CMP_PAD - qi * shift:2 * N_CMP_PAD - qi * shift]


def _bias_tables(rel_bias_table, seq):
    tab_rel = (rel_bias_table - rel_bias_table[N_BUCKETS - 1:N_BUCKETS, :]) * LOG2E
    nq = seq // TQ
    n_cmp = (seq - CMP_BLOCK) // CMP_STRIDE + 1
    shift = TQ // CMP_STRIDE
    assert nq * shift <= N_CMP_PAD and n_cmp * CMP_STRIDE + CMP_BLOCK - 1 > seq
    assert shift + 2 * N_CMP_PAD - 1 <= PATTERN_LEN
    d = _toeplitz_offsets(TOEPLITZ_LEN, TQ)
    v0 = _bias_of_rel(tab_rel, -d)[:, None, :]
    v1 = _bias_of_rel(tab_rel, TQ - d)[:, None, :]
    off = N_CMP_PAD * CMP_STRIDE - (CMP_BLOCK - 1)
    dp = _toeplitz_offsets(PATTERN_LEN, 2 * N_CMP_PAD)[None, :]
    vp = _bias_of_rel(tab_rel, np.arange(CMP_STRIDE)[:, None] + off - CMP_STRIDE * dp)
    head = lambda h: (h // HPG, h % HPG, 0)
    b0, b1, cb = pl.pallas_call(
        _tables_kernel,
        grid=(N_HEADS,),
        in_specs=[pl.BlockSpec((1, 1, TOEPLITZ_LEN), lambda h: (h, 0, 0)),
                  pl.BlockSpec((1, 1, TOEPLITZ_LEN), lambda h: (h, 0, 0)),
                  pl.BlockSpec((1, CMP_STRIDE, PATTERN_LEN), lambda h: (h, 0, 0))],
        out_specs=[pl.BlockSpec((1, TQ, TQ), head), pl.BlockSpec((1, TQ, TQ), head),
                   pl.BlockSpec((1, nq, TQ, N_CMP_PAD), lambda h: (h // HPG, 0, h % HPG, 0))],
        out_shape=[jax.ShapeDtypeStruct((N_GROUPS, ROWS, TQ), F32),
                   jax.ShapeDtypeStruct((N_GROUPS, ROWS, TQ), F32),
                   jax.ShapeDtypeStruct((N_GROUPS, nq, ROWS, N_CMP_PAD), F32)],
        compiler_params=pltpu.CompilerParams(
            dimension_semantics=("arbitrary",), vmem_limit_bytes=VMEM_LIMIT),
        name="bias_tables",
    )(v0, v1, vp)
    i = np.arange(TQ)[:, None]
    jj = np.arange(TQ)[None, :]
    w2 = jnp.asarray(np.tile(np.where(jj > i, 0.0, NEG_INF).astype(np.float32), (HPG, 1)))
    n_sel = seq // SEL_BLOCK
    cs = np.arange(n_cmp) * CMP_STRIDE
    ce = cs + CMP_BLOCK - 1
    ss = np.arange(n_sel) * SEL_BLOCK
    ov = np.clip(np.minimum(ce[:, None], ss[None, :] + SEL_BLOCK - 1)
                 - np.maximum(cs[:, None], ss[None, :]) + 1, 0, None) / CMP_STRIDE
    ovlt = np.zeros((LANES, N_CMP_PAD), np.float32)
    ovlt[SEL_LANE0:SEL_LANE0 + n_sel, :n_cmp] = ov.T
    return tab_rel, cb, b0, b1, w2, jnp.asarray(ovlt, BF16)


def kernel(x, c, w_ada, b_ada, norm1_gain, w_in, q_gain, k_cmp_gain, k_sel_gain, k_win_gain,
           cmp_pos_k, cmp_pos_v, w_ck1, w_ck2, w_cv1, w_cv2, rel_bias_table, conv_w,
           attn_out_gain, conv_out_gain, w_out, norm2_gain, w_ff1, w_ff3, w_ff2):
    bsz, seq, d = x.shape
    assert d == D_MODEL and seq % TQ == 0 and SEL_LANE0 + seq // SEL_BLOCK <= SHIFT_LANE
    assert (seq - CMP_BLOCK) // CMP_STRIDE + 1 <= N_CMP_PAD and seq // CMP_STRIDE == N_CMP_PAD

    seg_of = lambda n: jnp.asarray(np.kron(np.eye(n), np.ones((HEAD_DIM, HEAD_DIM))) / HEAD_DIM, BF16)
    seg4 = seg_of(MXU_N // HEAD_DIM)
    tab_rel, cb, b0, b1, w2m, ovlt = _bias_tables(rel_bias_table, seq)
    bias_max = jnp.max(jnp.abs(tab_rel))
    two = lambda g: jnp.tile(g.reshape(1, HEAD_DIM), (1, 2))
    n_gate = N_BRANCH * HPG
    gate_pad = jnp.zeros((d, LANES - n_gate), BF16)

    for layer in range(w_in.shape[0]):
        wi = w_in[layer].astype(BF16)
        o_g = D_ATTN + 6 * D_KV
        kv = lambda n: wi[:, D_ATTN + n * D_KV:D_ATTN + (n + 1) * D_KV]
        w_p = jnp.concatenate(
            [wi[:, :D_ATTN], kv(0), kv(1), kv(2), kv(4), kv(3), kv(5),
             wi[:, o_g:o_g + n_gate], gate_pad,
             wi[:, o_g + n_gate:o_g + 2 * n_gate], gate_pad, wi[:, o_g + 2 * n_gate:]],
            axis=-1)
        w1k, w1v = w_ck1[layer].astype(BF16), w_cv1[layer].astype(BF16)
        w2k, w2v = w_ck2[layer].astype(BF16), w_cv2[layer].astype(BF16)
        pk2 = cmp_pos_k[layer].reshape(2, CMP_STRIDE * HEAD_DIM)
        pv2 = cmp_pos_v[layer].reshape(2, CMP_STRIDE * HEAD_DIM)

        gq_max = jnp.max(jnp.abs(q_gain[layer]))
        bound = lambda gk: (HEAD_DIM ** 0.5 * LOG2E * 1.01) * gq_max * jnp.max(jnp.abs(gk)) + bias_max
        c_sel, c_win = bound(k_sel_gain[layer]), bound(k_win_gain[layer])
        flag = (2.0 * jnp.maximum(c_sel, c_win) <= MAX_SHIFTED_RANGE).astype(jnp.int32).reshape(1)
        tails = jnp.zeros((2, HEAD_DIM), F32).at[:, SHIFT_LANE - HEAD_DIM].set(
            -jnp.stack([c_sel, c_win]))

        mod3 = _adaln(c, w_ada[layer], b_ada[layer]).reshape(bsz, 6, d)
        q, kvc, ks, vs, kw, vw, gate, bg, u = _inproj(
            x, mod3, norm1_gain[layer].reshape(1, d), w_p, seg4,
            jnp.tile(q_gain[layer].reshape(1, HEAD_DIM), (1, MXU_N // HEAD_DIM)),
            jnp.concatenate([two(k_sel_gain[layer]), two(k_win_gain[layer])], axis=-1), tails)
        kct, vc = _compress(kvc, w1k, w1v, w2k, w2v, pk2, pv2,
                            k_cmp_gain[layer].reshape(1, HEAD_DIM))
        y_attn = _attn(flag, q, kct, vc, ks, vs, kw, vw, gate, cb, b0, b1, w2m, ovlt)
        x = _outproj_ffn(y_attn, bg, u, conv_w[layer], attn_out_gain[layer].reshape(1, D_ATTN),
                         conv_out_gain[layer].reshape(1, D_CONV), w_out[layer].astype(BF16), x, mod3,
                         norm2_gain[layer].reshape(1, d), w_ff1[layer].astype(BF16),
                         w_ff3[layer].astype(BF16), w_ff2[layer].astype(BF16))
    return x
```

```python
import functools
import math

import numpy as np
import jax
import jax.numpy as jnp
from jax import lax
from jax.experimental import pallas as pl
from jax.experimental.pallas import tpu as pltpu

F32 = jnp.float32
BF16 = jnp.bfloat16

D_MODEL = 1024
HEAD_DIM = 64
N_HEADS = 8
N_GROUPS = 2
HPG = N_HEADS // N_GROUPS
D_ATTN = N_HEADS * HEAD_DIM
D_KV = N_GROUPS * HEAD_DIM
N_BRANCH = 3
D_CONV = D_MODEL - D_ATTN
CONV_WIDTH = 3
CMP_BLOCK = 32
CMP_STRIDE = 16
CMP_HIDDEN = 256
SEL_BLOCK = 64
SEL_TOPK = 16
N_LOCAL_FORCED = 2
WINDOW = 512
N_BUCKETS = 32
MAX_DISTANCE = 128
EPS = 1e-6
NEG_INF = -1e30
FORCED_SCORE = 1e6

LANES = 128
SUBLANES = 8
TQ = 256
ROWS = HPG * TQ
N_CMP_PAD = 128
SEL_LANE0 = 64
SHIFT_LANE = 96
V_LANES = 256
LOG2E = math.log2(math.e)
MAX_SHIFTED_RANGE = 100.0
VMEM_LIMIT = 56 * 1024 * 1024

_NT = (((1,), (1,)), ((), ()))


def _dot(a, b):
    return jnp.dot(a, b, preferred_element_type=F32)


def _silu(v):
    return v * (1.0 / (1.0 + jnp.exp(-v)))


def _sigmoid(v):
    return 1.0 / (1.0 + jnp.exp(-v))


def _seg_mean_sq(v, seg):
    sq = v * v
    hi = sq.astype(BF16)
    lo = (sq - hi.astype(F32)).astype(BF16)
    return _dot(hi, seg) + _dot(lo, seg)


TN_ADALN = 1536


def _adaln_kernel(c_ref, w_ref, b_ref, o_ref):
    sc = _silu(c_ref[...]).astype(BF16)
    o_ref[...] = _dot(sc, w_ref[...].astype(BF16)) + b_ref[...]


def _adaln(c, w_ada, b_ada):
    bsz, d = c.shape
    n = w_ada.shape[1]
    tn = TN_ADALN
    return pl.pallas_call(
        _adaln_kernel,
        grid=(n // tn,),
        in_specs=[pl.BlockSpec((bsz, d), lambda j: (0, 0)),
                  pl.BlockSpec((d, tn), lambda j: (0, j)),
                  pl.BlockSpec((1, tn), lambda j: (0, j))],
        out_specs=pl.BlockSpec((bsz, tn), lambda j: (0, j)),
        out_shape=jax.ShapeDtypeStruct((bsz, n), F32),
        compiler_params=pltpu.CompilerParams(
            dimension_semantics=("arbitrary",), vmem_limit_bytes=VMEM_LIMIT),
        name="adaln",
    )(c, w_ada, b_ada.reshape(1, n))


C_Q, C_KVC, C_KK, C_VV, C_GATE, C_BG, C_CG, C_XT, C_END = (
    0, 512, 768, 1024, 1280, 1536, 2048, 2560, 3072)
MXU_N = 256
TS_IN = 1024


def _inproj_kernel(x_ref, mod_ref, g1_ref, w_ref, seg_ref, gq_ref, gkk_ref, tail_ref,
                   q_ref, kvc_ref, ks_ref, vs_ref, kw_ref, vw_ref, gate_ref, bg_ref, u_ref):
    ts = x_ref.shape[1]
    x = x_ref[0]
    ms = jnp.mean(x * x, axis=-1, keepdims=True)
    y = x * lax.rsqrt(ms + EPS) * g1_ref[...]
    h = y * (1.0 + mod_ref[0, 1:2, :]) + mod_ref[0, 0:1, :]
    hb = h.astype(BF16)
    seg = seg_ref[...]

    def proj(c0, c1):
        return _dot(hb, w_ref[:, c0:c1])

    lane = lax.broadcasted_iota(jnp.int32, (ts, HEAD_DIM), 1)
    ones192 = jnp.ones((ts, V_LANES - HEAD_DIM), F32)
    q_tail = jnp.where(lane == SHIFT_LANE - HEAD_DIM, 1.0, 0.0).astype(F32)

    heads_per_tile = MXU_N // HEAD_DIM
    row = pl.program_id(1) * ts + lax.broadcasted_iota(jnp.int32, (ts, HEAD_DIM), 0)
    onehot = jnp.where(lane == row // SEL_BLOCK, 1.0, 0.0).astype(F32) + tail_ref[0:1, :]
    win_tail = jnp.broadcast_to(tail_ref[1:2, :], (ts, HEAD_DIM))

    def do_q(c):
        v = proj(C_Q + c * MXU_N, C_Q + (c + 1) * MXU_N)
        vn = v * lax.rsqrt(_seg_mean_sq(v, seg) + EPS) * gq_ref[...] * (HEAD_DIM ** -0.5 * LOG2E)
        for hh in range(heads_per_tile):
            qh = jnp.concatenate([vn[:, hh * HEAD_DIM:(hh + 1) * HEAD_DIM], q_tail], axis=-1)
            q_ref[0, c * heads_per_tile + hh] = qh.astype(BF16)

    def do_kvc():
        v = proj(C_KVC, C_KK)
        kvc_ref[0, 0] = v[:, :D_KV]
        kvc_ref[0, 1] = v[:, D_KV:]

    def do_kk():
        v = proj(C_KK, C_VV)
        vn = v * lax.rsqrt(_seg_mean_sq(v, seg) + EPS) * gkk_ref[...]
        for g in range(N_GROUPS):
            ks_ref[0, g] = jnp.concatenate(
                [vn[:, g * HEAD_DIM:(g + 1) * HEAD_DIM], onehot], axis=-1).astype(BF16)
            kw_ref[0, g] = jnp.concatenate(
                [vn[:, D_KV + g * HEAD_DIM:D_KV + (g + 1) * HEAD_DIM], win_tail], axis=-1).astype(BF16)

    def do_vv():
        v = proj(C_VV, C_GATE)
        for g in range(N_GROUPS):
            vs_ref[0, g] = jnp.concatenate(
                [v[:, g * HEAD_DIM:(g + 1) * HEAD_DIM], ones192], axis=-1).astype(BF16)
            vw_ref[0, g] = jnp.concatenate(
                [v[:, D_KV + g * HEAD_DIM:D_KV + (g + 1) * HEAD_DIM], ones192], axis=-1).astype(BF16)

    def do_gate():
        v = _sigmoid(proj(C_GATE, C_BG))
        for g in range(N_GROUPS):
            gate_ref[0, g] = v[:, g * LANES:(g + 1) * LANES]

    def do_bg(c):
        cols = slice(c * MXU_N, (c + 1) * MXU_N)
        bg_ref[0, :, cols] = proj(C_BG + c * MXU_N, C_BG + (c + 1) * MXU_N).astype(BF16)

    def do_u(c):
        cols = slice(c * MXU_N, (c + 1) * MXU_N)
        u_ref[0, :, cols] = (proj(C_CG + c * MXU_N, C_CG + (c + 1) * MXU_N)
                             * proj(C_XT + c * MXU_N, C_XT + (c + 1) * MXU_N)).astype(BF16)

    do_q(0)
    do_bg(0)
    do_q(1)
    do_bg(1)
    do_kvc()
    do_u(0)
    do_kk()
    do_u(1)
    do_gate()
    do_vv()


def _inproj(x, mod3, g1, w_p, seg, gq4, gkk4, tails):
    bsz, seq, d = x.shape
    ts = TS_IN
    const2 = lambda b, i: (0, 0)
    k_shape = jax.ShapeDtypeStruct((bsz, N_GROUPS, seq, LANES), BF16)
    k_spec = pl.BlockSpec((1, N_GROUPS, ts, LANES), lambda b, i: (b, 0, i, 0))
    v_shape = jax.ShapeDtypeStruct((bsz, N_GROUPS, seq, V_LANES), BF16)
    v_spec = pl.BlockSpec((1, N_GROUPS, ts, V_LANES), lambda b, i: (b, 0, i, 0))
    return pl.pallas_call(
        _inproj_kernel,
        grid=(bsz, seq // ts),
        in_specs=[pl.BlockSpec((1, ts, d), lambda b, i: (b, i, 0)),
                  pl.BlockSpec((1, 6, d), lambda b, i: (b, 0, 0)),
                  pl.BlockSpec((1, d), const2),
                  pl.BlockSpec(w_p.shape, const2, pipeline_mode=pl.Buffered(1)),
                  pl.BlockSpec(seg.shape, const2),
                  pl.BlockSpec(gq4.shape, const2),
                  pl.BlockSpec(gkk4.shape, const2),
                  pl.BlockSpec(tails.shape, const2)],
        out_specs=[pl.BlockSpec((1, N_HEADS, ts, LANES), lambda b, i: (b, 0, i, 0)),
                   pl.BlockSpec((1, 2, ts, D_KV), lambda b, i: (b, 0, i, 0)),
                   k_spec, v_spec, k_spec, v_spec,
                   pl.BlockSpec((1, N_GROUPS, ts, LANES), lambda b, i: (b, 0, i, 0)),
                   pl.BlockSpec((1, ts, D_CONV), lambda b, i: (b, i, 0)),
                   pl.BlockSpec((1, ts, D_CONV), lambda b, i: (b, i, 0))],
        out_shape=[jax.ShapeDtypeStruct((bsz, N_HEADS, seq, LANES), BF16),
                   jax.ShapeDtypeStruct((bsz, 2, seq, D_KV), F32),
                   k_shape, v_shape, k_shape, v_shape,
                   jax.ShapeDtypeStruct((bsz, N_GROUPS, seq, LANES), F32),
                   jax.ShapeDtypeStruct((bsz, seq, D_CONV), BF16),
                   jax.ShapeDtypeStruct((bsz, seq, D_CONV), BF16)],
        compiler_params=pltpu.CompilerParams(
            dimension_semantics=("arbitrary", "arbitrary"), vmem_limit_bytes=VMEM_LIMIT),
        name="inproj",
    )(x, mod3, g1, w_p, seg, gq4, gkk4, tails)


def _compress_kernel(kvc_ref, w1k_ref, w1v_ref, w2k_ref, w2v_ref, pk_ref, pv_ref, gk_ref,
                     kct_ref, vc_ref):
    n_rows = kvc_ref.shape[2] // CMP_STRIDE
    half = CMP_STRIDE * HEAD_DIM
    zlane = jnp.zeros((n_rows, HEAD_DIM), F32)

    def branch(which, w1_ref, w2_ref, pos_ref):
        rows = [kvc_ref[0, which, pl.ds(r, n_rows, stride=CMP_STRIDE), :] for r in range(CMP_STRIDE)]
        outs = []
        for g in range(N_GROUPS):
            x = jnp.concatenate([rw[:, g * HEAD_DIM:(g + 1) * HEAD_DIM] for rw in rows], axis=-1)
            a1 = (x + pos_ref[0:1, :]).astype(BF16)
            a2 = (x + pos_ref[1:2, :]).astype(BF16)
            hid = _dot(a1, w1_ref[0:half, :]) + pltpu.roll(_dot(a2, w1_ref[half:, :]), n_rows - 1, 0)
            outs.append(_dot(_silu(hid).astype(BF16), w2_ref[...]))
        return outs

    kc = branch(0, w1k_ref, w2k_ref, pk_ref)
    vc = branch(1, w1v_ref, w2v_ref, pv_ref)
    for g in range(N_GROUPS):
        kn = kc[g] * lax.rsqrt(jnp.mean(kc[g] * kc[g], axis=-1, keepdims=True) + EPS) * gk_ref[...]
        kct_ref[0, g] = jnp.concatenate([kn, zlane], axis=-1).T.astype(BF16)
        vc_ref[0, g] = jnp.concatenate([vc[g], zlane], axis=-1).astype(BF16)


def _compress(kvc, w1k, w1v, w2k, w2v, pk2, pv2, gk):
    bsz, _, seq, _ = kvc.shape
    c2 = lambda b: (0, 0)
    out_shape = jax.ShapeDtypeStruct((bsz, N_GROUPS, N_CMP_PAD, LANES), BF16)
    out_spec = pl.BlockSpec((1, N_GROUPS, N_CMP_PAD, LANES), lambda b: (b, 0, 0, 0))
    return pl.pallas_call(
        _compress_kernel,
        grid=(bsz,),
        in_specs=[pl.BlockSpec((1, 2, seq, D_KV), lambda b: (b, 0, 0, 0)),
                  pl.BlockSpec(w1k.shape, c2), pl.BlockSpec(w1v.shape, c2),
                  pl.BlockSpec(w2k.shape, c2), pl.BlockSpec(w2v.shape, c2),
                  pl.BlockSpec(pk2.shape, c2), pl.BlockSpec(pv2.shape, c2),
                  pl.BlockSpec(gk.shape, c2)],
        out_specs=[out_spec, out_spec],
        out_shape=[out_shape, out_shape],
        compiler_params=pltpu.CompilerParams(
            dimension_semantics=("arbitrary",), vmem_limit_bytes=VMEM_LIMIT),
        name="compress",
    )(kvc, w1k, w1v, w2k, w2v, pk2, pv2, gk)


def _logits(qa, k_rows, bias):
    s = lax.dot_general(qa, k_rows, _NT, preferred_element_type=F32)
    return s if bias is None else s + bias


def _shifted_pv(qa, k_rows, v_rows, bias):
    return _dot(jnp.exp2(_logits(qa, k_rows, bias)).astype(BF16), v_rows)


def _online_update(qa, k_tile, v_tile, bias, acc_ref, m_ref):
    s = _logits(qa, k_tile, bias)
    m_prev = m_ref[...]
    m_new = jnp.maximum(m_prev, jnp.max(s, axis=-1, keepdims=True))
    alpha = jnp.exp2(m_prev - m_new)
    p = jnp.exp2(s - jnp.concatenate([m_new] * (TQ // LANES), axis=-1))
    acc_ref[...] = (acc_ref[...] * jnp.concatenate([alpha] * (V_LANES // LANES), axis=-1)
                    + _dot(p.astype(BF16), v_tile))
    m_ref[...] = m_new


def _selection_bias(pct, qi, ovlt, n_sel):
    psum = pct[:, 0:TQ] + pct[:, TQ:2 * TQ] + pct[:, 2 * TQ:3 * TQ] + pct[:, 3 * TQ:4 * TQ]
    hi = psum.astype(BF16)
    lo = (psum - hi.astype(F32)).astype(BF16)
    pslc_t = _dot(ovlt, hi) + _dot(ovlt, lo)
    top_k = min(SEL_TOPK, n_sel)
    sub = SUBLANES
    j = lax.broadcasted_iota(jnp.int32, (n_sel, TQ), 0)
    t = qi * TQ + lax.broadcasted_iota(jnp.int32, (n_sel, TQ), 1)
    dist = jnp.right_shift(t, int(math.log2(SEL_BLOCK))) - j
    score = jnp.where(dist < N_LOCAL_FORCED, FORCED_SCORE, pslc_t[SEL_LANE0:SEL_LANE0 + n_sel, :])
    score = jnp.where(j == 0, FORCED_SCORE, score)
    score = jnp.where(dist >= 0, score, NEG_INF)
    groups = [score[a * sub:(a + 1) * sub] for a in range(n_sel // sub)]
    cnts = [jnp.zeros((sub, TQ), F32) for _ in groups]
    j_sub = lax.broadcasted_iota(jnp.int32, (sub, TQ), 0)
    for i in range(n_sel):
        row = jnp.broadcast_to(score[i:i + 1, :], (sub, TQ))
        for a, grp in enumerate(groups):
            if a * sub > i:
                beats = jnp.where(row >= grp, 1.0, 0.0)
            elif a * sub + sub - 1 <= i:
                beats = jnp.where(row > grp, 1.0, 0.0)
            else:
                beats = jnp.where(j_sub + a * sub > i, jnp.where(row >= grp, 1.0, 0.0),
                                  jnp.where(row > grp, 1.0, 0.0))
            cnts[a] = cnts[a] + beats
    sel_t = jnp.where(jnp.concatenate(cnts, axis=0) < top_k, 0.0, NEG_INF)
    return jnp.concatenate(
        [jnp.zeros((SEL_LANE0, TQ), F32), sel_t,
         jnp.zeros((LANES - SEL_LANE0 - n_sel, TQ), F32)], axis=0).T


def _attn_step(shifted, q_ref, kct_ref, vc_ref, ks_ref, vs_ref, kw_ref, vw_ref, gate_ref,
               cb_ref, b0_ref, b1_ref, w2_ref, ovlt_ref, o_ref, qa_ref, oc_ref, acc_s, m_s, acc_w, m_w):
    qi = pl.program_id(1)
    groups = range(N_GROUPS)
    q = [q_ref[0, g * HPG:(g + 1) * HPG].reshape(ROWS, LANES) for g in groups]

    def key_rows(ref, g, kt, n_tiles=1):
        return ref[0, g, pl.ds(pl.multiple_of(kt * TQ, TQ), n_tiles * TQ), :]

    def compressed_and_selection(which=groups, rank=True):
        o_c = []
        for g in which:
            kc = kct_ref[0, g].astype(F32).T.astype(BF16)
            st = lax.dot_general(kc, q[g], _NT, preferred_element_type=F32) + cb_ref[g, 0]
            m = jnp.max(st, axis=0, keepdims=True)
            e = jnp.exp2(st - m)
            l = jnp.sum(e, axis=0, keepdims=True)
            pct = e * jnp.where(m > 0.5 * NEG_INF, 1.0 / l, 0.0)
            vct = vc_ref[0, g].astype(F32).T.astype(BF16)
            o_c.append(_dot(vct, pct.astype(BF16)).T)
            if not rank:
                qa_ref[g] = q[g]
                continue
            selbias = _selection_bias(pct, qi, ovlt_ref[...], ks_ref.shape[2] // SEL_BLOCK)
            for h in range(HPG):
                qa_ref[g, h * TQ:(h + 1) * TQ, :] = (
                    q_ref[0, g * HPG + h].astype(F32) + selbias).astype(BF16)
        return o_c

    kt1 = jnp.maximum(qi - 1, 0)
    kt2 = jnp.maximum(qi - 2, 0)
    def finish(a_s, a_w, o_c):
        outs = []
        for g in groups:
            o_s = a_s[g][:, :LANES] / a_s[g][:, LANES:]
            o_w = a_w[g][:, :LANES] / a_w[g][:, LANES:]
            gates = gate_ref[0, g]
            for h in range(HPG):
                rows = slice(h * TQ, (h + 1) * TQ)
                g_c = gates[:, h * N_BRANCH + 0:h * N_BRANCH + 1]
                g_s = gates[:, h * N_BRANCH + 1:h * N_BRANCH + 2]
                g_w = gates[:, h * N_BRANCH + 2:h * N_BRANCH + 3]
                o_h = g_c * o_c[g][rows] + g_s * o_s[rows] + g_w * o_w[rows]
                outs.append(o_h[:, :HEAD_DIM])
        o_ref[0] = jnp.concatenate(outs, axis=-1)

    if shifted:
        def window_tiles(g, n_win):
            acc = _shifted_pv(q[g], key_rows(kw_ref, g, qi), key_rows(vw_ref, g, qi), b0_ref[g])
            if n_win >= 2:
                acc += _shifted_pv(q[g], key_rows(kw_ref, g, kt1), key_rows(vw_ref, g, kt1), b1_ref[g])
            if n_win >= 3:
                acc += _shifted_pv(q[g], key_rows(kw_ref, g, kt2), key_rows(vw_ref, g, kt2), w2_ref[...])
            return acc

        def first_block(n_win, rank):
            acc_s[...] = jnp.zeros(acc_s.shape, F32)
            for g in groups:
                acc_w[g] = window_tiles(g, n_win)
                oc_ref[g] = compressed_and_selection([g], rank)[0]

        n_all = (min(SEL_TOPK, ks_ref.shape[2] // SEL_BLOCK) * SEL_BLOCK) // TQ
        pl.when(qi == 0)(functools.partial(first_block, 1, n_all <= 0))
        pl.when(qi == 1)(functools.partial(first_block, 2, n_all <= 1))
        if n_all > 2:
            pl.when((qi >= 2) & (qi < n_all))(functools.partial(first_block, 3, False))
        pl.when(qi >= max(n_all, 2))(functools.partial(first_block, 3, True))

        n_plain = jnp.maximum(qi - 1, 0)

        def pair_body(it, carry):
            pv = [None] * N_GROUPS
            for half in range(2):
                for g in groups:
                    t = _shifted_pv(qa_ref[g], key_rows(ks_ref, g, 2 * it + half),
                                    key_rows(vs_ref, g, 2 * it + half), None)
                    pv[g] = t if pv[g] is None else pv[g] + t
            for g in groups:
                acc_s[g] += pv[g]
            return carry

        lax.fori_loop(0, n_plain // 2, pair_body, 0)

        def last_block(with_prev, with_odd):
            a_s = []
            for g in groups:
                acc = acc_s[g] + _shifted_pv(qa_ref[g], key_rows(ks_ref, g, qi), key_rows(vs_ref, g, qi),
                                             b0_ref[g])
                if with_prev:
                    acc += _shifted_pv(qa_ref[g], key_rows(ks_ref, g, kt1), key_rows(vs_ref, g, kt1),
                                       b1_ref[g])
                if with_odd:
                    acc += _shifted_pv(qa_ref[g], key_rows(ks_ref, g, kt2), key_rows(vs_ref, g, kt2), None)
                a_s.append(acc)
            finish(a_s, [acc_w[g] for g in groups], [oc_ref[g] for g in groups])

        is_odd = n_plain % 2 == 1
        pl.when(qi == 0)(functools.partial(last_block, False, False))
        pl.when((qi >= 1) & jnp.logical_not(is_odd))(functools.partial(last_block, True, False))
        pl.when(is_odd)(functools.partial(last_block, True, True))
    else:
        o_c = compressed_and_selection()
        a_s, a_w = [], []
        for g in groups:
            qa = qa_ref[g]
            sel = (acc_s.at[g], m_s.at[g])
            win = (acc_w.at[g], m_w.at[g])
            for acc, m_ref in (sel, win):
                acc[...] = jnp.zeros(acc.shape, F32)
                m_ref[...] = jnp.full(m_ref.shape, NEG_INF, F32)

            def sel_body(kt, carry, g=g, qa=qa, sel=sel):
                _online_update(qa, key_rows(ks_ref, g, kt), key_rows(vs_ref, g, kt), None, *sel)
                return carry

            lax.fori_loop(0, qi - 1, sel_body, 0)

            @pl.when(qi >= 2)
            def _(g=g, win=win):
                _online_update(q[g], key_rows(kw_ref, g, kt2), key_rows(vw_ref, g, kt2), w2_ref[...], *win)

            @pl.when(qi >= 1)
            def _(g=g, qa=qa, sel=sel, win=win):
                _online_update(qa, key_rows(ks_ref, g, kt1), key_rows(vs_ref, g, kt1), b1_ref[g], *sel)
                _online_update(q[g], key_rows(kw_ref, g, kt1), key_rows(vw_ref, g, kt1), b1_ref[g], *win)

            _online_update(qa, key_rows(ks_ref, g, qi), key_rows(vs_ref, g, qi), b0_ref[g], *sel)
            _online_update(q[g], key_rows(kw_ref, g, qi), key_rows(vw_ref, g, qi), b0_ref[g], *win)
            a_s.append(acc_s[g])
            a_w.append(acc_w[g])

        finish(a_s, a_w, o_c)


def _attn_kernel(flag_ref, *refs):
    use_shift = flag_ref[0] == 1

    @pl.when(use_shift)
    def _():
        _attn_step(True, *refs)

    @pl.when(jnp.logical_not(use_shift))
    def _():
        _attn_step(False, *refs)


def _attn(flag, q, kct, vc, ks, vs, kw, vw, gate, cb, b0, b1, w2, ovlt):
    bsz, _, seq, _ = q.shape
    nq = seq // TQ
    per_batch = lambda b, i: (b, 0, 0, 0)
    k_spec = pl.BlockSpec((1, N_GROUPS, seq, LANES), per_batch)
    v_spec = pl.BlockSpec((1, N_GROUPS, seq, V_LANES), per_batch)
    cmp_spec = pl.BlockSpec((1, N_GROUPS, N_CMP_PAD, LANES), per_batch)
    const = lambda shape: pl.BlockSpec(shape, lambda b, i: (0,) * len(shape),
                                       pipeline_mode=pl.Buffered(1))
    return pl.pallas_call(
        _attn_kernel,
        grid=(bsz, nq),
        in_specs=[pl.BlockSpec(memory_space=pltpu.SMEM),
                  pl.BlockSpec((1, N_HEADS, TQ, LANES), lambda b, i: (b, 0, i, 0)),
                  cmp_spec, cmp_spec, k_spec, v_spec, k_spec, v_spec,
                  pl.BlockSpec((1, N_GROUPS, TQ, LANES), lambda b, i: (b, 0, i, 0)),
                  pl.BlockSpec((N_GROUPS, 1, N_CMP_PAD, ROWS), lambda b, i: (0, i, 0, 0)),
                  const(b0.shape), const(b1.shape), const(w2.shape), const(ovlt.shape)],
        out_specs=pl.BlockSpec((1, TQ, D_ATTN), lambda b, i: (b, i, 0)),
        out_shape=jax.ShapeDtypeStruct((bsz, seq, D_ATTN), F32),
        scratch_shapes=[pltpu.VMEM((N_GROUPS, ROWS, LANES), BF16),
                        pltpu.VMEM((N_GROUPS, ROWS, LANES), F32),
                        pltpu.VMEM((N_GROUPS, ROWS, V_LANES), F32),
                        pltpu.VMEM((N_GROUPS, ROWS, LANES), F32),
                        pltpu.VMEM((N_GROUPS, ROWS, V_LANES), F32),
                        pltpu.VMEM((N_GROUPS, ROWS, LANES), F32)],
        compiler_params=pltpu.CompilerParams(
            dimension_semantics=("arbitrary", "arbitrary"),
            vmem_limit_bytes=VMEM_LIMIT),
        name="attn",
    )(flag, q, kct, vc, ks, vs, kw, vw, gate, cb, b0, b1, w2, ovlt)


TS_OUT = 512
HALO = 16


def _outproj_ffn_kernel(ya_ref, bg_ref, u_ref, uh_ref, cw_ref, ga_ref, gc_ref, w_ref, x_ref, mod_ref,
                        g2_ref, w1_ref, w3_ref, w2_ref, o_ref):
    u = u_ref[0].astype(F32)
    halo = uh_ref[0].astype(F32) * jnp.where(pl.program_id(1) > 0, 1.0, 0.0)
    h1 = halo[HALO - 1:HALO, :]
    h2 = halo[HALO - 2:HALO - 1, :]
    row = lax.broadcasted_iota(jnp.int32, u.shape, 0)
    u1 = jnp.where(row == 0, h1, pltpu.roll(u, 1, 0))
    u2 = jnp.where(row == 0, h2, jnp.where(row == 1, h1, pltpu.roll(u, 2, 0)))
    conv = u2 * cw_ref[0:1, :] + u1 * cw_ref[1:2, :] + u * cw_ref[2:3, :]
    yc = bg_ref[0].astype(F32) * conv
    yc = yc * lax.rsqrt(jnp.mean(yc * yc, axis=-1, keepdims=True) + EPS) * gc_ref[...]
    ya = ya_ref[0]
    ya = ya * lax.rsqrt(jnp.mean(ya * ya, axis=-1, keepdims=True) + EPS) * ga_ref[...]
    proj = _dot(ya.astype(BF16), w_ref[0:D_ATTN, :]) + _dot(yc.astype(BF16), w_ref[D_ATTN:, :])
    x = x_ref[0] + mod_ref[0, 2:3, :] * proj

    ms = jnp.mean(x * x, axis=-1, keepdims=True)
    h = x * lax.rsqrt(ms + EPS) * g2_ref[...] * (1.0 + mod_ref[0, 4:5, :]) + mod_ref[0, 3:4, :]
    hb = h.astype(BF16)
    a = _dot(hb, w1_ref[...])
    b = _dot(hb, w3_ref[...])
    act = (_silu(a) * b).astype(BF16)
    o_ref[0] = x + mod_ref[0, 5:6, :] * _dot(act, w2_ref[...])


def _outproj_ffn(ya, bg, u, conv_w, ga, gc, w_out, x, mod3, g2, w1, w3, w2):
    bsz, seq, d = x.shape
    ts = TS_OUT
    c2 = lambda b, i: (0, 0)
    row_spec = lambda width: pl.BlockSpec((1, ts, width), lambda b, i: (b, i, 0))
    wspec = lambda w: pl.BlockSpec(w.shape, c2, pipeline_mode=pl.Buffered(1))
    return pl.pallas_call(
        _outproj_ffn_kernel,
        grid=(bsz, seq // ts),
        in_specs=[row_spec(D_ATTN), row_spec(D_CONV), row_spec(D_CONV),
                  pl.BlockSpec((1, HALO, D_CONV),
                               lambda b, i: (b, jnp.maximum(i * (ts // HALO) - 1, 0), 0)),
                  pl.BlockSpec(conv_w.shape, c2),
                  pl.BlockSpec((1, D_ATTN), c2), pl.BlockSpec((1, D_CONV), c2),
                  wspec(w_out),
                  row_spec(d),
                  pl.BlockSpec((1, 6, d), lambda b, i: (b, 0, 0)),
                  pl.BlockSpec((1, d), c2),
                  wspec(w1), wspec(w3), wspec(w2)],
        out_specs=row_spec(d),
        out_shape=jax.ShapeDtypeStruct((bsz, seq, d), F32),
        compiler_params=pltpu.CompilerParams(
            dimension_semantics=("arbitrary", "arbitrary"), vmem_limit_bytes=VMEM_LIMIT),
        name="outproj_ffn",
    )(ya, bg, u, u, conv_w, ga, gc, w_out, x, mod3, g2, w1, w3, w2)


def _t5_bucket_np(rel):
    n = np.maximum(rel, 0)
    max_exact = N_BUCKETS // 2
    nf = np.maximum(n, 1).astype(np.float32)
    large = max_exact + (np.log(nf / max_exact) / math.log(MAX_DISTANCE / max_exact)
                         * (N_BUCKETS - max_exact)).astype(np.int32)
    return np.where(n < max_exact, n, np.minimum(large, N_BUCKETS - 1)).astype(np.int32)


def _bias_of_rel(tab_rel, rel):
    idx = jnp.asarray(_t5_bucket_np(rel))
    tab_b = tab_rel.T.reshape((N_HEADS, N_BUCKETS) + (1,) * idx.ndim)
    vals = jnp.zeros((N_HEADS,) + idx.shape, F32)
    for k in range(N_BUCKETS):
        vals = jnp.where(idx == k, tab_b[:, k], vals)
    return jnp.where(jnp.asarray(rel >= 0), vals, NEG_INF)


def _toeplitz_offsets(length, n_cols):
    k = np.arange(length)
    return np.where(k < n_cols, k, k - length)


TOEPLITZ_LEN = 2 * TQ
PATTERN_LEN = 3 * LANES


def _tables_kernel(v0_ref, v1_ref, vp_ref, b0_ref, b1_ref, cb_ref):
    def toeplitz_tile(v_ref):
        rows = jnp.broadcast_to(v_ref[0], (TQ, TOEPLITZ_LEN))
        return pltpu.roll(rows, 0, 1, stride=1, stride_axis=0)[:, :TQ]

    b0_ref[0] = toeplitz_tile(v0_ref)
    b1_ref[0] = toeplitz_tile(v1_ref)
    vp = vp_ref[0]
    shift = TQ // CMP_STRIDE
    pat = jnp.concatenate(
        [vp[:, :2 * N_CMP_PAD]] + [pltpu.roll(vp, a, 1)[:, :2 * N_CMP_PAD] for a in range(1, shift)],
        axis=0)
    for qi in range(cb_ref.shape[1]):
        cb_ref[0, qi] = pat[:, N_CMP_PAD - qi * shift:2 * N_CMP_PAD - qi * shift].T


def _bias_tables(rel_bias_table, seq):
    tab_rel = (rel_bias_table - rel_bias_table[N_BUCKETS - 1:N_BUCKETS, :]) * LOG2E
    nq = seq // TQ
    n_cmp = (seq - CMP_BLOCK) // CMP_STRIDE + 1
    shift = TQ // CMP_STRIDE
    assert nq * shift <= N_CMP_PAD and n_cmp * CMP_STRIDE + CMP_BLOCK - 1 > seq
    assert shift + 2 * N_CMP_PAD - 1 <= PATTERN_LEN
    d = _toeplitz_offsets(TOEPLITZ_LEN, TQ)
    v0 = _bias_of_rel(tab_rel, -d)[:, None, :]
    v1 = _bias_of_rel(tab_rel, TQ - d)[:, None, :]
    off = N_CMP_PAD * CMP_STRIDE - (CMP_BLOCK - 1)
    dp = _toeplitz_offsets(PATTERN_LEN, 2 * N_CMP_PAD)[None, :]
    vp = _bias_of_rel(tab_rel, np.arange(CMP_STRIDE)[:, None] + off - CMP_STRIDE * dp)
    head = lambda h: (h // HPG, h % HPG, 0)
    b0, b1, cb = pl.pallas_call(
        _tables_kernel,
        grid=(N_HEADS,),
        in_specs=[pl.BlockSpec((1, 1, TOEPLITZ_LEN), lambda h: (h, 0, 0)),
                  pl.BlockSpec((1, 1, TOEPLITZ_LEN), lambda h: (h, 0, 0)),
                  pl.BlockSpec((1, CMP_STRIDE, PATTERN_LEN), lambda h: (h, 0, 0))],
        out_specs=[pl.BlockSpec((1, TQ, TQ), head), pl.BlockSpec((1, TQ, TQ), head),
                   pl.BlockSpec((1, nq, N_CMP_PAD, TQ), lambda h: (h // HPG, 0, 0, h % HPG))],
        out_shape=[jax.ShapeDtypeStruct((N_GROUPS, ROWS, TQ), F32),
                   jax.ShapeDtypeStruct((N_GROUPS, ROWS, TQ), F32),
                   jax.ShapeDtypeStruct((N_GROUPS, nq, N_CMP_PAD, ROWS), F32)],
        compiler_params=pltpu.CompilerParams(
            dimension_semantics=("arbitrary",), vmem_limit_bytes=VMEM_LIMIT),
        name="bias_tables",
    )(v0, v1, vp)
    i = np.arange(TQ)[:, None]
    jj = np.arange(TQ)[None, :]
    w2 = jnp.asarray(np.tile(np.where(jj > i, 0.0, NEG_INF).astype(np.float32), (HPG, 1)))
    n_sel = seq // SEL_BLOCK
    cs = np.arange(n_cmp) * CMP_STRIDE
    ce = cs + CMP_BLOCK - 1
    ss = np.arange(n_sel) * SEL_BLOCK
    ov = np.clip(np.minimum(ce[:, None], ss[None, :] + SEL_BLOCK - 1)
                 - np.maximum(cs[:, None], ss[None, :]) + 1, 0, None) / CMP_STRIDE
    ovlt = np.zeros((LANES, N_CMP_PAD), np.float32)
    ovlt[SEL_LANE0:SEL_LANE0 + n_sel, :n_cmp] = ov.T
    return tab_rel, cb, b0, b1, w2, jnp.asarray(ovlt, BF16)


def kernel(x, c, w_ada, b_ada, norm1_gain, w_in, q_gain, k_cmp_gain, k_sel_gain, k_win_gain,
           cmp_pos_k, cmp_pos_v, w_ck1, w_ck2, w_cv1, w_cv2, rel_bias_table, conv_w,
           attn_out_gain, conv_out_gain, w_out, norm2_gain, w_ff1, w_ff3, w_ff2):
    bsz, seq, d = x.shape
    assert d == D_MODEL and seq % TQ == 0 and SEL_LANE0 + seq // SEL_BLOCK <= SHIFT_LANE
    assert (seq - CMP_BLOCK) // CMP_STRIDE + 1 <= N_CMP_PAD and seq // CMP_STRIDE == N_CMP_PAD

    seg_of = lambda n: jnp.asarray(np.kron(np.eye(n), np.ones((HEAD_DIM, HEAD_DIM))) / HEAD_DIM, BF16)
    seg4 = seg_of(MXU_N // HEAD_DIM)
    tab_rel, cb, b0, b1, w2m, ovlt = _bias_tables(rel_bias_table, seq)
    bias_max = jnp.max(jnp.abs(tab_rel))
    two = lambda g: jnp.tile(g.reshape(1, HEAD_DIM), (1, 2))
    n_gate = N_BRANCH * HPG
    gate_pad = jnp.zeros((d, LANES - n_gate), BF16)

    for layer in range(w_in.shape[0]):
        wi = w_in[layer].astype(BF16)
        o_g = D_ATTN + 6 * D_KV
        kv = lambda n: wi[:, D_ATTN + n * D_KV:D_ATTN + (n + 1) * D_KV]
        w_p = jnp.concatenate(
            [wi[:, :D_ATTN], kv(0), kv(1), kv(2), kv(4), kv(3), kv(5),
             wi[:, o_g:o_g + n_gate], gate_pad,
             wi[:, o_g + n_gate:o_g + 2 * n_gate], gate_pad, wi[:, o_g + 2 * n_gate:]],
            axis=-1)
        w1k, w1v = w_ck1[layer].astype(BF16), w_cv1[layer].astype(BF16)
        w2k, w2v = w_ck2[layer].astype(BF16), w_cv2[layer].astype(BF16)
        pk2 = cmp_pos_k[layer].reshape(2, CMP_STRIDE * HEAD_DIM)
        pv2 = cmp_pos_v[layer].reshape(2, CMP_STRIDE * HEAD_DIM)

        gq_max = jnp.max(jnp.abs(q_gain[layer]))
        bound = lambda gk: (HEAD_DIM ** 0.5 * LOG2E * 1.01) * gq_max * jnp.max(jnp.abs(gk)) + bias_max
        c_sel, c_win = bound(k_sel_gain[layer]), bound(k_win_gain[layer])
        flag = (2.0 * jnp.maximum(c_sel, c_win) <= MAX_SHIFTED_RANGE).astype(jnp.int32).reshape(1)
        tails = jnp.zeros((2, HEAD_DIM), F32).at[:, SHIFT_LANE - HEAD_DIM].set(
            -jnp.stack([c_sel, c_win]))

        mod3 = _adaln(c, w_ada[layer], b_ada[layer]).reshape(bsz, 6, d)
        q, kvc, ks, vs, kw, vw, gate, bg, u = _inproj(
            x, mod3, norm1_gain[layer].reshape(1, d), w_p, seg4,
            jnp.tile(q_gain[layer].reshape(1, HEAD_DIM), (1, MXU_N // HEAD_DIM)),
            jnp.concatenate([two(k_sel_gain[layer]), two(k_win_gain[layer])], axis=-1), tails)
        kct, vc = _compress(kvc, w1k, w1v, w2k, w2v, pk2, pv2,
                            k_cmp_gain[layer].reshape(1, HEAD_DIM))
        y_attn = _attn(flag, q, kct, vc, ks, vs, kw, vw, gate, cb, b0, b1, w2m, ovlt)
        x = _outproj_ffn(y_attn, bg, u, conv_w[layer], attn_out_gain[layer].reshape(1, D_ATTN),
                         conv_out_gain[layer].reshape(1, D_CONV), w_out[layer].astype(BF16), x, mod3,
                         norm2_gain[layer].reshape(1, d), w_ff1[layer].astype(BF16),
                         w_ff3[layer].astype(BF16), w_ff2[layer].astype(BF16))
    return x
```

```python
import functools
import math

import numpy as np
import jax
import jax.numpy as jnp
from jax import lax
from jax.experimental import pallas as pl
from jax.experimental.pallas import tpu as pltpu

F32 = jnp.float32
BF16 = jnp.bfloat16

D_MODEL = 1024
HEAD_DIM = 64
N_HEADS = 8
N_GROUPS = 2
HPG = N_HEADS // N_GROUPS
D_ATTN = N_HEADS * HEAD_DIM
D_KV = N_GROUPS * HEAD_DIM
N_BRANCH = 3
D_CONV = D_MODEL - D_ATTN
CONV_WIDTH = 3
CMP_BLOCK = 32
CMP_STRIDE = 16
CMP_HIDDEN = 256
SEL_BLOCK = 64
SEL_TOPK = 16
N_LOCAL_FORCED = 2
WINDOW = 512
N_BUCKETS = 32
MAX_DISTANCE = 128
EPS = 1e-6
NEG_INF = -1e30
FORCED_SCORE = 1e6

LANES = 128
SUBLANES = 8
TQ = 256
ROWS = HPG * TQ
N_CMP_PAD = 128
SEL_LANE0 = 64
SHIFT_LANE = 96
V_LANES = 256
LOG2E = math.log2(math.e)
MAX_SHIFTED_RANGE = 100.0
VMEM_LIMIT = 56 * 1024 * 1024

_NT = (((1,), (1,)), ((), ()))


def _dot(a, b):
    return jnp.dot(a, b, preferred_element_type=F32)


def _silu(v):
    return v * (1.0 / (1.0 + jnp.exp(-v)))


def _sigmoid(v):
    return 1.0 / (1.0 + jnp.exp(-v))


def _seg_mean_sq(v, seg):
    sq = v * v
    hi = sq.astype(BF16)
    lo = (sq - hi.astype(F32)).astype(BF16)
    return _dot(hi, seg) + _dot(lo, seg)


TN_ADALN = 1536


def _adaln_kernel(c_ref, w_ref, b_ref, o_ref):
    sc = _silu(c_ref[...]).astype(BF16)
    o_ref[...] = _dot(sc, w_ref[...].astype(BF16)) + b_ref[...]


def _adaln(c, w_ada, b_ada):
    bsz, d = c.shape
    n = w_ada.shape[1]
    tn = TN_ADALN
    return pl.pallas_call(
        _adaln_kernel,
        grid=(n // tn,),
        in_specs=[pl.BlockSpec((bsz, d), lambda j: (0, 0)),
                  pl.BlockSpec((d, tn), lambda j: (0, j)),
                  pl.BlockSpec((1, tn), lambda j: (0, j))],
        out_specs=pl.BlockSpec((bsz, tn), lambda j: (0, j)),
        out_shape=jax.ShapeDtypeStruct((bsz, n), F32),
        compiler_params=pltpu.CompilerParams(
            dimension_semantics=("arbitrary",), vmem_limit_bytes=VMEM_LIMIT),
        name="adaln",
    )(c, w_ada, b_ada.reshape(1, n))


C_Q, C_KVC, C_KK, C_VV, C_GATE, C_BG, C_CG, C_XT, C_END = (
    0, 512, 768, 1024, 1280, 1536, 2048, 2560, 3072)
MXU_N = 256
TS_IN = 1024


def _inproj_kernel(x_ref, mod_ref, g1_ref, w_ref, seg_ref, gq_ref, gkk_ref, tail_ref,
                   q_ref, kvc_ref, ks_ref, vs_ref, kw_ref, vw_ref, gate_ref, bg_ref, u_ref):
    ts = x_ref.shape[1]
    x = x_ref[0]
    ms = jnp.mean(x * x, axis=-1, keepdims=True)
    y = x * lax.rsqrt(ms + EPS) * g1_ref[...]
    h = y * (1.0 + mod_ref[0, 1:2, :]) + mod_ref[0, 0:1, :]
    hb = h.astype(BF16)
    seg = seg_ref[...]

    def proj(c0, c1):
        return _dot(hb, w_ref[:, c0:c1])

    lane = lax.broadcasted_iota(jnp.int32, (ts, HEAD_DIM), 1)
    ones192 = jnp.ones((ts, V_LANES - HEAD_DIM), F32)
    q_tail = jnp.where(lane == SHIFT_LANE - HEAD_DIM, 1.0, 0.0).astype(F32)

    heads_per_tile = MXU_N // HEAD_DIM
    row = pl.program_id(1) * ts + lax.broadcasted_iota(jnp.int32, (ts, HEAD_DIM), 0)
    onehot = jnp.where(lane == row // SEL_BLOCK, 1.0, 0.0).astype(F32) + tail_ref[0:1, :]
    win_tail = jnp.broadcast_to(tail_ref[1:2, :], (ts, HEAD_DIM))

    def do_q(c):
        v = proj(C_Q + c * MXU_N, C_Q + (c + 1) * MXU_N)
        vn = v * lax.rsqrt(_seg_mean_sq(v, seg) + EPS) * gq_ref[...] * (HEAD_DIM ** -0.5 * LOG2E)
        for hh in range(heads_per_tile):
            qh = jnp.concatenate([vn[:, hh * HEAD_DIM:(hh + 1) * HEAD_DIM], q_tail], axis=-1)
            q_ref[0, c * heads_per_tile + hh] = qh.astype(BF16)

    def do_kvc():
        v = proj(C_KVC, C_KK)
        kvc_ref[0, 0] = v[:, :D_KV]
        kvc_ref[0, 1] = v[:, D_KV:]

    def do_kk():
        v = proj(C_KK, C_VV)
        vn = v * lax.rsqrt(_seg_mean_sq(v, seg) + EPS) * gkk_ref[...]
        for g in range(N_GROUPS):
            ks_ref[0, g] = jnp.concatenate(
                [vn[:, g * HEAD_DIM:(g + 1) * HEAD_DIM], onehot], axis=-1).astype(BF16)
            kw_ref[0, g] = jnp.concatenate(
                [vn[:, D_KV + g * HEAD_DIM:D_KV + (g + 1) * HEAD_DIM], win_tail], axis=-1).astype(BF16)

    def do_vv():
        v = proj(C_VV, C_GATE)
        for g in range(N_GROUPS):
            vs_ref[0, g] = jnp.concatenate(
                [v[:, g * HEAD_DIM:(g + 1) * HEAD_DIM], ones192], axis=-1).astype(BF16)
            vw_ref[0, g] = jnp.concatenate(
                [v[:, D_KV + g * HEAD_DIM:D_KV + (g + 1) * HEAD_DIM], ones192], axis=-1).astype(BF16)

    def do_gate():
        v = _sigmoid(proj(C_GATE, C_BG))
        for g in range(N_GROUPS):
            gate_ref[0, g] = v[:, g * LANES:(g + 1) * LANES]

    def do_bg(c):
        cols = slice(c * MXU_N, (c + 1) * MXU_N)
        bg_ref[0, :, cols] = proj(C_BG + c * MXU_N, C_BG + (c + 1) * MXU_N).astype(BF16)

    def do_u(c):
        cols = slice(c * MXU_N, (c + 1) * MXU_N)
        u_ref[0, :, cols] = (proj(C_CG + c * MXU_N, C_CG + (c + 1) * MXU_N)
                             * proj(C_XT + c * MXU_N, C_XT + (c + 1) * MXU_N)).astype(BF16)

    do_q(0)
    do_bg(0)
    do_q(1)
    do_bg(1)
    do_kvc()
    do_u(0)
    do_kk()
    do_u(1)
    do_gate()
    do_vv()


def _inproj(x, mod3, g1, w_p, seg, gq4, gkk4, tails):
    bsz, seq, d = x.shape
    ts = TS_IN
    const2 = lambda b, i: (0, 0)
    k_shape = jax.ShapeDtypeStruct((bsz, N_GROUPS, seq, LANES), BF16)
    k_spec = pl.BlockSpec((1, N_GROUPS, ts, LANES), lambda b, i: (b, 0, i, 0))
    v_shape = jax.ShapeDtypeStruct((bsz, N_GROUPS, seq, V_LANES), BF16)
    v_spec = pl.BlockSpec((1, N_GROUPS, ts, V_LANES), lambda b, i: (b, 0, i, 0))
    return pl.pallas_call(
        _inproj_kernel,
        grid=(bsz, seq // ts),
        in_specs=[pl.BlockSpec((1, ts, d), lambda b, i: (b, i, 0)),
                  pl.BlockSpec((1, 6, d), lambda b, i: (b, 0, 0)),
                  pl.BlockSpec((1, d), const2),
                  pl.BlockSpec(w_p.shape, const2, pipeline_mode=pl.Buffered(1)),
                  pl.BlockSpec(seg.shape, const2),
                  pl.BlockSpec(gq4.shape, const2),
                  pl.BlockSpec(gkk4.shape, const2),
                  pl.BlockSpec(tails.shape, const2)],
        out_specs=[pl.BlockSpec((1, N_HEADS, ts, LANES), lambda b, i: (b, 0, i, 0)),
                   pl.BlockSpec((1, 2, ts, D_KV), lambda b, i: (b, 0, i, 0)),
                   k_spec, v_spec, k_spec, v_spec,
                   pl.BlockSpec((1, N_GROUPS, ts, LANES), lambda b, i: (b, 0, i, 0)),
                   pl.BlockSpec((1, ts, D_CONV), lambda b, i: (b, i, 0)),
                   pl.BlockSpec((1, ts, D_CONV), lambda b, i: (b, i, 0))],
        out_shape=[jax.ShapeDtypeStruct((bsz, N_HEADS, seq, LANES), BF16),
                   jax.ShapeDtypeStruct((bsz, 2, seq, D_KV), F32),
                   k_shape, v_shape, k_shape, v_shape,
                   jax.ShapeDtypeStruct((bsz, N_GROUPS, seq, LANES), F32),
                   jax.ShapeDtypeStruct((bsz, seq, D_CONV), BF16),
                   jax.ShapeDtypeStruct((bsz, seq, D_CONV), BF16)],
        compiler_params=pltpu.CompilerParams(
            dimension_semantics=("arbitrary", "arbitrary"), vmem_limit_bytes=VMEM_LIMIT),
        name="inproj",
    )(x, mod3, g1, w_p, seg, gq4, gkk4, tails)


def _compress_kernel(kvc_ref, w1k_ref, w1v_ref, w2k_ref, w2v_ref, pk_ref, pv_ref, gk_ref,
                     kct_ref, vc_ref):
    n_rows = kvc_ref.shape[2] // CMP_STRIDE
    half = CMP_STRIDE * HEAD_DIM
    zlane = jnp.zeros((n_rows, HEAD_DIM), F32)

    def branch(which, w1_ref, w2_ref, pos_ref):
        rows = [kvc_ref[0, which, pl.ds(r, n_rows, stride=CMP_STRIDE), :] for r in range(CMP_STRIDE)]
        outs = []
        for g in range(N_GROUPS):
            x = jnp.concatenate([rw[:, g * HEAD_DIM:(g + 1) * HEAD_DIM] for rw in rows], axis=-1)
            a1 = (x + pos_ref[0:1, :]).astype(BF16)
            a2 = (x + pos_ref[1:2, :]).astype(BF16)
            hid = _dot(a1, w1_ref[0:half, :]) + pltpu.roll(_dot(a2, w1_ref[half:, :]), n_rows - 1, 0)
            outs.append(_dot(_silu(hid).astype(BF16), w2_ref[...]))
        return outs

    kc = branch(0, w1k_ref, w2k_ref, pk_ref)
    vc = branch(1, w1v_ref, w2v_ref, pv_ref)
    for g in range(N_GROUPS):
        kn = kc[g] * lax.rsqrt(jnp.mean(kc[g] * kc[g], axis=-1, keepdims=True) + EPS) * gk_ref[...]
        kct_ref[0, g] = jnp.concatenate([kn, zlane], axis=-1).astype(BF16)
        vc_ref[0, g] = jnp.concatenate([vc[g], zlane], axis=-1).T.astype(BF16)


def _compress(kvc, w1k, w1v, w2k, w2v, pk2, pv2, gk):
    bsz, _, seq, _ = kvc.shape
    c2 = lambda b: (0, 0)
    out_shape = jax.ShapeDtypeStruct((bsz, N_GROUPS, N_CMP_PAD, LANES), BF16)
    out_spec = pl.BlockSpec((1, N_GROUPS, N_CMP_PAD, LANES), lambda b: (b, 0, 0, 0))
    return pl.pallas_call(
        _compress_kernel,
        grid=(bsz,),
        in_specs=[pl.BlockSpec((1, 2, seq, D_KV), lambda b: (b, 0, 0, 0)),
                  pl.BlockSpec(w1k.shape, c2), pl.BlockSpec(w1v.shape, c2),
                  pl.BlockSpec(w2k.shape, c2), pl.BlockSpec(w2v.shape, c2),
                  pl.BlockSpec(pk2.shape, c2), pl.BlockSpec(pv2.shape, c2),
                  pl.BlockSpec(gk.shape, c2)],
        out_specs=[out_spec, out_spec],
        out_shape=[out_shape, out_shape],
        compiler_params=pltpu.CompilerParams(
            dimension_semantics=("arbitrary",), vmem_limit_bytes=VMEM_LIMIT),
        name="compress",
    )(kvc, w1k, w1v, w2k, w2v, pk2, pv2, gk)


def _logits(qa, k_rows, bias):
    s = lax.dot_general(qa, k_rows, _NT, preferred_element_type=F32)
    return s if bias is None else s + bias


def _shifted_pv(qa, k_rows, v_rows, bias):
    return _dot(jnp.exp2(_logits(qa, k_rows, bias)).astype(BF16), v_rows)


def _online_update(qa, k_tile, v_tile, bias, acc_ref, m_ref):
    s = _logits(qa, k_tile, bias)
    m_prev = m_ref[...]
    m_new = jnp.maximum(m_prev, jnp.max(s, axis=-1, keepdims=True))
    alpha = jnp.exp2(m_prev - m_new)
    p = jnp.exp2(s - jnp.concatenate([m_new] * (TQ // LANES), axis=-1))
    acc_ref[...] = (acc_ref[...] * jnp.concatenate([alpha] * (V_LANES // LANES), axis=-1)
                    + _dot(p.astype(BF16), v_tile))
    m_ref[...] = m_new


def _selection_bias(pct, qi, ovlt, n_sel):
    psum = pct[:, 0:TQ] + pct[:, TQ:2 * TQ] + pct[:, 2 * TQ:3 * TQ] + pct[:, 3 * TQ:4 * TQ]
    hi = psum.astype(BF16)
    lo = (psum - hi.astype(F32)).astype(BF16)
    pslc_t = _dot(ovlt, hi) + _dot(ovlt, lo)
    top_k = min(SEL_TOPK, n_sel)
    sub = SUBLANES
    j = lax.broadcasted_iota(jnp.int32, (n_sel, TQ), 0)
    t = qi * TQ + lax.broadcasted_iota(jnp.int32, (n_sel, TQ), 1)
    dist = jnp.right_shift(t, int(math.log2(SEL_BLOCK))) - j
    score = jnp.where(dist < N_LOCAL_FORCED, FORCED_SCORE, pslc_t[SEL_LANE0:SEL_LANE0 + n_sel, :])
    score = jnp.where(j == 0, FORCED_SCORE, score)
    score = jnp.where(dist >= 0, score, NEG_INF)
    groups = [score[a * sub:(a + 1) * sub] for a in range(n_sel // sub)]
    cnts = [jnp.zeros((sub, TQ), F32) for _ in groups]
    j_sub = lax.broadcasted_iota(jnp.int32, (sub, TQ), 0)
    for i in range(n_sel):
        row = jnp.broadcast_to(score[i:i + 1, :], (sub, TQ))
        for a, grp in enumerate(groups):
            if a * sub > i:
                beats = jnp.where(row >= grp, 1.0, 0.0)
            elif a * sub + sub - 1 <= i:
                beats = jnp.where(row > grp, 1.0, 0.0)
            else:
                beats = jnp.where(j_sub + a * sub > i, jnp.where(row >= grp, 1.0, 0.0),
                                  jnp.where(row > grp, 1.0, 0.0))
            cnts[a] = cnts[a] + beats
    sel_t = jnp.where(jnp.concatenate(cnts, axis=0) < top_k, 0.0, NEG_INF)
    return jnp.concatenate(
        [jnp.zeros((SEL_LANE0, TQ), F32), sel_t,
         jnp.zeros((LANES - SEL_LANE0 - n_sel, TQ), F32)], axis=0).T


def _attn_step(shifted, q_ref, kct_ref, vc_ref, ks_ref, vs_ref, kw_ref, vw_ref, gate_ref,
               cb_ref, b0_ref, b1_ref, w2_ref, ovlt_ref, o_ref, qa_ref, oc_ref, acc_s, m_s, acc_w, m_w):
    qi = pl.program_id(1)
    groups = range(N_GROUPS)
    q = [q_ref[0, g * HPG:(g + 1) * HPG].reshape(ROWS, LANES) for g in groups]

    def key_rows(ref, g, kt, n_tiles=1):
        return ref[0, g, pl.ds(pl.multiple_of(kt * TQ, TQ), n_tiles * TQ), :]

    def compressed_and_selection(which=groups, rank=True):
        o_c = []
        for g in which:
            kc = kct_ref[0, g]
            st = lax.dot_general(kc, q[g], _NT, preferred_element_type=F32) + cb_ref[g, 0]
            m = jnp.max(st, axis=0, keepdims=True)
            e = jnp.exp2(st - m)
            l = jnp.sum(e, axis=0, keepdims=True)
            pct = e * jnp.where(m > 0.5 * NEG_INF, 1.0 / l, 0.0)
            vct = vc_ref[0, g]
            o_c.append(_dot(vct, pct.astype(BF16)).T)
            if not rank:
                qa_ref[g] = q[g]
                continue
            selbias = _selection_bias(pct, qi, ovlt_ref[...], ks_ref.shape[2] // SEL_BLOCK)
            for h in range(HPG):
                qa_ref[g, h * TQ:(h + 1) * TQ, :] = (
                    q_ref[0, g * HPG + h].astype(F32) + selbias).astype(BF16)
        return o_c

    kt1 = jnp.maximum(qi - 1, 0)
    kt2 = jnp.maximum(qi - 2, 0)
    def finish(a_s, a_w, o_c):
        outs = []
        for g in groups:
            o_s = a_s[g][:, :LANES] / a_s[g][:, LANES:]
            o_w = a_w[g][:, :LANES] / a_w[g][:, LANES:]
            gates = gate_ref[0, g]
            for h in range(HPG):
                rows = slice(h * TQ, (h + 1) * TQ)
                g_c = gates[:, h * N_BRANCH + 0:h * N_BRANCH + 1]
                g_s = gates[:, h * N_BRANCH + 1:h * N_BRANCH + 2]
                g_w = gates[:, h * N_BRANCH + 2:h * N_BRANCH + 3]
                o_h = g_c * o_c[g][rows] + g_s * o_s[rows] + g_w * o_w[rows]
                outs.append(o_h[:, :HEAD_DIM])
        o_ref[0] = jnp.concatenate(outs, axis=-1)

    if shifted:
        def window_tiles(g, n_win):
            acc = _shifted_pv(q[g], key_rows(kw_ref, g, qi), key_rows(vw_ref, g, qi), b0_ref[g])
            if n_win >= 2:
                acc += _shifted_pv(q[g], key_rows(kw_ref, g, kt1), key_rows(vw_ref, g, kt1), b1_ref[g])
            if n_win >= 3:
                acc += _shifted_pv(q[g], key_rows(kw_ref, g, kt2), key_rows(vw_ref, g, kt2), w2_ref[...])
            return acc

        def first_block(n_win, rank):
            acc_s[...] = jnp.zeros(acc_s.shape, F32)
            for g in groups:
                acc_w[g] = window_tiles(g, n_win)
                oc_ref[g] = compressed_and_selection([g], rank)[0]

        n_all = (min(SEL_TOPK, ks_ref.shape[2] // SEL_BLOCK) * SEL_BLOCK) // TQ
        pl.when(qi == 0)(functools.partial(first_block, 1, n_all <= 0))
        pl.when(qi == 1)(functools.partial(first_block, 2, n_all <= 1))
        if n_all > 2:
            pl.when((qi >= 2) & (qi < n_all))(functools.partial(first_block, 3, False))
        pl.when(qi >= max(n_all, 2))(functools.partial(first_block, 3, True))

        n_plain = jnp.maximum(qi - 1, 0)

        def pair_body(it, carry):
            pv = [None] * N_GROUPS
            for half in range(2):
                for g in groups:
                    t = _shifted_pv(qa_ref[g], key_rows(ks_ref, g, 2 * it + half),
                                    key_rows(vs_ref, g, 2 * it + half), None)
                    pv[g] = t if pv[g] is None else pv[g] + t
            for g in groups:
                acc_s[g] += pv[g]
            return carry

        lax.fori_loop(0, n_plain // 2, pair_body, 0)

        def last_block(with_prev, with_odd):
            a_s = []
            for g in groups:
                acc = acc_s[g] + _shifted_pv(qa_ref[g], key_rows(ks_ref, g, qi), key_rows(vs_ref, g, qi),
                                             b0_ref[g])
                if with_prev:
                    acc += _shifted_pv(qa_ref[g], key_rows(ks_ref, g, kt1), key_rows(vs_ref, g, kt1),
                                       b1_ref[g])
                if with_odd:
                    acc += _shifted_pv(qa_ref[g], key_rows(ks_ref, g, kt2), key_rows(vs_ref, g, kt2), None)
                a_s.append(acc)
            finish(a_s, [acc_w[g] for g in groups], [oc_ref[g] for g in groups])

        is_odd = n_plain % 2 == 1
        pl.when(qi == 0)(functools.partial(last_block, False, False))
        pl.when((qi >= 1) & jnp.logical_not(is_odd))(functools.partial(last_block, True, False))
        pl.when(is_odd)(functools.partial(last_block, True, True))
    else:
        o_c = compressed_and_selection()
        a_s, a_w = [], []
        for g in groups:
            qa = qa_ref[g]
            sel = (acc_s.at[g], m_s.at[g])
            win = (acc_w.at[g], m_w.at[g])
            for acc, m_ref in (sel, win):
                acc[...] = jnp.zeros(acc.shape, F32)
                m_ref[...] = jnp.full(m_ref.shape, NEG_INF, F32)

            def sel_body(kt, carry, g=g, qa=qa, sel=sel):
                _online_update(qa, key_rows(ks_ref, g, kt), key_rows(vs_ref, g, kt), None, *sel)
                return carry

            lax.fori_loop(0, qi - 1, sel_body, 0)

            @pl.when(qi >= 2)
            def _(g=g, win=win):
                _online_update(q[g], key_rows(kw_ref, g, kt2), key_rows(vw_ref, g, kt2), w2_ref[...], *win)

            @pl.when(qi >= 1)
            def _(g=g, qa=qa, sel=sel, win=win):
                _online_update(qa, key_rows(ks_ref, g, kt1), key_rows(vs_ref, g, kt1), b1_ref[g], *sel)
                _online_update(q[g], key_rows(kw_ref, g, kt1), key_rows(vw_ref, g, kt1), b1_ref[g], *win)

            _online_update(qa, key_rows(ks_ref, g, qi), key_rows(vs_ref, g, qi), b0_ref[g], *sel)
            _online_update(q[g], key_rows(kw_ref, g, qi), key_rows(vw_ref, g, qi), b0_ref[g], *win)
            a_s.append(acc_s[g])
            a_w.append(acc_w[g])

        finish(a_s, a_w, o_c)


def _attn_kernel(flag_ref, *refs):
    use_shift = flag_ref[0] == 1

    @pl.when(use_shift)
    def _():
        _attn_step(True, *refs)

    @pl.when(jnp.logical_not(use_shift))
    def _():
        _attn_step(False, *refs)


def _attn(flag, q, kct, vc, ks, vs, kw, vw, gate, cb, b0, b1, w2, ovlt):
    bsz, _, seq, _ = q.shape
    nq = seq // TQ
    per_batch = lambda b, i: (b, 0, 0, 0)
    k_spec = pl.BlockSpec((1, N_GROUPS, seq, LANES), per_batch)
    v_spec = pl.BlockSpec((1, N_GROUPS, seq, V_LANES), per_batch)
    cmp_spec = pl.BlockSpec((1, N_GROUPS, N_CMP_PAD, LANES), per_batch)
    const = lambda shape: pl.BlockSpec(shape, lambda b, i: (0,) * len(shape),
                                       pipeline_mode=pl.Buffered(1))
    return pl.pallas_call(
        _attn_kernel,
        grid=(bsz, nq),
        in_specs=[pl.BlockSpec(memory_space=pltpu.SMEM),
                  pl.BlockSpec((1, N_HEADS, TQ, LANES), lambda b, i: (b, 0, i, 0)),
                  cmp_spec, cmp_spec, k_spec, v_spec, k_spec, v_spec,
                  pl.BlockSpec((1, N_GROUPS, TQ, LANES), lambda b, i: (b, 0, i, 0)),
                  pl.BlockSpec((N_GROUPS, 1, N_CMP_PAD, ROWS), lambda b, i: (0, i, 0, 0)),
                  const(b0.shape), const(b1.shape), const(w2.shape), const(ovlt.shape)],
        out_specs=pl.BlockSpec((1, TQ, D_ATTN), lambda b, i: (b, i, 0)),
        out_shape=jax.ShapeDtypeStruct((bsz, seq, D_ATTN), F32),
        scratch_shapes=[pltpu.VMEM((N_GROUPS, ROWS, LANES), BF16),
                        pltpu.VMEM((N_GROUPS, ROWS, LANES), F32),
                        pltpu.VMEM((N_GROUPS, ROWS, V_LANES), F32),
                        pltpu.VMEM((N_GROUPS, ROWS, LANES), F32),
                        pltpu.VMEM((N_GROUPS, ROWS, V_LANES), F32),
                        pltpu.VMEM((N_GROUPS, ROWS, LANES), F32)],
        compiler_params=pltpu.CompilerParams(
            dimension_semantics=("arbitrary", "arbitrary"),
            vmem_limit_bytes=VMEM_LIMIT),
        name="attn",
    )(flag, q, kct, vc, ks, vs, kw, vw, gate, cb, b0, b1, w2, ovlt)


TS_OUT = 512
HALO = 16


def _outproj_ffn_kernel(ya_ref, bg_ref, u_ref, uh_ref, cw_ref, ga_ref, gc_ref, w_ref, x_ref, mod_ref,
                        g2_ref, w1_ref, w3_ref, w2_ref, o_ref):
    u = u_ref[0].astype(F32)
    halo = uh_ref[0].astype(F32) * jnp.where(pl.program_id(1) > 0, 1.0, 0.0)
    h1 = halo[HALO - 1:HALO, :]
    h2 = halo[HALO - 2:HALO - 1, :]
    row = lax.broadcasted_iota(jnp.int32, u.shape, 0)
    u1 = jnp.where(row == 0, h1, pltpu.roll(u, 1, 0))
    u2 = jnp.where(row == 0, h2, jnp.where(row == 1, h1, pltpu.roll(u, 2, 0)))
    conv = u2 * cw_ref[0:1, :] + u1 * cw_ref[1:2, :] + u * cw_ref[2:3, :]
    yc = bg_ref[0].astype(F32) * conv
    yc = yc * lax.rsqrt(jnp.mean(yc * yc, axis=-1, keepdims=True) + EPS) * gc_ref[...]
    ya = ya_ref[0]
    ya = ya * lax.rsqrt(jnp.mean(ya * ya, axis=-1, keepdims=True) + EPS) * ga_ref[...]
    proj = _dot(ya.astype(BF16), w_ref[0:D_ATTN, :]) + _dot(yc.astype(BF16), w_ref[D_ATTN:, :])
    x = x_ref[0] + mod_ref[0, 2:3, :] * proj

    ms = jnp.mean(x * x, axis=-1, keepdims=True)
    h = x * lax.rsqrt(ms + EPS) * g2_ref[...] * (1.0 + mod_ref[0, 4:5, :]) + mod_ref[0, 3:4, :]
    hb = h.astype(BF16)
    a = _dot(hb, w1_ref[...])
    b = _dot(hb, w3_ref[...])
    act = (_silu(a) * b).astype(BF16)
    o_ref[0] = x + mod_ref[0, 5:6, :] * _dot(act, w2_ref[...])


def _outproj_ffn(ya, bg, u, conv_w, ga, gc, w_out, x, mod3, g2, w1, w3, w2):
    bsz, seq, d = x.shape
    ts = TS_OUT
    c2 = lambda b, i: (0, 0)
    row_spec = lambda width: pl.BlockSpec((1, ts, width), lambda b, i: (b, i, 0))
    wspec = lambda w: pl.BlockSpec(w.shape, c2, pipeline_mode=pl.Buffered(1))
    return pl.pallas_call(
        _outproj_ffn_kernel,
        grid=(bsz, seq // ts),
        in_specs=[row_spec(D_ATTN), row_spec(D_CONV), row_spec(D_CONV),
                  pl.BlockSpec((1, HALO, D_CONV),
                               lambda b, i: (b, jnp.maximum(i * (ts // HALO) - 1, 0), 0)),
                  pl.BlockSpec(conv_w.shape, c2),
                  pl.BlockSpec((1, D_ATTN), c2), pl.BlockSpec((1, D_CONV), c2),
                  wspec(w_out),
                  row_spec(d),
                  pl.BlockSpec((1, 6, d), lambda b, i: (b, 0, 0)),
                  pl.BlockSpec((1, d), c2),
                  wspec(w1), wspec(w3), wspec(w2)],
        out_specs=row_spec(d),
        out_shape=jax.ShapeDtypeStruct((bsz, seq, d), F32),
        compiler_params=pltpu.CompilerParams(
            dimension_semantics=("arbitrary", "arbitrary"), vmem_limit_bytes=VMEM_LIMIT),
        name="outproj_ffn",
    )(ya, bg, u, u, conv_w, ga, gc, w_out, x, mod3, g2, w1, w3, w2)


def _t5_bucket_np(rel):
    n = np.maximum(rel, 0)
    max_exact = N_BUCKETS // 2
    nf = np.maximum(n, 1).astype(np.float32)
    large = max_exact + (np.log(nf / max_exact) / math.log(MAX_DISTANCE / max_exact)
                         * (N_BUCKETS - max_exact)).astype(np.int32)
    return np.where(n < max_exact, n, np.minimum(large, N_BUCKETS - 1)).astype(np.int32)


def _bias_of_rel(tab_rel, rel):
    idx = jnp.asarray(_t5_bucket_np(rel))
    tab_b = tab_rel.T.reshape((N_HEADS, N_BUCKETS) + (1,) * idx.ndim)
    vals = jnp.zeros((N_HEADS,) + idx.shape, F32)
    for k in range(N_BUCKETS):
        vals = jnp.where(idx == k, tab_b[:, k], vals)
    return jnp.where(jnp.asarray(rel >= 0), vals, NEG_INF)


def _toeplitz_offsets(length, n_cols):
    k = np.arange(length)
    return np.where(k < n_cols, k, k - length)


TOEPLITZ_LEN = 2 * TQ
PATTERN_LEN = 3 * LANES


def _tables_kernel(v0_ref, v1_ref, vp_ref, b0_ref, b1_ref, cb_ref):
    def toeplitz_tile(v_ref):
        rows = jnp.broadcast_to(v_ref[0], (TQ, TOEPLITZ_LEN))
        return pltpu.roll(rows, 0, 1, stride=1, stride_axis=0)[:, :TQ]

    b0_ref[0] = toeplitz_tile(v0_ref)
    b1_ref[0] = toeplitz_tile(v1_ref)
    vp = vp_ref[0]
    shift = TQ // CMP_STRIDE
    pat = jnp.concatenate(
        [vp[:, :2 * N_CMP_PAD]] + [pltpu.roll(vp, a, 1)[:, :2 * N_CMP_PAD] for a in range(1, shift)],
        axis=0)
    for qi in range(cb_ref.shape[1]):
        cb_ref[0, qi] = pat[:, N_CMP_PAD - qi * shift:2 * N_CMP_PAD - qi * shift].T


def _bias_tables(rel_bias_table, seq):
    tab_rel = (rel_bias_table - rel_bias_table[N_BUCKETS - 1:N_BUCKETS, :]) * LOG2E
    nq = seq // TQ
    n_cmp = (seq - CMP_BLOCK) // CMP_STRIDE + 1
    shift = TQ // CMP_STRIDE
    assert nq * shift <= N_CMP_PAD and n_cmp * CMP_STRIDE + CMP_BLOCK - 1 > seq
    assert shift + 2 * N_CMP_PAD - 1 <= PATTERN_LEN
    d = _toeplitz_offsets(TOEPLITZ_LEN, TQ)
    v0 = _bias_of_rel(tab_rel, -d)[:, None, :]
    v1 = _bias_of_rel(tab_rel, TQ - d)[:, None, :]
    off = N_CMP_PAD * CMP_STRIDE - (CMP_BLOCK - 1)
    dp = _toeplitz_offsets(PATTERN_LEN, 2 * N_CMP_PAD)[None, :]
    vp = _bias_of_rel(tab_rel, np.arange(CMP_STRIDE)[:, None] + off - CMP_STRIDE * dp)
    head = lambda h: (h // HPG, h % HPG, 0)
    b0, b1, cb = pl.pallas_call(
        _tables_kernel,
        grid=(N_HEADS,),
        in_specs=[pl.BlockSpec((1, 1, TOEPLITZ_LEN), lambda h: (h, 0, 0)),
                  pl.BlockSpec((1, 1, TOEPLITZ_LEN), lambda h: (h, 0, 0)),
                  pl.BlockSpec((1, CMP_STRIDE, PATTERN_LEN), lambda h: (h, 0, 0))],
        out_specs=[pl.BlockSpec((1, TQ, TQ), head), pl.BlockSpec((1, TQ, TQ), head),
                   pl.BlockSpec((1, nq, N_CMP_PAD, TQ), lambda h: (h // HPG, 0, 0, h % HPG))],
        out_shape=[jax.ShapeDtypeStruct((N_GROUPS, ROWS, TQ), F32),
                   jax.ShapeDtypeStruct((N_GROUPS, ROWS, TQ), F32),
                   jax.ShapeDtypeStruct((N_GROUPS, nq, N_CMP_PAD, ROWS), F32)],
        compiler_params=pltpu.CompilerParams(
            dimension_semantics=("arbitrary",), vmem_limit_bytes=VMEM_LIMIT),
        name="bias_tables",
    )(v0, v1, vp)
    i = np.arange(TQ)[:, None]
    jj = np.arange(TQ)[None, :]
    w2 = jnp.asarray(np.tile(np.where(jj > i, 0.0, NEG_INF).astype(np.float32), (HPG, 1)))
    n_sel = seq // SEL_BLOCK
    cs = np.arange(n_cmp) * CMP_STRIDE
    ce = cs + CMP_BLOCK - 1
    ss = np.arange(n_sel) * SEL_BLOCK
    ov = np.clip(np.minimum(ce[:, None], ss[None, :] + SEL_BLOCK - 1)
                 - np.maximum(cs[:, None], ss[None, :]) + 1, 0, None) / CMP_STRIDE
    ovlt = np.zeros((LANES, N_CMP_PAD), np.float32)
    ovlt[SEL_LANE0:SEL_LANE0 + n_sel, :n_cmp] = ov.T
    return tab_rel, cb, b0, b1, w2, jnp.asarray(ovlt, BF16)


def kernel(x, c, w_ada, b_ada, norm1_gain, w_in, q_gain, k_cmp_gain, k_sel_gain, k_win_gain,
           cmp_pos_k, cmp_pos_v, w_ck1, w_ck2, w_cv1, w_cv2, rel_bias_table, conv_w,
           attn_out_gain, conv_out_gain, w_out, norm2_gain, w_ff1, w_ff3, w_ff2):
    bsz, seq, d = x.shape
    assert d == D_MODEL and seq % TQ == 0 and SEL_LANE0 + seq // SEL_BLOCK <= SHIFT_LANE
    assert (seq - CMP_BLOCK) // CMP_STRIDE + 1 <= N_CMP_PAD and seq // CMP_STRIDE == N_CMP_PAD

    seg_of = lambda n: jnp.asarray(np.kron(np.eye(n), np.ones((HEAD_DIM, HEAD_DIM))) / HEAD_DIM, BF16)
    seg4 = seg_of(MXU_N // HEAD_DIM)
    tab_rel, cb, b0, b1, w2m, ovlt = _bias_tables(rel_bias_table, seq)
    bias_max = jnp.max(jnp.abs(tab_rel))
    two = lambda g: jnp.tile(g.reshape(1, HEAD_DIM), (1, 2))
    n_gate = N_BRANCH * HPG
    gate_pad = jnp.zeros((d, LANES - n_gate), BF16)

    for layer in range(w_in.shape[0]):
        wi = w_in[layer].astype(BF16)
        o_g = D_ATTN + 6 * D_KV
        kv = lambda n: wi[:, D_ATTN + n * D_KV:D_ATTN + (n + 1) * D_KV]
        w_p = jnp.concatenate(
            [wi[:, :D_ATTN], kv(0), kv(1), kv(2), kv(4), kv(3), kv(5),
             wi[:, o_g:o_g + n_gate], gate_pad,
             wi[:, o_g + n_gate:o_g + 2 * n_gate], gate_pad, wi[:, o_g + 2 * n_gate:]],
            axis=-1)
        w1k, w1v = w_ck1[layer].astype(BF16), w_cv1[layer].astype(BF16)
        w2k, w2v = w_ck2[layer].astype(BF16), w_cv2[layer].astype(BF16)
        pk2 = cmp_pos_k[layer].reshape(2, CMP_STRIDE * HEAD_DIM)
        pv2 = cmp_pos_v[layer].reshape(2, CMP_STRIDE * HEAD_DIM)

        gq_max = jnp.max(jnp.abs(q_gain[layer]))
        bound = lambda gk: (HEAD_DIM ** 0.5 * LOG2E * 1.01) * gq_max * jnp.max(jnp.abs(gk)) + bias_max
        c_sel, c_win = bound(k_sel_gain[layer]), bound(k_win_gain[layer])
        flag = (2.0 * jnp.maximum(c_sel, c_win) <= MAX_SHIFTED_RANGE).astype(jnp.int32).reshape(1)
        tails = jnp.zeros((2, HEAD_DIM), F32).at[:, SHIFT_LANE - HEAD_DIM].set(
            -jnp.stack([c_sel, c_win]))

        mod3 = _adaln(c, w_ada[layer], b_ada[layer]).reshape(bsz, 6, d)
        q, kvc, ks, vs, kw, vw, gate, bg, u = _inproj(
            x, mod3, norm1_gain[layer].reshape(1, d), w_p, seg4,
            jnp.tile(q_gain[layer].reshape(1, HEAD_DIM), (1, MXU_N // HEAD_DIM)),
            jnp.concatenate([two(k_sel_gain[layer]), two(k_win_gain[layer])], axis=-1), tails)
        kct, vc = _compress(kvc, w1k, w1v, w2k, w2v, pk2, pv2,
                            k_cmp_gain[layer].reshape(1, HEAD_DIM))
        y_attn = _attn(flag, q, kct, vc, ks, vs, kw, vw, gate, cb, b0, b1, w2m, ovlt)
        x = _outproj_ffn(y_attn, bg, u, conv_w[layer], attn_out_gain[layer].reshape(1, D_ATTN),
                         conv_out_gain[layer].reshape(1, D_CONV), w_out[layer].astype(BF16), x, mod3,
                         norm2_gain[layer].reshape(1, d), w_ff1[layer].astype(BF16),
                         w_ff3[layer].astype(BF16), w_ff2[layer].astype(BF16))
    return x
```
